```python
import math, functools
import jax, jax.numpy as jnp
from jax import lax
import numpy as np

D_MODEL = 1024
BATCH = 8
SEQ = 2048
DEPTH = 4

N_MIXERS = 3
POOL_WINDOWS = (2, 4, 8, 16)
POOL_GROUPS = len(POOL_WINDOWS)
POOL_GROUP_DIM = D_MODEL // POOL_GROUPS
ATTN_PATTERNS = ((128, 1), (512, 4), (2048, 16))
ATTN_GROUPS = len(ATTN_PATTERNS)
HEAD_DIM = 64
HEADS_PER_GROUP = D_MODEL // HEAD_DIM
ATTN_INNER = HEADS_PER_GROUP * HEAD_DIM
ROT_DIM = HEAD_DIM // 4
ROPE_THETA = 500000.0
CONV_WIDTH = 3
D_FF = 2816
N_EXPERTS = 8
TOP_K = 2
D_FF_EXPERT = 3584
DN_ALPHA = (2 * DEPTH) ** 0.25
DN_BETA = (8 * DEPTH) ** -0.25
LN_EPS = 1e-5

kernel_name = "hybrid_pool_dilattn_shortconv_moe_deepnorm"


def layer_norm(x, g, b):
    xf = x.astype(jnp.float32)
    mu = jnp.mean(xf, axis=-1, keepdims=True)
    var = jnp.mean(jnp.square(xf - mu), axis=-1, keepdims=True)
    y = (xf - mu) * lax.rsqrt(var + LN_EPS) * g.astype(jnp.float32) + b.astype(jnp.float32)
    return y.astype(x.dtype)


def causal_mean_minus_self(u, window):
    S = u.shape[1]
    c = jnp.cumsum(u.astype(jnp.float32), axis=1)
    c = jnp.pad(c, ((0, 0), (1, 0), (0, 0)))
    t = jnp.arange(S)
    lo = jnp.maximum(t + 1 - window, 0)
    total = c[:, 1:] - c[:, lo]
    cnt = jnp.minimum(t + 1, window).astype(jnp.float32)[None, :, None]
    return (total / cnt - u.astype(jnp.float32)).astype(u.dtype)


def pool_mixer(x, w_in, w_grp, scale):
    B, S, _ = x.shape
    u = (x @ w_in).reshape(B, S, POOL_GROUPS, POOL_GROUP_DIM)
    pooled = jnp.stack([causal_mean_minus_self(u[:, :, g], w) for g, w in enumerate(POOL_WINDOWS)], axis=2)
    y = jnp.einsum('bsgc,gcd->bsgd', pooled, w_grp).reshape(B, S, D_MODEL)
    return y * scale


def rope_tables(positions):
    inv_freq = ROPE_THETA ** (-(jnp.arange(0, ROT_DIM, 2, dtype=jnp.float32) / ROT_DIM))
    ang = positions.astype(jnp.float32)[..., None] * inv_freq
    return jnp.cos(ang), jnp.sin(ang)


def apply_partial_rope(t, cos, sin):
    half = ROT_DIM // 2
    c = cos[:, :, None, :].astype(t.dtype)
    s = sin[:, :, None, :].astype(t.dtype)
    r1, r2, rest = t[..., :half], t[..., half:ROT_DIM], t[..., ROT_DIM:]
    return jnp.concatenate([r1 * c - r2 * s, r1 * s + r2 * c, rest], axis=-1)


def dilated_window_attention(q, k, v, window, dilation):
    B, S, H, Dh = q.shape
    L = S // dilation
    nk = window // dilation
    blk = nk
    nb = -(-L // blk)
    Lp = nb * blk

    def to_blocks(t):
        t = t.reshape(B, L, dilation, H, Dh).transpose(0, 2, 1, 3, 4)
        t = jnp.pad(t, ((0, 0), (0, 0), (0, Lp - L), (0, 0), (0, 0)))
        return t.reshape(B, dilation, nb, blk, H, Dh)

    def with_prev(t):
        prev = jnp.pad(t, ((0, 0), (0, 0), (1, 0), (0, 0), (0, 0), (0, 0)))[:, :, :-1]
        return jnp.concatenate([prev, t], axis=3)

    qb = to_blocks(q)
    kk = with_prev(to_blocks(k))
    vv = with_prev(to_blocks(v))
    scores = jnp.einsum('brnqhd,brnkhd->brnhqk', qb, kk).astype(jnp.float32) * (Dh ** -0.5)
    qi = jnp.arange(nb)[:, None] * blk + jnp.arange(blk)[None, :]
    ki = jnp.arange(nb)[:, None] * blk - blk + jnp.arange(2 * blk)[None, :]
    rel = qi[:, :, None] - ki[:, None, :]
    valid = (rel >= 0) & (rel <= nk) & (ki[:, None, :] >= 0)
    scores = jnp.where(valid[None, None, :, None], scores, -jnp.inf)
    lse = jax.nn.logsumexp(scores, axis=-1)
    p = jnp.exp(scores - lse[..., None])
    out = jnp.einsum('brnhqk,brnkhd->brnqhd', p.astype(v.dtype), vv)
    out = out.reshape(B, dilation, Lp, H, Dh)[:, :, :L].transpose(0, 2, 1, 3, 4).reshape(B, S, H, Dh)
    lse = lse.transpose(0, 1, 2, 4, 3).reshape(B, dilation, Lp, H)[:, :, :L]
    lse = lse.transpose(0, 2, 1, 3).reshape(B, S, H)
    return out, lse


def dilated_attention_mixer(x, cos, sin, w_qkv, w_o):
    B, S, _ = x.shape
    qkv = (x @ w_qkv).reshape(B, S, ATTN_GROUPS, 3, HEADS_PER_GROUP, HEAD_DIM)
    outs, lses = [], []
    for g, (win, dil) in enumerate(ATTN_PATTERNS):
        q = apply_partial_rope(qkv[:, :, g, 0], cos, sin)
        k = apply_partial_rope(qkv[:, :, g, 1], cos, sin)
        o, l = dilated_window_attention(q, k, qkv[:, :, g, 2], win, dil)
        outs.append(o.astype(jnp.float32))
        lses.append(l)
    wts = jax.nn.softmax(jnp.stack(lses, axis=0), axis=0)
    o = jnp.sum(wts[..., None] * jnp.stack(outs, axis=0), axis=0)
    return o.reshape(B, S, ATTN_INNER).astype(x.dtype) @ w_o


def short_conv_mixer(x, w_in, conv_w, w_out):
    S = x.shape[1]
    gate_b, gate_c, u = jnp.split(x @ w_in, 3, axis=-1)
    z = gate_c * u
    zp = jnp.pad(z, ((0, 0), (CONV_WIDTH - 1, 0), (0, 0)))
    conv = sum(conv_w[j] * zp[:, CONV_WIDTH - 1 - j: CONV_WIDTH - 1 - j + S] for j in range(CONV_WIDTH))
    return (gate_b * conv) @ w_out


def swiglu(x, w_gate, w_up, w_down):
    return (jax.nn.silu(x @ w_gate) * (x @ w_up)) @ w_down


def moe_swiglu(x, w_router, w_gate, w_up, w_down):
    logits = (x @ w_router).astype(jnp.float32)
    top_vals, top_idx = lax.top_k(logits, TOP_K)
    gates = jax.nn.softmax(top_vals, axis=-1)
    comb = jnp.sum(jax.nn.one_hot(top_idx, N_EXPERTS, dtype=jnp.float32) * gates[..., None], axis=-2)
    comb = comb.astype(x.dtype)
    out = jnp.zeros_like(x)
    for e in range(N_EXPERTS):
        out = out + comb[..., e:e + 1] * swiglu(x, w_gate[e], w_up[e], w_down[e])
    return out


def _normal(key, shape, scale):
    return jax.random.normal(key, shape, jnp.float32) * scale


def _ln_params(key, pfx):
    k1, k2 = jax.random.split(key)
    return {pfx + "_g": 1.0 + _normal(k1, (D_MODEL,), 0.02), pfx + "_b": _normal(k2, (D_MODEL,), 0.02)}


def _pool_params(key, pfx):
    k1, k2, k3 = jax.random.split(key, 3)
    return {pfx + "_w_in": _normal(k1, (D_MODEL, D_MODEL), D_MODEL ** -0.5),
            pfx + "_w_grp": _normal(k2, (POOL_GROUPS, POOL_GROUP_DIM, POOL_GROUP_DIM), POOL_GROUP_DIM ** -0.5 * DN_BETA),
            pfx + "_scale": 1.0 + _normal(k3, (D_MODEL,), 0.02)}


def _attn_params(key, pfx):
    k1, k2 = jax.random.split(key)
    qkv_scale = jnp.array([1.0, 1.0, DN_BETA], jnp.float32)[None, None, :, None] * D_MODEL ** -0.5
    w_qkv = _normal(k1, (D_MODEL, ATTN_GROUPS, 3, ATTN_INNER), 1.0) * qkv_scale
    return {pfx + "_w_qkv": w_qkv.reshape(D_MODEL, ATTN_GROUPS * 3 * ATTN_INNER),
            pfx + "_w_o": _normal(k2, (ATTN_INNER, D_MODEL), ATTN_INNER ** -0.5 * DN_BETA)}


def _conv_params(key, pfx):
    k1, k2, k3 = jax.random.split(key, 3)
    return {pfx + "_w_in": _normal(k1, (D_MODEL, 3 * D_MODEL), D_MODEL ** -0.5),
            pfx + "_w": _normal(k2, (CONV_WIDTH, D_MODEL), CONV_WIDTH ** -0.5),
            pfx + "_w_out": _normal(k3, (D_MODEL, D_MODEL), D_MODEL ** -0.5 * DN_BETA)}


def _ffn_params(key, pfx):
    k1, k2, k3 = jax.random.split(key, 3)
    return {pfx + "_w_gate": _normal(k1, (D_MODEL, D_FF), D_MODEL ** -0.5),
            pfx + "_w_up": _normal(k2, (D_MODEL, D_FF), D_MODEL ** -0.5),
            pfx + "_w_down": _normal(k3, (D_FF, D_MODEL), D_FF ** -0.5 * DN_BETA)}


def _moe_params(key, pfx):
    k0, k1, k2, k3 = jax.random.split(key, 4)
    return {pfx + "_w_router": _normal(k0, (D_MODEL, N_EXPERTS), D_MODEL ** -0.5),
            pfx + "_w_gate": _normal(k1, (N_EXPERTS, D_MODEL, D_FF_EXPERT), D_MODEL ** -0.5),
            pfx + "_w_up": _normal(k2, (N_EXPERTS, D_MODEL, D_FF_EXPERT), D_MODEL ** -0.5),
            pfx + "_w_down": _normal(k3, (N_EXPERTS, D_FF_EXPERT, D_MODEL), D_FF_EXPERT ** -0.5 * DN_BETA)}


def setup_inputs(seed: int = 0) -> dict:
    key = jax.random.key(seed)
    ks = jax.random.split(key, 2 + 4 * DEPTH)
    d = {}
    d["x"] = jax.random.normal(ks[0], (BATCH, SEQ, D_MODEL), jnp.float32)
    offsets = jax.random.randint(ks[1], (BATCH, 1), 0, 4096, dtype=jnp.int32)
    d["positions"] = jnp.arange(SEQ, dtype=jnp.int32)[None, :] + offsets
    mixer_makers = (_pool_params, _attn_params, _conv_params)
    mixer_names = ("pool", "attn", "conv")
    for i in range(DEPTH):
        km, kl1, kf, kl2 = ks[2 + 4 * i: 6 + 4 * i]
        kind = i % N_MIXERS
        d.update(mixer_makers[kind](km, "l%d_%s" % (i, mixer_names[kind])))
        d.update(_ln_params(kl1, "l%d_ln1" % i))
        if i % 2 == 0:
            d.update(_ffn_params(kf, "l%d_ffn" % i))
        else:
            d.update(_moe_params(kf, "l%d_moe" % i))
        d.update(_ln_params(kl2, "l%d_ln2" % i))
    return d


def reference(x, positions,
              l0_pool_w_in, l0_pool_w_grp, l0_pool_scale, l0_ln1_g, l0_ln1_b,
              l0_ffn_w_gate, l0_ffn_w_up, l0_ffn_w_down, l0_ln2_g, l0_ln2_b,
              l1_attn_w_qkv, l1_attn_w_o, l1_ln1_g, l1_ln1_b,
              l1_moe_w_router, l1_moe_w_gate, l1_moe_w_up, l1_moe_w_down, l1_ln2_g, l1_ln2_b,
              l2_conv_w_in, l2_conv_w, l2_conv_w_out, l2_ln1_g, l2_ln1_b,
              l2_ffn_w_gate, l2_ffn_w_up, l2_ffn_w_down, l2_ln2_g, l2_ln2_b,
              l3_pool_w_in, l3_pool_w_grp, l3_pool_scale, l3_ln1_g, l3_ln1_b,
              l3_moe_w_router, l3_moe_w_gate, l3_moe_w_up, l3_moe_w_down, l3_ln2_g, l3_ln2_b):
    cos, sin = rope_tables(positions)
    mixer_params = [(l0_pool_w_in, l0_pool_w_grp, l0_pool_scale),
                    (l1_attn_w_qkv, l1_attn_w_o),
                    (l2_conv_w_in, l2_conv_w, l2_conv_w_out),
                    (l3_pool_w_in, l3_pool_w_grp, l3_pool_scale)]
    ln1_params = [(l0_ln1_g, l0_ln1_b), (l1_ln1_g, l1_ln1_b), (l2_ln1_g, l2_ln1_b), (l3_ln1_g, l3_ln1_b)]
    ffn_params = [(l0_ffn_w_gate, l0_ffn_w_up, l0_ffn_w_down),
                  (l1_moe_w_router, l1_moe_w_gate, l1_moe_w_up, l1_moe_w_down),
                  (l2_ffn_w_gate, l2_ffn_w_up, l2_ffn_w_down),
                  (l3_moe_w_router, l3_moe_w_gate, l3_moe_w_up, l3_moe_w_down)]
    ln2_params = [(l0_ln2_g, l0_ln2_b), (l1_ln2_g, l1_ln2_b), (l2_ln2_g, l2_ln2_b), (l3_ln2_g, l3_ln2_b)]
    for i in range(DEPTH):
        kind = i % N_MIXERS
        if kind == 0:
            h = pool_mixer(x, *mixer_params[i])
        elif kind == 1:
            h = dilated_attention_mixer(x, cos, sin, *mixer_params[i])
        else:
            h = short_conv_mixer(x, *mixer_params[i])
        x = layer_norm(DN_ALPHA * x + h, *ln1_params[i])
        if i % 2 == 0:
            f = swiglu(x, *ffn_params[i])
        else:
            f = moe_swiglu(x, *ffn_params[i])
        x = layer_norm(DN_ALPHA * x + f, *ln2_params[i])
    return x
```

```python
import functools

import jax
import jax.numpy as jnp
from jax import lax
from jax.experimental import pallas as pl
from jax.experimental.pallas import tpu as pltpu

D_MODEL = 1024
SEQ = 2048
DEPTH = 4
POOL_WINDOWS = (2, 4, 8, 16)
POOL_GROUP_DIM = D_MODEL // len(POOL_WINDOWS)
ATTN_DILATIONS = (1, 4, 16)
ATTN_BLOCK = 128
HEAD_DIM = 64
ROT_DIM = HEAD_DIM // 4
ROPE_THETA = 500000.0
CONV_WIDTH = 3
N_EXPERTS = 8
DN_ALPHA = (2 * DEPTH) ** 0.25
LN_EPS = 1e-5

LANES = 128
HALO = 16
NEG_BIG = -1e30
VMEM_LIMIT = 56 * 1024 * 1024

F32 = jnp.float32
BF16 = jnp.bfloat16


def _params(semantics, vmem=VMEM_LIMIT):
    return pltpu.CompilerParams(dimension_semantics=semantics, vmem_limit_bytes=vmem)


def _dot(a, b):
    return jnp.dot(a, b, preferred_element_type=F32)


def _layer_norm(z, g, b):
    mu = jnp.mean(z, axis=-1, keepdims=True)
    zc = z - mu
    var = jnp.mean(zc * zc, axis=-1, keepdims=True)
    return zc * lax.rsqrt(var + LN_EPS) * g + b


def _row(v):
    return v.reshape(1, -1)


def _pool_kernel(x_ref, w_in_ref, w_grp_ref, scale_ref, g_ref, b_ref, o_ref, halo_ref, *, tm, tiles_per_seq):
    i = pl.program_id(0)
    x = x_ref[...]
    u = _dot(x.astype(BF16), w_in_ref[...])

    @pl.when(i % tiles_per_seq == 0)
    def _():
        halo_ref[...] = jnp.zeros_like(halo_ref)

    buf = jnp.concatenate([halo_ref[...], u], axis=0)
    halo_ref[...] = u[tm - HALO:, :]
    t = (i % tiles_per_seq) * tm + lax.broadcasted_iota(jnp.int32, (tm, 1), 0)
    outs = []
    for grp, w in enumerate(POOL_WINDOWS):
        cols = slice(grp * POOL_GROUP_DIM, (grp + 1) * POOL_GROUP_DIM)
        s = buf[:, cols]
        k = 1
        while k < w:
            s = s + pltpu.roll(s, k, 0)
            k *= 2
        cnt = jnp.minimum(t + 1, w).astype(F32)
        pooled = s[HALO:, :] / cnt - u[:, cols]
        outs.append(_dot(pooled.astype(BF16), w_grp_ref[grp]))
    h = jnp.concatenate(outs, axis=1) * scale_ref[...]
    o_ref[...] = _layer_norm(DN_ALPHA * x + h, g_ref[...], b_ref[...])


def _pool_layer(x, w_in, w_grp, scale, g, b, *, tm=512):
    n = x.shape[0]
    kern = functools.partial(_pool_kernel, tm=tm, tiles_per_seq=SEQ // tm)
    vec = pl.BlockSpec((1, D_MODEL), lambda i: (0, 0))
    return pl.pallas_call(
        kern,
        out_shape=jax.ShapeDtypeStruct((n, D_MODEL), F32),
        grid=(n // tm,),
        in_specs=[pl.BlockSpec((tm, D_MODEL), lambda i: (i, 0)),
                  pl.BlockSpec((D_MODEL, D_MODEL), lambda i: (0, 0)),
                  pl.BlockSpec((len(POOL_WINDOWS), POOL_GROUP_DIM, POOL_GROUP_DIM), lambda i: (0, 0, 0)),
                  vec, vec, vec],
        out_specs=pl.BlockSpec((tm, D_MODEL), lambda i: (i, 0)),
        scratch_shapes=[pltpu.VMEM((HALO, D_MODEL), F32)],
        compiler_params=_params(("arbitrary",)),
        name="pool_mixer_ln",
    )(x, w_in.astype(BF16), w_grp.astype(BF16), _row(scale), _row(g), _row(b))


def _conv_kernel(x_ref, w_in_ref, cw_ref, w_out_ref, g_ref, b_ref, o_ref, halo_ref, *, tm, tiles_per_seq):
    i = pl.program_id(0)
    x = x_ref[...]
    proj = _dot(x.astype(BF16), w_in_ref[...])
    gate_b = proj[:, :D_MODEL]
    z = proj[:, D_MODEL:2 * D_MODEL] * proj[:, 2 * D_MODEL:]

    @pl.when(i % tiles_per_seq == 0)
    def _():
        halo_ref[...] = jnp.zeros_like(halo_ref)

    buf = jnp.concatenate([halo_ref[...], z], axis=0)
    halo_ref[...] = z[tm - HALO:, :]
    conv = cw_ref[0:1, :] * z
    for j in range(1, CONV_WIDTH):
        conv = conv + cw_ref[j:j + 1, :] * pltpu.roll(buf, j, 0)[HALO:, :]
    h = _dot((gate_b * conv).astype(BF16), w_out_ref[...])
    o_ref[...] = _layer_norm(DN_ALPHA * x + h, g_ref[...], b_ref[...])


def _conv_layer(x, w_in, conv_w, w_out, g, b, *, tm=512):
    n = x.shape[0]
    kern = functools.partial(_conv_kernel, tm=tm, tiles_per_seq=SEQ // tm)
    vec = pl.BlockSpec((1, D_MODEL), lambda i: (0, 0))
    return pl.pallas_call(
        kern,
        out_shape=jax.ShapeDtypeStruct((n, D_MODEL), F32),
        grid=(n // tm,),
        in_specs=[pl.BlockSpec((tm, D_MODEL), lambda i: (i, 0)),
                  pl.BlockSpec((D_MODEL, 3 * D_MODEL), lambda i: (0, 0)),
                  pl.BlockSpec((CONV_WIDTH, D_MODEL), lambda i: (0, 0)),
                  pl.BlockSpec((D_MODEL, D_MODEL), lambda i: (0, 0)),
                  vec, vec],
        out_specs=pl.BlockSpec((tm, D_MODEL), lambda i: (i, 0)),
        scratch_shapes=[pltpu.VMEM((HALO, D_MODEL), F32)],
        compiler_params=_params(("arbitrary",)),
        name="conv_mixer_ln",
    )(x, w_in.astype(BF16), conv_w, w_out.astype(BF16), _row(g), _row(b))


def _ffn_kernel(x_ref, wg_ref, wu_ref, wd_ref, g_ref, b_ref, o_ref, xb_ref, acc_ref):
    f = pl.program_id(1)

    @pl.when(f == 0)
    def _():
        xb_ref[...] = x_ref[...].astype(BF16)
        acc_ref[...] = jnp.zeros_like(acc_ref)

    xb = xb_ref[...]
    gate = _dot(xb, wg_ref[...])
    up = _dot(xb, wu_ref[...])
    h = gate * jax.nn.sigmoid(gate) * up
    acc_ref[...] += _dot(h.astype(BF16), wd_ref[...])

    @pl.when(f == pl.num_programs(1) - 1)
    def _():
        o_ref[...] = _layer_norm(DN_ALPHA * x_ref[...] + acc_ref[...], g_ref[...], b_ref[...])


def _ffn_layer(x, w_gate, w_up, w_down, g, b, *, tm=1024, tf=256):
    n = x.shape[0]
    d_ff = w_gate.shape[1]
    vec = pl.BlockSpec((1, D_MODEL), lambda i, f: (0, 0))
    return pl.pallas_call(
        _ffn_kernel,
        out_shape=jax.ShapeDtypeStruct((n, D_MODEL), F32),
        grid=(n // tm, d_ff // tf),
        in_specs=[pl.BlockSpec((tm, D_MODEL), lambda i, f: (i, 0)),
                  pl.BlockSpec((D_MODEL, tf), lambda i, f: (0, f)),
                  pl.BlockSpec((D_MODEL, tf), lambda i, f: (0, f)),
                  pl.BlockSpec((tf, D_MODEL), lambda i, f: (f, 0)),
                  vec, vec],
        out_specs=pl.BlockSpec((tm, D_MODEL), lambda i, f: (i, 0)),
        scratch_shapes=[pltpu.VMEM((tm, D_MODEL), BF16), pltpu.VMEM((tm, D_MODEL), F32)],
        compiler_params=_params(("arbitrary", "arbitrary")),
        name="swiglu_ln",
    )(x, w_gate.astype(BF16), w_up.astype(BF16), w_down.astype(BF16), _row(g), _row(b))


def _router_kernel(x_ref, w_ref, comb_ref):
    logits = jnp.dot(x_ref[...], w_ref[...], preferred_element_type=F32, precision=lax.Precision.HIGHEST)
    lane = lax.broadcasted_iota(jnp.int32, logits.shape, 1)
    logits = jnp.where(lane < N_EXPERTS, logits, -jnp.inf)
    v1 = jnp.max(logits, axis=1, keepdims=True)
    i1 = jnp.min(jnp.where(logits == v1, lane, LANES), axis=1, keepdims=True)
    rest = jnp.where(lane == i1, -jnp.inf, logits)
    v2 = jnp.max(rest, axis=1, keepdims=True)
    i2 = jnp.min(jnp.where(rest == v2, lane, LANES), axis=1, keepdims=True)
    e2 = jnp.exp(v2 - v1)
    g1 = 1.0 / (1.0 + e2)
    g2 = e2 / (1.0 + e2)
    comb_ref[...] = jnp.where(lane == i1, g1, 0.0) + jnp.where(lane == i2, g2, 0.0)


def _router(x, w_router, *, tm=1024):
    n = x.shape[0]
    w = jnp.pad(w_router, ((0, 0), (0, LANES - N_EXPERTS)))
    return pl.pallas_call(
        _router_kernel,
        out_shape=jax.ShapeDtypeStruct((n, LANES), F32),
        grid=(n // tm,),
        in_specs=[pl.BlockSpec((tm, D_MODEL), lambda i: (i, 0)),
                  pl.BlockSpec((D_MODEL, LANES), lambda i: (0, 0))],
        out_specs=pl.BlockSpec((tm, LANES), lambda i: (i, 0)),
        compiler_params=_params(("arbitrary",)),
        name="router_top2",
    )(x, w)


def _moe_kernel(x_ref, comb_ref, wg_ref, wu_ref, wd_ref, g_ref, b_ref, o_ref, xb_ref, acc_ref):
    e = pl.program_id(1)
    f = pl.program_id(2)

    @pl.when((e == 0) & (f == 0))
    def _():
        xb_ref[...] = x_ref[...].astype(BF16)
        acc_ref[...] = jnp.zeros_like(acc_ref)

    comb = comb_ref[...]
    lane = lax.broadcasted_iota(jnp.int32, comb.shape, 1)
    ce = jnp.sum(jnp.where(lane == e, comb, 0.0), axis=1, keepdims=True)
    xb = xb_ref[...]
    gate = _dot(xb, wg_ref[...])
    up = _dot(xb, wu_ref[...])
    h = gate * jax.nn.sigmoid(gate) * up * ce
    acc_ref[...] += _dot(h.astype(BF16), wd_ref[...])

    @pl.when((e == pl.num_programs(1) - 1) & (f == pl.num_programs(2) - 1))
    def _():
        o_ref[...] = _layer_norm(DN_ALPHA * x_ref[...] + acc_ref[...], g_ref[...], b_ref[...])


def _moe_layer(x, w_router, w_gate, w_up, w_down, g, b, *, tm=1024, tf=256):
    n = x.shape[0]
    d_ff = w_gate.shape[2]
    comb = _router(x, w_router)
    vec = pl.BlockSpec((1, D_MODEL), lambda i, e, f: (0, 0))
    return pl.pallas_call(
        _moe_kernel,
        out_shape=jax.ShapeDtypeStruct((n, D_MODEL), F32),
        grid=(n // tm, N_EXPERTS, d_ff // tf),
        in_specs=[pl.BlockSpec((tm, D_MODEL), lambda i, e, f: (i, 0)),
                  pl.BlockSpec((tm, LANES), lambda i, e, f: (i, 0)),
                  pl.BlockSpec((None, D_MODEL, tf), lambda i, e, f: (e, 0, f)),
                  pl.BlockSpec((None, D_MODEL, tf), lambda i, e, f: (e, 0, f)),
                  pl.BlockSpec((None, tf, D_MODEL), lambda i, e, f: (e, f, 0)),
                  vec, vec],
        out_specs=pl.BlockSpec((tm, D_MODEL), lambda i, e, f: (i, 0)),
        scratch_shapes=[pltpu.VMEM((tm, D_MODEL), BF16), pltpu.VMEM((tm, D_MODEL), F32)],
        compiler_params=_params(("arbitrary", "arbitrary", "arbitrary")),
        name="moe_swiglu_ln",
    )(x, comb, w_gate.astype(BF16), w_up.astype(BF16), w_down.astype(BF16), _row(g), _row(b))


def _rope_kernel(pos_ref, invf_ref, c_ref, s1_ref, s2_ref):
    ang = pos_ref[...].astype(F32) * invf_ref[...]
    c = jnp.cos(ang)
    s = jnp.sin(ang)
    dd = lax.broadcasted_iota(jnp.int32, ang.shape, 1) % HEAD_DIM
    c_ref[...] = c
    s1_ref[...] = jnp.where(dd < ROT_DIM // 2, -s, 0.0)
    s2_ref[...] = jnp.where((dd >= ROT_DIM // 2) & (dd < ROT_DIM), s, 0.0)


def _rope_tables(positions, *, tm=2048):
    n = positions.size
    half = ROT_DIM // 2
    inv_freq = ROPE_THETA ** (-(jnp.arange(0, ROT_DIM, 2, dtype=F32) / ROT_DIM))
    per_head = jnp.concatenate([inv_freq, inv_freq, jnp.zeros((HEAD_DIM - 2 * half,), F32)])
    invf = jnp.tile(per_head, LANES // HEAD_DIM).reshape(1, LANES)
    out = jax.ShapeDtypeStruct((n, LANES), F32)
    spec = pl.BlockSpec((tm, LANES), lambda i: (i, 0))
    return pl.pallas_call(
        _rope_kernel,
        out_shape=(out, out, out),
        grid=(n // tm,),
        in_specs=[pl.BlockSpec((tm, 1), lambda i: (i, 0)), pl.BlockSpec((1, LANES), lambda i: (0, 0))],
        out_specs=(spec, spec, spec),
        compiler_params=_params(("arbitrary",)),
        name="rope_tables",
    )(positions.reshape(n, 1), invf)


def _qkv_kernel(x_ref, c_ref, s1_ref, s2_ref, w_ref, o_ref, *, n_res, rows):
    for r in range(n_res):
        xb = x_ref[:, r * D_MODEL:(r + 1) * D_MODEL].astype(BF16)
        y = _dot(xb, w_ref[...])
        c = c_ref[:, r * LANES:(r + 1) * LANES]
        s1 = s1_ref[:, r * LANES:(r + 1) * LANES]
        s2 = s2_ref[:, r * LANES:(r + 1) * LANES]
        for part in range(2):
            scale = HEAD_DIM ** -0.5 if part == 0 else 1.0
            for blk in range(D_MODEL // LANES):
                lo = part * D_MODEL + blk * LANES
                t = y[:, lo:lo + LANES]
                rot = t * c + pltpu.roll(t, LANES - ROT_DIM // 2, 1) * s1 + pltpu.roll(t, ROT_DIM // 2, 1) * s2
                o_ref[r * rows:(r + 1) * rows, lo:lo + LANES] = (rot * scale).astype(BF16)
        o_ref[r * rows:(r + 1) * rows, 2 * D_MODEL:] = y[:, 2 * D_MODEL:].astype(BF16)


def _qkv_group(x, tabs, w, dil, *, tm=512):
    n = x.shape[0]
    seq_rows = SEQ // dil
    rows = min(tm, seq_rows)
    n_res = tm // rows
    seq_blocks = seq_rows // rows
    steps_per_seq = SEQ // tm
    assert n_res == 1 or seq_blocks == 1

    def view_map(i):
        s = i % steps_per_seq
        return (i // steps_per_seq) * seq_blocks + s % seq_blocks, s // seq_blocks

    xv = x.reshape(n // dil, dil * D_MODEL)
    tv = [t.reshape(n // dil, dil * LANES) for t in tabs]
    kern = functools.partial(_qkv_kernel, n_res=n_res, rows=rows)
    tab_spec = pl.BlockSpec((rows, n_res * LANES), view_map)
    return pl.pallas_call(
        kern,
        out_shape=jax.ShapeDtypeStruct((n, 3 * D_MODEL), BF16),
        grid=(n // tm,),
        in_specs=[pl.BlockSpec((rows, n_res * D_MODEL), view_map),
                  tab_spec, tab_spec, tab_spec,
                  pl.BlockSpec((D_MODEL, 3 * D_MODEL), lambda i: (0, 0))],
        out_specs=pl.BlockSpec((tm, 3 * D_MODEL), lambda i: (i, 0)),
        compiler_params=_params(("arbitrary",)),
        name="qkv_proj_dil%d" % dil,
    )(xv, *tv, w)


def _attn_kernel(*refs):
    qkv = refs[:9]
    o_ref = refs[9]
    acc_s, m_s, l_s = refs[10:13], refs[13:16], refs[16:19]

    lane = lax.broadcasted_iota(jnp.int32, (1, LANES), 1)
    head0 = lane < HEAD_DIM
    hm0 = head0.astype(BF16)
    hm1 = 1.0 - hm0
    qi = lax.broadcasted_iota(jnp.int32, (ATTN_BLOCK, ATTN_BLOCK), 0)
    kj = lax.broadcasted_iota(jnp.int32, (ATTN_BLOCK, ATTN_BLOCK), 1)
    cur_bias = jnp.where(kj <= qi, 0.0, NEG_BIG)
    prev_bias = jnp.where(kj >= qi, 0.0, NEG_BIG)
    bias_cur = jnp.concatenate([cur_bias, cur_bias], axis=0)
    bias_both = jnp.concatenate([jnp.concatenate([prev_bias, prev_bias], axis=0), bias_cur], axis=1)

    def one_block(grp, base, nat_start, dil, has_prev):
        q_ref, k_ref, v_ref = qkv[3 * grp:3 * grp + 3]
        q = q_ref[pl.ds(base, ATTN_BLOCK), :]
        q2 = jnp.concatenate([q * hm0, q * hm1], axis=0)
        if has_prev:
            kk = k_ref[pl.ds(base - ATTN_BLOCK, 2 * ATTN_BLOCK), :]
            vv = v_ref[pl.ds(base - ATTN_BLOCK, 2 * ATTN_BLOCK), :]
            bias = bias_both
        else:
            kk = k_ref[pl.ds(base, ATTN_BLOCK), :]
            vv = v_ref[pl.ds(base, ATTN_BLOCK), :]
            bias = bias_cur
        s = lax.dot_general(q2, kk, (((1,), (1,)), ((), ())), preferred_element_type=F32) + bias
        m = jnp.max(s, axis=1, keepdims=True)
        p = jnp.exp(s - m)
        l = jnp.sum(p, axis=1, keepdims=True)
        pb = p.astype(BF16)
        acc = _dot(pb[:ATTN_BLOCK], vv * hm0) + _dot(pb[ATTN_BLOCK:], vv * hm1)
        mb = jnp.where(head0, m[:ATTN_BLOCK], m[ATTN_BLOCK:])
        lb = jnp.where(head0, l[:ATTN_BLOCK], l[ATTN_BLOCK:])
        if dil == 1:
            rows = pl.ds(nat_start, ATTN_BLOCK)
        else:
            rows = pl.ds(nat_start, ATTN_BLOCK, stride=dil)
        acc_s[grp][rows, :] = acc
        m_s[grp][rows, :] = mb
        l_s[grp][rows, :] = lb

    for grp, dil in enumerate(ATTN_DILATIONS):
        seq_rows = SEQ // dil
        n_blocks = seq_rows // ATTN_BLOCK
        if n_blocks == 1:
            def single(r, carry, grp=grp, dil=dil):
                one_block(grp, pl.multiple_of(r * ATTN_BLOCK, ATTN_BLOCK), r, dil, False)
                return carry
            lax.fori_loop(0, dil, single, 0)
            continue
        for r in range(dil):
            one_block(grp, r * seq_rows, r, dil, False)

            def banded(nb, carry, grp=grp, dil=dil, r=r, seq_rows=seq_rows):
                base = pl.multiple_of(r * seq_rows + nb * ATTN_BLOCK, ATTN_BLOCK)
                nat = nb * (ATTN_BLOCK * dil) + r
                one_block(grp, base, pl.multiple_of(nat, ATTN_BLOCK) if dil == 1 else nat, dil, True)
                return carry
            lax.fori_loop(1, n_blocks, banded, 0)

    def merge(c, carry):
        rows = pl.ds(pl.multiple_of(c * ATTN_BLOCK, ATTN_BLOCK), ATTN_BLOCK)
        ms = [m_s[g][rows, :] for g in range(3)]
        top = jnp.maximum(jnp.maximum(ms[0], ms[1]), ms[2])
        num = jnp.zeros((ATTN_BLOCK, LANES), F32)
        den = jnp.zeros((ATTN_BLOCK, LANES), F32)
        for g in range(3):
            w = jnp.exp(ms[g] - top)
            num = num + w * acc_s[g][rows, :]
            den = den + w * l_s[g][rows, :]
        o_ref[rows, :] = (num / den).astype(BF16)
        return carry
    lax.fori_loop(0, SEQ // ATTN_BLOCK, merge, 0)


def _attention(qkvs, n):
    n_pairs = D_MODEL // LANES
    in_specs, args = [], []
    for qkv in qkvs:
        for part in range(3):
            in_specs.append(pl.BlockSpec((SEQ, LANES), lambda b, hp, part=part: (b, part * n_pairs + hp)))
            args.append(qkv)
    scratch = [pltpu.VMEM((SEQ, LANES), F32) for _ in range(9)]
    return pl.pallas_call(
        _attn_kernel,
        out_shape=jax.ShapeDtypeStruct((n, D_MODEL), BF16),
        grid=(n // SEQ, n_pairs),
        in_specs=in_specs,
        out_specs=pl.BlockSpec((SEQ, LANES), lambda b, hp: (b, hp)),
        scratch_shapes=scratch,
        compiler_params=_params(("arbitrary", "arbitrary")),
        name="dilated_attention",
    )(*args)


def _proj_ln_kernel(x_ref, a_ref, w_ref, g_ref, b_ref, o_ref):
    h = _dot(a_ref[...], w_ref[...])
    o_ref[...] = _layer_norm(DN_ALPHA * x_ref[...] + h, g_ref[...], b_ref[...])


def _proj_ln(x, a, w, g, b, *, tm=512):
    n = x.shape[0]
    vec = pl.BlockSpec((1, D_MODEL), lambda i: (0, 0))
    return pl.pallas_call(
        _proj_ln_kernel,
        out_shape=jax.ShapeDtypeStruct((n, D_MODEL), F32),
        grid=(n // tm,),
        in_specs=[pl.BlockSpec((tm, D_MODEL), lambda i: (i, 0)),
                  pl.BlockSpec((tm, D_MODEL), lambda i: (i, 0)),
                  pl.BlockSpec((D_MODEL, D_MODEL), lambda i: (0, 0)),
                  vec, vec],
        out_specs=pl.BlockSpec((tm, D_MODEL), lambda i: (i, 0)),
        compiler_params=_params(("arbitrary",)),
        name="out_proj_ln",
    )(x, a, w.astype(BF16), _row(g), _row(b))


def _attn_layer(x, tabs, w_qkv, w_o, g, b):
    n = x.shape[0]
    w = w_qkv.astype(BF16)
    qkvs = [_qkv_group(x, tabs, w[:, grp * 3 * D_MODEL:(grp + 1) * 3 * D_MODEL], dil)
            for grp, dil in enumerate(ATTN_DILATIONS)]
    return _proj_ln(x, _attention(qkvs, n), w_o, g, b)


def kernel(x, positions, l0_pool_w_in, l0_pool_w_grp, l0_pool_scale, l0_ln1_g, l0_ln1_b, l0_ffn_w_gate, l0_ffn_w_up, l0_ffn_w_down, l0_ln2_g, l0_ln2_b, l1_attn_w_qkv, l1_attn_w_o, l1_ln1_g, l1_ln1_b, l1_moe_w_router, l1_moe_w_gate, l1_moe_w_up, l1_moe_w_down, l1_ln2_g, l1_ln2_b, l2_conv_w_in, l2_conv_w, l2_conv_w_out, l2_ln1_g, l2_ln1_b, l2_ffn_w_gate, l2_ffn_w_up, l2_ffn_w_down, l2_ln2_g, l2_ln2_b, l3_pool_w_in, l3_pool_w_grp, l3_pool_scale, l3_ln1_g, l3_ln1_b, l3_moe_w_router, l3_moe_w_gate, l3_moe_w_up, l3_moe_w_down, l3_ln2_g, l3_ln2_b):
    batch, seq, d = x.shape
    h = x.reshape(batch * seq, d)
    tabs = _rope_tables(positions)
    h = _pool_layer(h, l0_pool_w_in, l0_pool_w_grp, l0_pool_scale, l0_ln1_g, l0_ln1_b)
    h = _ffn_layer(h, l0_ffn_w_gate, l0_ffn_w_up, l0_ffn_w_down, l0_ln2_g, l0_ln2_b)
    h = _attn_layer(h, tabs, l1_attn_w_qkv, l1_attn_w_o, l1_ln1_g, l1_ln1_b)
    h = _moe_layer(h, l1_moe_w_router, l1_moe_w_gate, l1_moe_w_up, l1_moe_w_down, l1_ln2_g, l1_ln2_b)
    h = _conv_layer(h, l2_conv_w_in, l2_conv_w, l2_conv_w_out, l2_ln1_g, l2_ln1_b)
    h = _ffn_layer(h, l2_ffn_w_gate, l2_ffn_w_up, l2_ffn_w_down, l2_ln2_g, l2_ln2_b)
    h = _pool_layer(h, l3_pool_w_in, l3_pool_w_grp, l3_pool_scale, l3_ln1_g, l3_ln1_b)
    h = _moe_layer(h, l3_moe_w_router, l3_moe_w_gate, l3_moe_w_up, l3_moe_w_down, l3_ln2_g, l3_ln2_b)
    return h.reshape(batch, seq, d)
```

```python
import functools

import jax
import jax.numpy as jnp
from jax import lax
from jax.experimental import pallas as pl
from jax.experimental.pallas import tpu as pltpu

D_MODEL = 1024
SEQ = 2048
DEPTH = 4
POOL_WINDOWS = (2, 4, 8, 16)
POOL_GROUP_DIM = D_MODEL // len(POOL_WINDOWS)
ATTN_DILATIONS = (1, 4, 16)
ATTN_BLOCK = 128
HEAD_DIM = 64
ROT_DIM = HEAD_DIM // 4
ROPE_THETA = 500000.0
CONV_WIDTH = 3
N_EXPERTS = 8
MOE_TILE = 512
META_E1, META_E2, META_R1, META_R2, META_G1, META_G2 = range(6)
DN_ALPHA = (2 * DEPTH) ** 0.25
LN_EPS = 1e-5

LANES = 128
HALO = 16
NEG_BIG = -1e30
VMEM_LIMIT = 56 * 1024 * 1024

F32 = jnp.float32
BF16 = jnp.bfloat16


def _params(semantics, vmem=VMEM_LIMIT):
    return pltpu.CompilerParams(dimension_semantics=semantics, vmem_limit_bytes=vmem)


def _dot(a, b):
    return jnp.dot(a, b, preferred_element_type=F32)


def _layer_norm(z, g, b):
    mu = jnp.mean(z, axis=-1, keepdims=True)
    zc = z - mu
    var = jnp.mean(zc * zc, axis=-1, keepdims=True)
    return zc * lax.rsqrt(var + LN_EPS) * g + b


def _row(v):
    return v.reshape(1, -1)


def _pool_kernel(x_ref, w_in_ref, w_grp_ref, scale_ref, g_ref, b_ref, o_ref, halo_ref, *, tm, tiles_per_seq):
    i = pl.program_id(0)
    x = x_ref[...]
    u = _dot(x.astype(BF16), w_in_ref[...])

    @pl.when(i % tiles_per_seq == 0)
    def _():
        halo_ref[...] = jnp.zeros_like(halo_ref)

    buf = jnp.concatenate([halo_ref[...], u], axis=0)
    halo_ref[...] = u[tm - HALO:, :]
    t = (i % tiles_per_seq) * tm + lax.broadcasted_iota(jnp.int32, (tm, 1), 0)
    outs = []
    for grp, w in enumerate(POOL_WINDOWS):
        cols = slice(grp * POOL_GROUP_DIM, (grp + 1) * POOL_GROUP_DIM)
        s = buf[:, cols]
        k = 1
        while k < w:
            s = s + pltpu.roll(s, k, 0)
            k *= 2
        cnt = jnp.minimum(t + 1, w).astype(F32)
        pooled = s[HALO:, :] / cnt - u[:, cols]
        outs.append(_dot(pooled.astype(BF16), w_grp_ref[grp]))
    h = jnp.concatenate(outs, axis=1) * scale_ref[...]
    o_ref[...] = _layer_norm(DN_ALPHA * x + h, g_ref[...], b_ref[...])


def _pool_layer(x, w_in, w_grp, scale, g, b, *, tm=512):
    n = x.shape[0]
    kern = functools.partial(_pool_kernel, tm=tm, tiles_per_seq=SEQ // tm)
    vec = pl.BlockSpec((1, D_MODEL), lambda i: (0, 0))
    return pl.pallas_call(
        kern,
        out_shape=jax.ShapeDtypeStruct((n, D_MODEL), F32),
        grid=(n // tm,),
        in_specs=[pl.BlockSpec((tm, D_MODEL), lambda i: (i, 0)),
                  pl.BlockSpec((D_MODEL, D_MODEL), lambda i: (0, 0)),
                  pl.BlockSpec((len(POOL_WINDOWS), POOL_GROUP_DIM, POOL_GROUP_DIM), lambda i: (0, 0, 0)),
                  vec, vec, vec],
        out_specs=pl.BlockSpec((tm, D_MODEL), lambda i: (i, 0)),
        scratch_shapes=[pltpu.VMEM((HALO, D_MODEL), F32)],
        compiler_params=_params(("arbitrary",)),
        name="pool_mixer_ln",
    )(x, w_in.astype(BF16), w_grp.astype(BF16), _row(scale), _row(g), _row(b))


def _conv_kernel(x_ref, w_in_ref, cw_ref, w_out_ref, g_ref, b_ref, o_ref, halo_ref, *, tm, tiles_per_seq):
    i = pl.program_id(0)
    x = x_ref[...]
    proj = _dot(x.astype(BF16), w_in_ref[...])
    gate_b = proj[:, :D_MODEL]
    z = proj[:, D_MODEL:2 * D_MODEL] * proj[:, 2 * D_MODEL:]

    @pl.when(i % tiles_per_seq == 0)
    def _():
        halo_ref[...] = jnp.zeros_like(halo_ref)

    buf = jnp.concatenate([halo_ref[...], z], axis=0)
    halo_ref[...] = z[tm - HALO:, :]
    conv = cw_ref[0:1, :] * z
    for j in range(1, CONV_WIDTH):
        conv = conv + cw_ref[j:j + 1, :] * pltpu.roll(buf, j, 0)[HALO:, :]
    h = _dot((gate_b * conv).astype(BF16), w_out_ref[...])
    o_ref[...] = _layer_norm(DN_ALPHA * x + h, g_ref[...], b_ref[...])


def _conv_layer(x, w_in, conv_w, w_out, g, b, *, tm=512):
    n = x.shape[0]
    kern = functools.partial(_conv_kernel, tm=tm, tiles_per_seq=SEQ // tm)
    vec = pl.BlockSpec((1, D_MODEL), lambda i: (0, 0))
    return pl.pallas_call(
        kern,
        out_shape=jax.ShapeDtypeStruct((n, D_MODEL), F32),
        grid=(n // tm,),
        in_specs=[pl.BlockSpec((tm, D_MODEL), lambda i: (i, 0)),
                  pl.BlockSpec((D_MODEL, 3 * D_MODEL), lambda i: (0, 0)),
                  pl.BlockSpec((CONV_WIDTH, D_MODEL), lambda i: (0, 0)),
                  pl.BlockSpec((D_MODEL, D_MODEL), lambda i: (0, 0)),
                  vec, vec],
        out_specs=pl.BlockSpec((tm, D_MODEL), lambda i: (i, 0)),
        scratch_shapes=[pltpu.VMEM((HALO, D_MODEL), F32)],
        compiler_params=_params(("arbitrary",)),
        name="conv_mixer_ln",
    )(x, w_in.astype(BF16), conv_w, w_out.astype(BF16), _row(g), _row(b))


def _ffn_kernel(x_ref, wg_ref, wu_ref, wd_ref, g_ref, b_ref, o_ref, xb_ref, acc_ref):
    f = pl.program_id(1)

    @pl.when(f == 0)
    def _():
        xb_ref[...] = x_ref[...].astype(BF16)
        acc_ref[...] = jnp.zeros_like(acc_ref)

    xb = xb_ref[...]
    gate = _dot(xb, wg_ref[...])
    up = _dot(xb, wu_ref[...])
    h = gate * jax.nn.sigmoid(gate) * up
    acc_ref[...] += _dot(h.astype(BF16), wd_ref[...])

    @pl.when(f == pl.num_programs(1) - 1)
    def _():
        o_ref[...] = _layer_norm(DN_ALPHA * x_ref[...] + acc_ref[...], g_ref[...], b_ref[...])


def _ffn_layer(x, w_gate, w_up, w_down, g, b, *, tm=1024, tf=256):
    n = x.shape[0]
    d_ff = w_gate.shape[1]
    vec = pl.BlockSpec((1, D_MODEL), lambda i, f: (0, 0))
    return pl.pallas_call(
        _ffn_kernel,
        out_shape=jax.ShapeDtypeStruct((n, D_MODEL), F32),
        grid=(n // tm, d_ff // tf),
        in_specs=[pl.BlockSpec((tm, D_MODEL), lambda i, f: (i, 0)),
                  pl.BlockSpec((D_MODEL, tf), lambda i, f: (0, f)),
                  pl.BlockSpec((D_MODEL, tf), lambda i, f: (0, f)),
                  pl.BlockSpec((tf, D_MODEL), lambda i, f: (f, 0)),
                  vec, vec],
        out_specs=pl.BlockSpec((tm, D_MODEL), lambda i, f: (i, 0)),
        scratch_shapes=[pltpu.VMEM((tm, D_MODEL), BF16), pltpu.VMEM((tm, D_MODEL), F32)],
        compiler_params=_params(("arbitrary", "arbitrary")),
        name="swiglu_ln",
    )(x, w_gate.astype(BF16), w_up.astype(BF16), w_down.astype(BF16), _row(g), _row(b))


def _router_kernel(x_ref, w_ref, meta_ref, cnt_ref, run_ref):
    i = pl.program_id(0)

    @pl.when(i == 0)
    def _():
        run_ref[...] = jnp.zeros_like(run_ref)

    logits = jnp.dot(x_ref[...], w_ref[...], preferred_element_type=F32, precision=lax.Precision.HIGHEST)
    tm = logits.shape[0]
    lane = lax.broadcasted_iota(jnp.int32, logits.shape, 1)
    logits = jnp.where(lane < N_EXPERTS, logits, -jnp.inf)
    v1 = jnp.max(logits, axis=1, keepdims=True)
    i1 = jnp.min(jnp.where(logits == v1, lane, LANES), axis=1, keepdims=True)
    rest = jnp.where(lane == i1, -jnp.inf, logits)
    v2 = jnp.max(rest, axis=1, keepdims=True)
    i2 = jnp.min(jnp.where(rest == v2, lane, LANES), axis=1, keepdims=True)
    e2 = jnp.exp(v2 - v1)
    g1 = 1.0 / (1.0 + e2)
    g2 = e2 / (1.0 + e2)

    sel = jnp.where(lane == i1, 1.0, jnp.where(lane == i2, 1.0, 0.0))
    before = (lax.broadcasted_iota(jnp.int32, (tm, tm), 1) < lax.broadcasted_iota(jnp.int32, (tm, tm), 0))
    rank = run_ref[0:1, :] + _dot(before.astype(BF16), sel.astype(BF16))
    r1 = jnp.sum(jnp.where(lane == i1, rank, 0.0), axis=1, keepdims=True)
    r2 = jnp.sum(jnp.where(lane == i2, rank, 0.0), axis=1, keepdims=True)
    run_ref[...] = run_ref[...] + jnp.sum(sel, axis=0, keepdims=True)
    cnt_ref[...] = run_ref[...]
    meta = jnp.zeros_like(logits)
    for k, val in enumerate((i1.astype(F32), i2.astype(F32), r1, r2, g1, g2)):
        meta = jnp.where(lane == k, val, meta)
    meta_ref[...] = meta


def _router(x, w_router, *, tm=512):
    n = x.shape[0]
    w = jnp.pad(w_router, ((0, 0), (0, LANES - N_EXPERTS)))
    return pl.pallas_call(
        _router_kernel,
        out_shape=(jax.ShapeDtypeStruct((n, LANES), F32), jax.ShapeDtypeStruct((8, LANES), F32)),
        grid=(n // tm,),
        in_specs=[pl.BlockSpec((tm, D_MODEL), lambda i: (i, 0)),
                  pl.BlockSpec((D_MODEL, LANES), lambda i: (0, 0))],
        out_specs=(pl.BlockSpec((tm, LANES), lambda i: (i, 0)), pl.BlockSpec((8, LANES), lambda i: (0, 0))),
        scratch_shapes=[pltpu.VMEM((8, LANES), F32)],
        compiler_params=_params(("arbitrary",)),
        name="router_top2",
    )(x, w)


def _routing_tables(meta, counts, tm):
    n = meta.shape[0]
    idx = meta[:, :4].astype(jnp.int32)
    idx = idx.reshape(n // tm, tm, 4).transpose(0, 2, 1).reshape(n // tm, 4 * tm)
    cnt = counts[0, :N_EXPERTS].astype(jnp.int32)
    padded = (cnt + MOE_TILE - 1) // MOE_TILE * MOE_TILE
    ends = jnp.cumsum(padded)
    offs = ends - padded
    max_tiles = 2 * n // MOE_TILE + N_EXPERTS
    first_row = jnp.arange(max_tiles, dtype=jnp.int32) * MOE_TILE
    tile_expert = jnp.minimum(jnp.sum(first_row[:, None] >= ends[None, :], axis=1), N_EXPERTS - 1).astype(jnp.int32)
    return idx, offs, ends, tile_expert, ends[-1:] // MOE_TILE, max_tiles


def _dispatch_kernel(offs_ref, ends_ref, idx_hbm, x_ref, xs_hbm, idx_smem, zero_ref, sem_idx, sem_rows, *, tm):
    i = pl.program_id(0)
    idx_copy = pltpu.make_async_copy(idx_hbm.at[i], idx_smem, sem_idx)
    idx_copy.start()

    @pl.when(i == 0)
    def _():
        zero_ref[...] = jnp.zeros_like(zero_ref)

        def clear_tile(start):
            clear = pltpu.make_async_copy(zero_ref, xs_hbm.at[pl.ds(pl.multiple_of(start, MOE_TILE), MOE_TILE)], sem_rows)
            clear.start()
            clear.wait()

        for e in range(N_EXPERTS):
            @pl.when(ends_ref[e] > offs_ref[e])
            def _():
                clear_tile(ends_ref[e] - MOE_TILE)

        def clear_unused(j, carry):
            clear_tile(j * MOE_TILE)
            return carry
        lax.fori_loop(ends_ref[N_EXPERTS - 1] // MOE_TILE, xs_hbm.shape[0] // MOE_TILE, clear_unused, 0)

    idx_copy.wait()

    def send(t, carry):
        p1 = offs_ref[idx_smem[t]] + idx_smem[2 * tm + t]
        p2 = offs_ref[idx_smem[tm + t]] + idx_smem[3 * tm + t]
        pltpu.make_async_copy(x_ref.at[pl.ds(t, 1)], xs_hbm.at[pl.ds(p1, 1)], sem_rows).start()
        pltpu.make_async_copy(x_ref.at[pl.ds(t, 1)], xs_hbm.at[pl.ds(p2, 1)], sem_rows).start()
        return carry
    lax.fori_loop(0, tm, send, 0, unroll=8)
    for _ in range(2):
        pltpu.make_async_copy(x_ref, xs_hbm.at[pl.ds(0, tm)], sem_rows).wait()


def _dispatch(x, idx, offs, ends, max_tiles, *, tm):
    n = x.shape[0]
    kern = functools.partial(_dispatch_kernel, tm=tm)
    return pl.pallas_call(
        kern,
        out_shape=jax.ShapeDtypeStruct((max_tiles * MOE_TILE, D_MODEL), F32),
        grid_spec=pltpu.PrefetchScalarGridSpec(
            num_scalar_prefetch=2,
            grid=(n // tm,),
            in_specs=[pl.BlockSpec(memory_space=pl.ANY),
                      pl.BlockSpec((tm, D_MODEL), lambda i, offs, ends: (i, 0))],
            out_specs=pl.BlockSpec(memory_space=pl.ANY),
            scratch_shapes=[pltpu.SMEM((4 * tm,), jnp.int32), pltpu.VMEM((MOE_TILE, D_MODEL), F32),
                            pltpu.SemaphoreType.DMA, pltpu.SemaphoreType.DMA]),
        compiler_params=_params(("arbitrary",)),
        name="moe_dispatch",
    )(offs, ends, idx, x)


def _expert_kernel(te_ref, nu_ref, x_ref, wg_ref, wu_ref, wd_ref, o_ref, xb_ref):
    i = pl.program_id(0)
    f = pl.program_id(1)

    @pl.when(i < nu_ref[0])
    def _():
        @pl.when(f == 0)
        def _():
            xb_ref[...] = x_ref[...].astype(BF16)
            o_ref[...] = jnp.zeros_like(o_ref)

        xb = xb_ref[...]
        gate = _dot(xb, wg_ref[...])
        up = _dot(xb, wu_ref[...])
        h = gate * jax.nn.sigmoid(gate) * up
        o_ref[...] += _dot(h.astype(BF16), wd_ref[...])

    @pl.when((i >= nu_ref[0]) & (f == 0))
    def _():
        o_ref[...] = jnp.zeros_like(o_ref)


def _experts(xs, tile_expert, n_used, w_gate, w_up, w_down, max_tiles, *, tf=256):
    d_ff = w_gate.shape[2]
    n_f = d_ff // tf

    def row_map(i, f, te, nu):
        return jnp.minimum(i, nu[0] - 1), 0

    def out_map(i, f, te, nu):
        return i, 0

    def up_map(i, f, te, nu):
        return te[jnp.minimum(i, nu[0] - 1)], 0, jnp.where(i < nu[0], f, n_f - 1)

    def down_map(i, f, te, nu):
        return te[jnp.minimum(i, nu[0] - 1)], jnp.where(i < nu[0], f, n_f - 1), 0

    return pl.pallas_call(
        _expert_kernel,
        out_shape=jax.ShapeDtypeStruct((max_tiles * MOE_TILE, D_MODEL), F32),
        grid_spec=pltpu.PrefetchScalarGridSpec(
            num_scalar_prefetch=2,
            grid=(max_tiles, n_f),
            in_specs=[pl.BlockSpec((MOE_TILE, D_MODEL), row_map),
                      pl.BlockSpec((None, D_MODEL, tf), up_map),
                      pl.BlockSpec((None, D_MODEL, tf), up_map),
                      pl.BlockSpec((None, tf, D_MODEL), down_map)],
            out_specs=pl.BlockSpec((MOE_TILE, D_MODEL), out_map),
            scratch_shapes=[pltpu.VMEM((MOE_TILE, D_MODEL), BF16)]),
        compiler_params=_params(("arbitrary", "arbitrary")),
        name="moe_experts",
    )(tile_expert, n_used, xs, w_gate.astype(BF16), w_up.astype(BF16), w_down.astype(BF16))


def _combine_kernel(offs_ref, idx_hbm, x_ref, meta_ref, y_hbm, g_ref, b_ref, o_ref, idx_smem, buf_ref, sem_idx, sem_rows, *, tm):
    i = pl.program_id(0)
    idx_copy = pltpu.make_async_copy(idx_hbm.at[i], idx_smem, sem_idx)
    idx_copy.start()
    idx_copy.wait()

    def fetch(t, carry):
        p1 = offs_ref[idx_smem[t]] + idx_smem[2 * tm + t]
        p2 = offs_ref[idx_smem[tm + t]] + idx_smem[3 * tm + t]
        pltpu.make_async_copy(y_hbm.at[pl.ds(p1, 1)], buf_ref.at[0, pl.ds(t, 1)], sem_rows).start()
        pltpu.make_async_copy(y_hbm.at[pl.ds(p2, 1)], buf_ref.at[1, pl.ds(t, 1)], sem_rows).start()
        return carry
    lax.fori_loop(0, tm, fetch, 0, unroll=8)
    for k in range(2):
        pltpu.make_async_copy(y_hbm.at[pl.ds(0, tm)], buf_ref.at[k], sem_rows).wait()

    meta = meta_ref[...]
    lane = lax.broadcasted_iota(jnp.int32, meta.shape, 1)
    g1 = jnp.sum(jnp.where(lane == META_G1, meta, 0.0), axis=1, keepdims=True)
    g2 = jnp.sum(jnp.where(lane == META_G2, meta, 0.0), axis=1, keepdims=True)
    mix = g1 * buf_ref[0] + g2 * buf_ref[1]
    o_ref[...] = _layer_norm(DN_ALPHA * x_ref[...] + mix, g_ref[...], b_ref[...])


def _combine(x, meta, y, idx, offs, g, b, *, tm):
    n = x.shape[0]
    kern = functools.partial(_combine_kernel, tm=tm)
    vec = pl.BlockSpec((1, D_MODEL), lambda i, offs: (0, 0))
    return pl.pallas_call(
        kern,
        out_shape=jax.ShapeDtypeStruct((n, D_MODEL), F32),
        grid_spec=pltpu.PrefetchScalarGridSpec(
            num_scalar_prefetch=1,
            grid=(n // tm,),
            in_specs=[pl.BlockSpec(memory_space=pl.ANY),
                      pl.BlockSpec((tm, D_MODEL), lambda i, offs: (i, 0)),
                      pl.BlockSpec((tm, LANES), lambda i, offs: (i, 0)),
                      pl.BlockSpec(memory_space=pl.ANY),
                      vec, vec],
            out_specs=pl.BlockSpec((tm, D_MODEL), lambda i, offs: (i, 0)),
            scratch_shapes=[pltpu.SMEM((4 * tm,), jnp.int32), pltpu.VMEM((2, tm, D_MODEL), F32),
                            pltpu.SemaphoreType.DMA, pltpu.SemaphoreType.DMA]),
        compiler_params=_params(("arbitrary",)),
        name="moe_combine_ln",
    )(offs, idx, x, meta, y, _row(g), _row(b))


def _moe_layer(x, w_router, w_gate, w_up, w_down, g, b, *, tm=512):
    meta, counts = _router(x, w_router)
    idx, offs, ends, tile_expert, n_used, max_tiles = _routing_tables(meta, counts, tm)
    xs = _dispatch(x, idx, offs, ends, max_tiles, tm=tm)
    y = _experts(xs, tile_expert, n_used, w_gate, w_up, w_down, max_tiles)
    return _combine(x, meta, y, idx, offs, g, b, tm=tm)


def _rope_kernel(pos_ref, invf_ref, c_ref, s1_ref, s2_ref):
    ang = pos_ref[...].astype(F32) * invf_ref[...]
    c = jnp.cos(ang)
    s = jnp.sin(ang)
    dd = lax.broadcasted_iota(jnp.int32, ang.shape, 1) % HEAD_DIM
    c_ref[...] = c
    s1_ref[...] = jnp.where(dd < ROT_DIM // 2, -s, 0.0)
    s2_ref[...] = jnp.where((dd >= ROT_DIM // 2) & (dd < ROT_DIM), s, 0.0)


def _rope_tables(positions, *, tm=2048):
    n = positions.size
    half = ROT_DIM // 2
    inv_freq = ROPE_THETA ** (-(jnp.arange(0, ROT_DIM, 2, dtype=F32) / ROT_DIM))
    per_head = jnp.concatenate([inv_freq, inv_freq, jnp.zeros((HEAD_DIM - 2 * half,), F32)])
    invf = jnp.tile(per_head, LANES // HEAD_DIM).reshape(1, LANES)
    out = jax.ShapeDtypeStruct((n, LANES), F32)
    spec = pl.BlockSpec((tm, LANES), lambda i: (i, 0))
    return pl.pallas_call(
        _rope_kernel,
        out_shape=(out, out, out),
        grid=(n // tm,),
        in_specs=[pl.BlockSpec((tm, 1), lambda i: (i, 0)), pl.BlockSpec((1, LANES), lambda i: (0, 0))],
        out_specs=(spec, spec, spec),
        compiler_params=_params(("arbitrary",)),
        name="rope_tables",
    )(positions.reshape(n, 1), invf)


def _qkv_kernel(x_ref, c_ref, s1_ref, s2_ref, w_ref, o_ref, *, n_res, rows):
    for r in range(n_res):
        xb = x_ref[:, r * D_MODEL:(r + 1) * D_MODEL].astype(BF16)
        y = _dot(xb, w_ref[...])
        c = c_ref[:, r * LANES:(r + 1) * LANES]
        s1 = s1_ref[:, r * LANES:(r + 1) * LANES]
        s2 = s2_ref[:, r * LANES:(r + 1) * LANES]
        for part in range(2):
            scale = HEAD_DIM ** -0.5 if part == 0 else 1.0
            for blk in range(D_MODEL // LANES):
                lo = part * D_MODEL + blk * LANES
                t = y[:, lo:lo + LANES]
                rot = t * c + pltpu.roll(t, LANES - ROT_DIM // 2, 1) * s1 + pltpu.roll(t, ROT_DIM // 2, 1) * s2
                o_ref[r * rows:(r + 1) * rows, lo:lo + LANES] = (rot * scale).astype(BF16)
        o_ref[r * rows:(r + 1) * rows, 2 * D_MODEL:] = y[:, 2 * D_MODEL:].astype(BF16)


def _qkv_group(x, tabs, w, dil, *, tm=512):
    n = x.shape[0]
    seq_rows = SEQ // dil
    rows = min(tm, seq_rows)
    n_res = tm // rows
    seq_blocks = seq_rows // rows
    steps_per_seq = SEQ // tm
    assert n_res == 1 or seq_blocks == 1

    def view_map(i):
        s = i % steps_per_seq
        return (i // steps_per_seq) * seq_blocks + s % seq_blocks, s // seq_blocks

    xv = x.reshape(n // dil, dil * D_MODEL)
    tv = [t.reshape(n // dil, dil * LANES) for t in tabs]
    kern = functools.partial(_qkv_kernel, n_res=n_res, rows=rows)
    tab_spec = pl.BlockSpec((rows, n_res * LANES), view_map)
    return pl.pallas_call(
        kern,
        out_shape=jax.ShapeDtypeStruct((n, 3 * D_MODEL), BF16),
        grid=(n // tm,),
        in_specs=[pl.BlockSpec((rows, n_res * D_MODEL), view_map),
                  tab_spec, tab_spec, tab_spec,
                  pl.BlockSpec((D_MODEL, 3 * D_MODEL), lambda i: (0, 0))],
        out_specs=pl.BlockSpec((tm, 3 * D_MODEL), lambda i: (i, 0)),
        compiler_params=_params(("arbitrary",)),
        name="qkv_proj_dil%d" % dil,
    )(xv, *tv, w)


def _attn_kernel(*refs):
    qkv = refs[:9]
    o_ref = refs[9]
    acc_s, m_s, l_s = refs[10:13], refs[13:16], refs[16:19]

    lane = lax.broadcasted_iota(jnp.int32, (1, LANES), 1)
    head0 = lane < HEAD_DIM
    hm0 = head0.astype(BF16)
    hm1 = 1.0 - hm0
    qi = lax.broadcasted_iota(jnp.int32, (ATTN_BLOCK, ATTN_BLOCK), 0)
    kj = lax.broadcasted_iota(jnp.int32, (ATTN_BLOCK, ATTN_BLOCK), 1)
    cur_bias = jnp.where(kj <= qi, 0.0, NEG_BIG)
    prev_bias = jnp.where(kj >= qi, 0.0, NEG_BIG)
    bias_cur = jnp.concatenate([cur_bias, cur_bias], axis=0)
    bias_both = jnp.concatenate([jnp.concatenate([prev_bias, prev_bias], axis=0), bias_cur], axis=1)

    def one_block(grp, base, nat_start, dil, has_prev):
        q_ref, k_ref, v_ref = qkv[3 * grp:3 * grp + 3]
        q = q_ref[pl.ds(base, ATTN_BLOCK), :]
        q2 = jnp.concatenate([q * hm0, q * hm1], axis=0)
        if has_prev:
            kk = k_ref[pl.ds(base - ATTN_BLOCK, 2 * ATTN_BLOCK), :]
            vv = v_ref[pl.ds(base - ATTN_BLOCK, 2 * ATTN_BLOCK), :]
            bias = bias_both
        else:
            kk = k_ref[pl.ds(base, ATTN_BLOCK), :]
            vv = v_ref[pl.ds(base, ATTN_BLOCK), :]
            bias = bias_cur
        s = lax.dot_general(q2, kk, (((1,), (1,)), ((), ())), preferred_element_type=F32) + bias
        m = jnp.max(s, axis=1, keepdims=True)
        p = jnp.exp(s - m)
        l = jnp.sum(p, axis=1, keepdims=True)
        pb = p.astype(BF16)
        acc = _dot(pb[:ATTN_BLOCK], vv * hm0) + _dot(pb[ATTN_BLOCK:], vv * hm1)
        mb = jnp.where(head0, m[:ATTN_BLOCK], m[ATTN_BLOCK:])
        lb = jnp.where(head0, l[:ATTN_BLOCK], l[ATTN_BLOCK:])
        if dil == 1:
            rows = pl.ds(nat_start, ATTN_BLOCK)
        else:
            rows = pl.ds(nat_start, ATTN_BLOCK, stride=dil)
        acc_s[grp][rows, :] = acc
        m_s[grp][rows, :] = mb
        l_s[grp][rows, :] = lb

    for grp, dil in enumerate(ATTN_DILATIONS):
        seq_rows = SEQ // dil
        n_blocks = seq_rows // ATTN_BLOCK
        if n_blocks == 1:
            def single(r, carry, grp=grp, dil=dil):
                one_block(grp, pl.multiple_of(r * ATTN_BLOCK, ATTN_BLOCK), r, dil, False)
                return carry
            lax.fori_loop(0, dil, single, 0)
            continue
        for r in range(dil):
            one_block(grp, r * seq_rows, r, dil, False)

            def banded(nb, carry, grp=grp, dil=dil, r=r, seq_rows=seq_rows):
                base = pl.multiple_of(r * seq_rows + nb * ATTN_BLOCK, ATTN_BLOCK)
                nat = nb * (ATTN_BLOCK * dil) + r
                one_block(grp, base, pl.multiple_of(nat, ATTN_BLOCK) if dil == 1 else nat, dil, True)
                return carry
            lax.fori_loop(1, n_blocks, banded, 0)

    def merge(c, carry):
        rows = pl.ds(pl.multiple_of(c * ATTN_BLOCK, ATTN_BLOCK), ATTN_BLOCK)
        ms = [m_s[g][rows, :] for g in range(3)]
        top = jnp.maximum(jnp.maximum(ms[0], ms[1]), ms[2])
        num = jnp.zeros((ATTN_BLOCK, LANES), F32)
        den = jnp.zeros((ATTN_BLOCK, LANES), F32)
        for g in range(3):
            w = jnp.exp(ms[g] - top)
            num = num + w * acc_s[g][rows, :]
            den = den + w * l_s[g][rows, :]
        o_ref[rows, :] = (num / den).astype(BF16)
        return carry
    lax.fori_loop(0, SEQ // ATTN_BLOCK, merge, 0)


def _attention(qkvs, n):
    n_pairs = D_MODEL // LANES
    in_specs, args = [], []
    for qkv in qkvs:
        for part in range(3):
            in_specs.append(pl.BlockSpec((SEQ, LANES), lambda b, hp, part=part: (b, part * n_pairs + hp)))
            args.append(qkv)
    scratch = [pltpu.VMEM((SEQ, LANES), F32) for _ in range(9)]
    return pl.pallas_call(
        _attn_kernel,
        out_shape=jax.ShapeDtypeStruct((n, D_MODEL), BF16),
        grid=(n // SEQ, n_pairs),
        in_specs=in_specs,
        out_specs=pl.BlockSpec((SEQ, LANES), lambda b, hp: (b, hp)),
        scratch_shapes=scratch,
        compiler_params=_params(("arbitrary", "arbitrary")),
        name="dilated_attention",
    )(*args)


def _proj_ln_kernel(x_ref, a_ref, w_ref, g_ref, b_ref, o_ref):
    h = _dot(a_ref[...], w_ref[...])
    o_ref[...] = _layer_norm(DN_ALPHA * x_ref[...] + h, g_ref[...], b_ref[...])


def _proj_ln(x, a, w, g, b, *, tm=512):
    n = x.shape[0]
    vec = pl.BlockSpec((1, D_MODEL), lambda i: (0, 0))
    return pl.pallas_call(
        _proj_ln_kernel,
        out_shape=jax.ShapeDtypeStruct((n, D_MODEL), F32),
        grid=(n // tm,),
        in_specs=[pl.BlockSpec((tm, D_MODEL), lambda i: (i, 0)),
                  pl.BlockSpec((tm, D_MODEL), lambda i: (i, 0)),
                  pl.BlockSpec((D_MODEL, D_MODEL), lambda i: (0, 0)),
                  vec, vec],
        out_specs=pl.BlockSpec((tm, D_MODEL), lambda i: (i, 0)),
        compiler_params=_params(("arbitrary",)),
        name="out_proj_ln",
    )(x, a, w.astype(BF16), _row(g), _row(b))


def _attn_layer(x, tabs, w_qkv, w_o, g, b):
    n = x.shape[0]
    w = w_qkv.astype(BF16)
    qkvs = [_qkv_group(x, tabs, w[:, grp * 3 * D_MODEL:(grp + 1) * 3 * D_MODEL], dil)
            for grp, dil in enumerate(ATTN_DILATIONS)]
    return _proj_ln(x, _attention(qkvs, n), w_o, g, b)


def kernel(x, positions, l0_pool_w_in, l0_pool_w_grp, l0_pool_scale, l0_ln1_g, l0_ln1_b, l0_ffn_w_gate, l0_ffn_w_up, l0_ffn_w_down, l0_ln2_g, l0_ln2_b, l1_attn_w_qkv, l1_attn_w_o, l1_ln1_g, l1_ln1_b, l1_moe_w_router, l1_moe_w_gate, l1_moe_w_up, l1_moe_w_down, l1_ln2_g, l1_ln2_b, l2_conv_w_in, l2_conv_w, l2_conv_w_out, l2_ln1_g, l2_ln1_b, l2_ffn_w_gate, l2_ffn_w_up, l2_ffn_w_down, l2_ln2_g, l2_ln2_b, l3_pool_w_in, l3_pool_w_grp, l3_pool_scale, l3_ln1_g, l3_ln1_b, l3_moe_w_router, l3_moe_w_gate, l3_moe_w_up, l3_moe_w_down, l3_ln2_g, l3_ln2_b):
    batch, seq, d = x.shape
    h = x.reshape(batch * seq, d)
    tabs = _rope_tables(positions)
    h = _pool_layer(h, l0_pool_w_in, l0_pool_w_grp, l0_pool_scale, l0_ln1_g, l0_ln1_b)
    h = _ffn_layer(h, l0_ffn_w_gate, l0_ffn_w_up, l0_ffn_w_down, l0_ln2_g, l0_ln2_b)
    h = _attn_layer(h, tabs, l1_attn_w_qkv, l1_attn_w_o, l1_ln1_g, l1_ln1_b)
    h = _moe_layer(h, l1_moe_w_router, l1_moe_w_gate, l1_moe_w_up, l1_moe_w_down, l1_ln2_g, l1_ln2_b)
    h = _conv_layer(h, l2_conv_w_in, l2_conv_w, l2_conv_w_out, l2_ln1_g, l2_ln1_b)
    h = _ffn_layer(h, l2_ffn_w_gate, l2_ffn_w_up, l2_ffn_w_down, l2_ln2_g, l2_ln2_b)
    h = _pool_layer(h, l3_pool_w_in, l3_pool_w_grp, l3_pool_scale, l3_ln1_g, l3_ln1_b)
    h = _moe_layer(h, l3_moe_w_router, l3_moe_w_gate, l3_moe_w_up, l3_moe_w_down, l3_ln2_g, l3_ln2_b)
    return h.reshape(batch, seq, d)
```

```python
import functools

import jax
import jax.numpy as jnp
from jax import lax
from jax.experimental import pallas as pl
from jax.experimental.pallas import tpu as pltpu

D_MODEL = 1024
SEQ = 2048
DEPTH = 4
POOL_WINDOWS = (2, 4, 8, 16)
POOL_GROUP_DIM = D_MODEL // len(POOL_WINDOWS)
ATTN_DILATIONS = (1, 4, 16)
ATTN_BLOCK = 128
ATTN_UNROLL = 16
HEAD_DIM = 64
ROT_DIM = HEAD_DIM // 4
ROPE_THETA = 500000.0
LOG2_E = 1.4426950408889634
CONV_WIDTH = 3
N_EXPERTS = 8
MOE_TILE = 512
META_E1, META_E2, META_R1, META_R2, META_G1, META_G2 = range(6)
DN_ALPHA = (2 * DEPTH) ** 0.25
LN_EPS = 1e-5

LANES = 128
HALO = 16
NEG_BIG = -1e30
VMEM_LIMIT = 56 * 1024 * 1024

F32 = jnp.float32
BF16 = jnp.bfloat16


def _params(semantics, vmem=VMEM_LIMIT):
    return pltpu.CompilerParams(dimension_semantics=semantics, vmem_limit_bytes=vmem)


def _dot(a, b):
    return jnp.dot(a, b, preferred_element_type=F32)


def _layer_norm(z, g, b):
    mu = jnp.mean(z, axis=-1, keepdims=True)
    zc = z - mu
    var = jnp.mean(zc * zc, axis=-1, keepdims=True)
    return zc * lax.rsqrt(var + LN_EPS) * g + b


def _row(v):
    return v.reshape(1, -1)


def _pool_kernel(x_ref, w_in_ref, w_grp_ref, scale_ref, g_ref, b_ref, o_ref, halo_ref, *, tm, tiles_per_seq):
    i = pl.program_id(0)
    x = x_ref[...]
    u = _dot(x.astype(BF16), w_in_ref[...])

    @pl.when(i % tiles_per_seq == 0)
    def _():
        halo_ref[...] = jnp.zeros_like(halo_ref)

    buf = jnp.concatenate([halo_ref[...], u], axis=0)
    halo_ref[...] = u[tm - HALO:, :]
    t = (i % tiles_per_seq) * tm + lax.broadcasted_iota(jnp.int32, (tm, 1), 0)
    outs = []
    for grp, w in enumerate(POOL_WINDOWS):
        cols = slice(grp * POOL_GROUP_DIM, (grp + 1) * POOL_GROUP_DIM)
        s = buf[:, cols]
        k = 1
        while k < w:
            s = s + pltpu.roll(s, k, 0)
            k *= 2
        cnt = jnp.minimum(t + 1, w).astype(F32)
        pooled = s[HALO:, :] / cnt - u[:, cols]
        outs.append(_dot(pooled.astype(BF16), w_grp_ref[grp]))
    h = jnp.concatenate(outs, axis=1) * scale_ref[...]
    o_ref[...] = _layer_norm(DN_ALPHA * x + h, g_ref[...], b_ref[...])


def _pool_layer(x, w_in, w_grp, scale, g, b, *, tm=512):
    n = x.shape[0]
    kern = functools.partial(_pool_kernel, tm=tm, tiles_per_seq=SEQ // tm)
    vec = pl.BlockSpec((1, D_MODEL), lambda i: (0, 0))
    return pl.pallas_call(
        kern,
        out_shape=jax.ShapeDtypeStruct((n, D_MODEL), F32),
        grid=(n // tm,),
        in_specs=[pl.BlockSpec((tm, D_MODEL), lambda i: (i, 0)),
                  pl.BlockSpec((D_MODEL, D_MODEL), lambda i: (0, 0)),
                  pl.BlockSpec((len(POOL_WINDOWS), POOL_GROUP_DIM, POOL_GROUP_DIM), lambda i: (0, 0, 0)),
                  vec, vec, vec],
        out_specs=pl.BlockSpec((tm, D_MODEL), lambda i: (i, 0)),
        scratch_shapes=[pltpu.VMEM((HALO, D_MODEL), F32)],
        compiler_params=_params(("arbitrary",)),
        name="pool_mixer_ln",
    )(x, w_in.astype(BF16), w_grp.astype(BF16), _row(scale), _row(g), _row(b))


def _conv_kernel(x_ref, w_in_ref, cw_ref, w_out_ref, g_ref, b_ref, o_ref, halo_ref, *, tm, tiles_per_seq):
    i = pl.program_id(0)
    x = x_ref[...]
    proj = _dot(x.astype(BF16), w_in_ref[...])
    gate_b = proj[:, :D_MODEL]
    z = proj[:, D_MODEL:2 * D_MODEL] * proj[:, 2 * D_MODEL:]

    @pl.when(i % tiles_per_seq == 0)
    def _():
        halo_ref[...] = jnp.zeros_like(halo_ref)

    buf = jnp.concatenate([halo_ref[...], z], axis=0)
    halo_ref[...] = z[tm - HALO:, :]
    conv = cw_ref[0:1, :] * z
    for j in range(1, CONV_WIDTH):
        conv = conv + cw_ref[j:j + 1, :] * pltpu.roll(buf, j, 0)[HALO:, :]
    h = _dot((gate_b * conv).astype(BF16), w_out_ref[...])
    o_ref[...] = _layer_norm(DN_ALPHA * x + h, g_ref[...], b_ref[...])


def _conv_layer(x, w_in, conv_w, w_out, g, b, *, tm=512):
    n = x.shape[0]
    kern = functools.partial(_conv_kernel, tm=tm, tiles_per_seq=SEQ // tm)
    vec = pl.BlockSpec((1, D_MODEL), lambda i: (0, 0))
    return pl.pallas_call(
        kern,
        out_shape=jax.ShapeDtypeStruct((n, D_MODEL), F32),
        grid=(n // tm,),
        in_specs=[pl.BlockSpec((tm, D_MODEL), lambda i: (i, 0)),
                  pl.BlockSpec((D_MODEL, 3 * D_MODEL), lambda i: (0, 0)),
                  pl.BlockSpec((CONV_WIDTH, D_MODEL), lambda i: (0, 0)),
                  pl.BlockSpec((D_MODEL, D_MODEL), lambda i: (0, 0)),
                  vec, vec],
        out_specs=pl.BlockSpec((tm, D_MODEL), lambda i: (i, 0)),
        scratch_shapes=[pltpu.VMEM((HALO, D_MODEL), F32)],
        compiler_params=_params(("arbitrary",)),
        name="conv_mixer_ln",
    )(x, w_in.astype(BF16), conv_w, w_out.astype(BF16), _row(g), _row(b))


def _ffn_kernel(x_ref, wg_ref, wu_ref, wd_ref, g_ref, b_ref, o_ref, xb_ref, acc_ref):
    f = pl.program_id(1)

    @pl.when(f == 0)
    def _():
        xb_ref[...] = x_ref[...].astype(BF16)
        acc_ref[...] = jnp.zeros_like(acc_ref)

    xb = xb_ref[...]
    gate = _dot(xb, wg_ref[...])
    up = _dot(xb, wu_ref[...])
    h = gate * jax.nn.sigmoid(gate) * up
    acc_ref[...] += _dot(h.astype(BF16), wd_ref[...])

    @pl.when(f == pl.num_programs(1) - 1)
    def _():
        o_ref[...] = _layer_norm(DN_ALPHA * x_ref[...] + acc_ref[...], g_ref[...], b_ref[...])


def _ffn_layer(x, w_gate, w_up, w_down, g, b, *, tm=512, tf=1408):
    n = x.shape[0]
    d_ff = w_gate.shape[1]
    vec = pl.BlockSpec((1, D_MODEL), lambda i, f: (0, 0))
    return pl.pallas_call(
        _ffn_kernel,
        out_shape=jax.ShapeDtypeStruct((n, D_MODEL), F32),
        grid=(n // tm, d_ff // tf),
        in_specs=[pl.BlockSpec((tm, D_MODEL), lambda i, f: (i, 0)),
                  pl.BlockSpec((D_MODEL, tf), lambda i, f: (0, f)),
                  pl.BlockSpec((D_MODEL, tf), lambda i, f: (0, f)),
                  pl.BlockSpec((tf, D_MODEL), lambda i, f: (f, 0)),
                  vec, vec],
        out_specs=pl.BlockSpec((tm, D_MODEL), lambda i, f: (i, 0)),
        scratch_shapes=[pltpu.VMEM((tm, D_MODEL), BF16), pltpu.VMEM((tm, D_MODEL), F32)],
        compiler_params=_params(("arbitrary", "arbitrary")),
        name="swiglu_ln",
    )(x, w_gate.astype(BF16), w_up.astype(BF16), w_down.astype(BF16), _row(g), _row(b))


def _router_kernel(x_ref, w_ref, meta_ref, cnt_ref, run_ref):
    i = pl.program_id(0)

    @pl.when(i == 0)
    def _():
        run_ref[...] = jnp.zeros_like(run_ref)

    logits = jnp.dot(x_ref[...], w_ref[...], preferred_element_type=F32, precision=lax.Precision.HIGHEST)
    tm = logits.shape[0]
    lane = lax.broadcasted_iota(jnp.int32, logits.shape, 1)
    logits = jnp.where(lane < N_EXPERTS, logits, -jnp.inf)
    v1 = jnp.max(logits, axis=1, keepdims=True)
    i1 = jnp.min(jnp.where(logits == v1, lane, LANES), axis=1, keepdims=True)
    rest = jnp.where(lane == i1, -jnp.inf, logits)
    v2 = jnp.max(rest, axis=1, keepdims=True)
    i2 = jnp.min(jnp.where(rest == v2, lane, LANES), axis=1, keepdims=True)
    e2 = jnp.exp(v2 - v1)
    g1 = 1.0 / (1.0 + e2)
    g2 = e2 / (1.0 + e2)

    sel = jnp.where(lane == i1, 1.0, jnp.where(lane == i2, 1.0, 0.0))
    before = (lax.broadcasted_iota(jnp.int32, (tm, tm), 1) < lax.broadcasted_iota(jnp.int32, (tm, tm), 0))
    rank = run_ref[0:1, :] + _dot(before.astype(BF16), sel.astype(BF16))
    r1 = jnp.sum(jnp.where(lane == i1, rank, 0.0), axis=1, keepdims=True)
    r2 = jnp.sum(jnp.where(lane == i2, rank, 0.0), axis=1, keepdims=True)
    run_ref[...] = run_ref[...] + jnp.sum(sel, axis=0, keepdims=True)
    cnt_ref[...] = run_ref[...]
    meta = jnp.zeros_like(logits)
    for k, val in enumerate((i1.astype(F32), i2.astype(F32), r1, r2, g1, g2)):
        meta = jnp.where(lane == k, val, meta)
    meta_ref[...] = meta


def _router(x, w_router, *, tm=512):
    n = x.shape[0]
    w = jnp.pad(w_router, ((0, 0), (0, LANES - N_EXPERTS)))
    return pl.pallas_call(
        _router_kernel,
        out_shape=(jax.ShapeDtypeStruct((n, LANES), F32), jax.ShapeDtypeStruct((8, LANES), F32)),
        grid=(n // tm,),
        in_specs=[pl.BlockSpec((tm, D_MODEL), lambda i: (i, 0)),
                  pl.BlockSpec((D_MODEL, LANES), lambda i: (0, 0))],
        out_specs=(pl.BlockSpec((tm, LANES), lambda i: (i, 0)), pl.BlockSpec((8, LANES), lambda i: (0, 0))),
        scratch_shapes=[pltpu.VMEM((8, LANES), F32)],
        compiler_params=_params(("arbitrary",)),
        name="router_top2",
    )(x, w)


def _routing_tables(meta, counts, tm):
    n = meta.shape[0]
    idx = meta[:, :4].astype(jnp.int32)
    idx = idx.reshape(n // tm, tm, 4).transpose(0, 2, 1).reshape(n // tm, 4 * tm)
    cnt = counts[0, :N_EXPERTS].astype(jnp.int32)
    padded = (cnt + MOE_TILE - 1) // MOE_TILE * MOE_TILE
    ends = jnp.cumsum(padded)
    offs = ends - padded
    max_tiles = 2 * n // MOE_TILE + N_EXPERTS
    first_row = jnp.arange(max_tiles, dtype=jnp.int32) * MOE_TILE
    tile_expert = jnp.minimum(jnp.sum(first_row[:, None] >= ends[None, :], axis=1), N_EXPERTS - 1).astype(jnp.int32)
    return idx, offs, ends, tile_expert, ends[-1:] // MOE_TILE, max_tiles


def _dispatch_kernel(offs_ref, ends_ref, idx_hbm, x_ref, xs_hbm, idx_smem, zero_ref, sem_idx, sem_rows, *, tm):
    i = pl.program_id(0)
    idx_copy = pltpu.make_async_copy(idx_hbm.at[i], idx_smem, sem_idx)
    idx_copy.start()

    @pl.when(i == 0)
    def _():
        zero_ref[...] = jnp.zeros_like(zero_ref)

        def clear_tile(start):
            clear = pltpu.make_async_copy(zero_ref, xs_hbm.at[pl.ds(pl.multiple_of(start, MOE_TILE), MOE_TILE)], sem_rows)
            clear.start()
            clear.wait()

        for e in range(N_EXPERTS):
            @pl.when(ends_ref[e] > offs_ref[e])
            def _():
                clear_tile(ends_ref[e] - MOE_TILE)

        def clear_unused(j, carry):
            clear_tile(j * MOE_TILE)
            return carry
        lax.fori_loop(ends_ref[N_EXPERTS - 1] // MOE_TILE, xs_hbm.shape[0] // MOE_TILE, clear_unused, 0)

    idx_copy.wait()

    def send(t, carry):
        p1 = offs_ref[idx_smem[t]] + idx_smem[2 * tm + t]
        p2 = offs_ref[idx_smem[tm + t]] + idx_smem[3 * tm + t]
        pltpu.make_async_copy(x_ref.at[pl.ds(t, 1)], xs_hbm.at[pl.ds(p1, 1)], sem_rows).start()
        pltpu.make_async_copy(x_ref.at[pl.ds(t, 1)], xs_hbm.at[pl.ds(p2, 1)], sem_rows).start()
        return carry
    lax.fori_loop(0, tm, send, 0, unroll=8)
    for _ in range(2):
        pltpu.make_async_copy(x_ref, xs_hbm.at[pl.ds(0, tm)], sem_rows).wait()


def _dispatch(x, idx, offs, ends, max_tiles, *, tm):
    n = x.shape[0]
    kern = functools.partial(_dispatch_kernel, tm=tm)
    return pl.pallas_call(
        kern,
        out_shape=jax.ShapeDtypeStruct((max_tiles * MOE_TILE, D_MODEL), F32),
        grid_spec=pltpu.PrefetchScalarGridSpec(
            num_scalar_prefetch=2,
            grid=(n // tm,),
            in_specs=[pl.BlockSpec(memory_space=pl.ANY),
                      pl.BlockSpec((tm, D_MODEL), lambda i, offs, ends: (i, 0))],
            out_specs=pl.BlockSpec(memory_space=pl.ANY),
            scratch_shapes=[pltpu.SMEM((4 * tm,), jnp.int32), pltpu.VMEM((MOE_TILE, D_MODEL), F32),
                            pltpu.SemaphoreType.DMA, pltpu.SemaphoreType.DMA]),
        compiler_params=_params(("arbitrary",)),
        name="moe_dispatch",
    )(offs, ends, idx, x)


def _expert_kernel(te_ref, nu_ref, x_ref, wg_ref, wu_ref, wd_ref, o_ref, xb_ref):
    i = pl.program_id(0)
    f = pl.program_id(1)

    @pl.when(i < nu_ref[0])
    def _():
        @pl.when(f == 0)
        def _():
            xb_ref[...] = x_ref[...].astype(BF16)
            o_ref[...] = jnp.zeros_like(o_ref)

        xb = xb_ref[...]
        gate = _dot(xb, wg_ref[...])
        up = _dot(xb, wu_ref[...])
        h = gate * jax.nn.sigmoid(gate) * up
        o_ref[...] += _dot(h.astype(BF16), wd_ref[...])

    @pl.when((i >= nu_ref[0]) & (f == 0))
    def _():
        o_ref[...] = jnp.zeros_like(o_ref)


def _experts(xs, tile_expert, n_used, w_gate, w_up, w_down, max_tiles, *, tf=512):
    d_ff = w_gate.shape[2]
    n_f = d_ff // tf

    def used_tile(i, nu):
        return jnp.minimum(i, jnp.maximum(nu[0] - 1, 0))

    def row_map(i, f, te, nu):
        return used_tile(i, nu), 0

    def out_map(i, f, te, nu):
        return i, 0

    def up_map(i, f, te, nu):
        return te[used_tile(i, nu)], 0, jnp.where(i < nu[0], f, n_f - 1)

    def down_map(i, f, te, nu):
        return te[used_tile(i, nu)], jnp.where(i < nu[0], f, n_f - 1), 0

    return pl.pallas_call(
        _expert_kernel,
        out_shape=jax.ShapeDtypeStruct((max_tiles * MOE_TILE, D_MODEL), F32),
        grid_spec=pltpu.PrefetchScalarGridSpec(
            num_scalar_prefetch=2,
            grid=(max_tiles, n_f),
            in_specs=[pl.BlockSpec((MOE_TILE, D_MODEL), row_map),
                      pl.BlockSpec((None, D_MODEL, tf), up_map),
                      pl.BlockSpec((None, D_MODEL, tf), up_map),
                      pl.BlockSpec((None, tf, D_MODEL), down_map)],
            out_specs=pl.BlockSpec((MOE_TILE, D_MODEL), out_map),
            scratch_shapes=[pltpu.VMEM((MOE_TILE, D_MODEL), BF16)]),
        compiler_params=_params(("arbitrary", "arbitrary")),
        name="moe_experts",
    )(tile_expert, n_used, xs, w_gate.astype(BF16), w_up.astype(BF16), w_down.astype(BF16))


def _combine_kernel(offs_ref, idx_hbm, x_ref, meta_ref, y_hbm, g_ref, b_ref, o_ref, idx_smem, buf_ref, sem_idx, sem_rows, *, tm):
    i = pl.program_id(0)
    idx_copy = pltpu.make_async_copy(idx_hbm.at[i], idx_smem, sem_idx)
    idx_copy.start()
    idx_copy.wait()

    def fetch(t, carry):
        p1 = offs_ref[idx_smem[t]] + idx_smem[2 * tm + t]
        p2 = offs_ref[idx_smem[tm + t]] + idx_smem[3 * tm + t]
        pltpu.make_async_copy(y_hbm.at[pl.ds(p1, 1)], buf_ref.at[0, pl.ds(t, 1)], sem_rows).start()
        pltpu.make_async_copy(y_hbm.at[pl.ds(p2, 1)], buf_ref.at[1, pl.ds(t, 1)], sem_rows).start()
        return carry
    lax.fori_loop(0, tm, fetch, 0, unroll=8)
    for k in range(2):
        pltpu.make_async_copy(y_hbm.at[pl.ds(0, tm)], buf_ref.at[k], sem_rows).wait()

    meta = meta_ref[...]
    lane = lax.broadcasted_iota(jnp.int32, meta.shape, 1)
    g1 = jnp.sum(jnp.where(lane == META_G1, meta, 0.0), axis=1, keepdims=True)
    g2 = jnp.sum(jnp.where(lane == META_G2, meta, 0.0), axis=1, keepdims=True)
    mix = g1 * buf_ref[0] + g2 * buf_ref[1]
    o_ref[...] = _layer_norm(DN_ALPHA * x_ref[...] + mix, g_ref[...], b_ref[...])


def _combine(x, meta, y, idx, offs, g, b, *, tm):
    n = x.shape[0]
    kern = functools.partial(_combine_kernel, tm=tm)
    vec = pl.BlockSpec((1, D_MODEL), lambda i, offs: (0, 0))
    return pl.pallas_call(
        kern,
        out_shape=jax.ShapeDtypeStruct((n, D_MODEL), F32),
        grid_spec=pltpu.PrefetchScalarGridSpec(
            num_scalar_prefetch=1,
            grid=(n // tm,),
            in_specs=[pl.BlockSpec(memory_space=pl.ANY),
                      pl.BlockSpec((tm, D_MODEL), lambda i, offs: (i, 0)),
                      pl.BlockSpec((tm, LANES), lambda i, offs: (i, 0)),
                      pl.BlockSpec(memory_space=pl.ANY),
                      vec, vec],
            out_specs=pl.BlockSpec((tm, D_MODEL), lambda i, offs: (i, 0)),
            scratch_shapes=[pltpu.SMEM((4 * tm,), jnp.int32), pltpu.VMEM((2, tm, D_MODEL), F32),
                            pltpu.SemaphoreType.DMA, pltpu.SemaphoreType.DMA]),
        compiler_params=_params(("arbitrary",)),
        name="moe_combine_ln",
    )(offs, idx, x, meta, y, _row(g), _row(b))


def _moe_layer(x, w_router, w_gate, w_up, w_down, g, b, *, tm=512):
    meta, counts = _router(x, w_router)
    idx, offs, ends, tile_expert, n_used, max_tiles = _routing_tables(meta, counts, tm)
    xs = _dispatch(x, idx, offs, ends, max_tiles, tm=tm)
    y = _experts(xs, tile_expert, n_used, w_gate, w_up, w_down, max_tiles)
    return _combine(x, meta, y, idx, offs, g, b, tm=tm)


def _rope_kernel(pos_ref, invf_ref, c_ref, s1_ref, s2_ref):
    ang = pos_ref[...].astype(F32) * invf_ref[...]
    c = jnp.cos(ang)
    s = jnp.sin(ang)
    dd = lax.broadcasted_iota(jnp.int32, ang.shape, 1) % HEAD_DIM
    c_ref[...] = c
    s1_ref[...] = jnp.where(dd < ROT_DIM // 2, -s, 0.0)
    s2_ref[...] = jnp.where((dd >= ROT_DIM // 2) & (dd < ROT_DIM), s, 0.0)


def _rope_tables(positions, *, tm=2048):
    n = positions.size
    half = ROT_DIM // 2
    inv_freq = ROPE_THETA ** (-(jnp.arange(0, ROT_DIM, 2, dtype=F32) / ROT_DIM))
    per_head = jnp.concatenate([inv_freq, inv_freq, jnp.zeros((HEAD_DIM - 2 * half,), F32)])
    invf = jnp.tile(per_head, LANES // HEAD_DIM).reshape(1, LANES)
    out = jax.ShapeDtypeStruct((n, LANES), F32)
    spec = pl.BlockSpec((tm, LANES), lambda i: (i, 0))
    return pl.pallas_call(
        _rope_kernel,
        out_shape=(out, out, out),
        grid=(n // tm,),
        in_specs=[pl.BlockSpec((tm, 1), lambda i: (i, 0)), pl.BlockSpec((1, LANES), lambda i: (0, 0))],
        out_specs=(spec, spec, spec),
        compiler_params=_params(("arbitrary",)),
        name="rope_tables",
    )(positions.reshape(n, 1), invf)


def _qkv_kernel(x_ref, c_ref, s1_ref, s2_ref, w_ref, o_ref, *, n_res, rows):
    for r in range(n_res):
        xb = x_ref[:, r * D_MODEL:(r + 1) * D_MODEL].astype(BF16)
        y = _dot(xb, w_ref[...])
        c = c_ref[:, r * LANES:(r + 1) * LANES]
        s1 = s1_ref[:, r * LANES:(r + 1) * LANES]
        s2 = s2_ref[:, r * LANES:(r + 1) * LANES]
        for part in range(2):
            scale = HEAD_DIM ** -0.5 * LOG2_E if part == 0 else 1.0
            for blk in range(D_MODEL // LANES):
                lo = part * D_MODEL + blk * LANES
                t = y[:, lo:lo + LANES]
                rot = t * c + pltpu.roll(t, LANES - ROT_DIM // 2, 1) * s1 + pltpu.roll(t, ROT_DIM // 2, 1) * s2
                o_ref[r * rows:(r + 1) * rows, lo:lo + LANES] = (rot * scale).astype(BF16)
        o_ref[r * rows:(r + 1) * rows, 2 * D_MODEL:] = y[:, 2 * D_MODEL:].astype(BF16)


def _qkv_group(x, tabs, w, dil, *, tm=512):
    n = x.shape[0]
    seq_rows = SEQ // dil
    rows = min(tm, seq_rows)
    n_res = tm // rows
    seq_blocks = seq_rows // rows
    steps_per_seq = SEQ // tm
    assert n_res == 1 or seq_blocks == 1

    def view_map(i):
        s = i % steps_per_seq
        return (i // steps_per_seq) * seq_blocks + s % seq_blocks, s // seq_blocks

    xv = x.reshape(n // dil, dil * D_MODEL)
    tv = [t.reshape(n // dil, dil * LANES) for t in tabs]
    kern = functools.partial(_qkv_kernel, n_res=n_res, rows=rows)
    tab_spec = pl.BlockSpec((rows, n_res * LANES), view_map)
    return pl.pallas_call(
        kern,
        out_shape=jax.ShapeDtypeStruct((n, 3 * D_MODEL), BF16),
        grid=(n // tm,),
        in_specs=[pl.BlockSpec((rows, n_res * D_MODEL), view_map),
                  tab_spec, tab_spec, tab_spec,
                  pl.BlockSpec((D_MODEL, 3 * D_MODEL), lambda i: (0, 0))],
        out_specs=pl.BlockSpec((tm, 3 * D_MODEL), lambda i: (i, 0)),
        compiler_params=_params(("arbitrary",)),
        name="qkv_proj_dil%d" % dil,
    )(xv, *tv, w)


def _largest_divisor(n, cap):
    return max(d for d in range(1, cap + 1) if n % d == 0)


def _attn_kernel(*refs):
    qkv = refs[:9]
    o_ref = refs[9]
    acc_s, m_s, l_s = refs[10:13], refs[13:16], refs[16:19]

    lane = lax.broadcasted_iota(jnp.int32, (1, LANES), 1)
    head0 = lane < HEAD_DIM
    hm0 = head0.astype(BF16)
    hm1 = 1.0 - hm0
    qi = lax.broadcasted_iota(jnp.int32, (ATTN_BLOCK, ATTN_BLOCK), 0)
    kj = lax.broadcasted_iota(jnp.int32, (ATTN_BLOCK, ATTN_BLOCK), 1)
    cur_mask = jnp.where(qi <= kj, 0.0, NEG_BIG).astype(BF16)
    prev_mask = jnp.where(qi >= kj, 0.0, NEG_BIG).astype(BF16)
    mask_both = jnp.concatenate([prev_mask, cur_mask], axis=0)
    row_onehot = (qi == kj).astype(BF16)
    row_onehot = jnp.concatenate([row_onehot, row_onehot], axis=0)

    def scores(grp, base, has_prev):
        q_ref, k_ref, v_ref = qkv[3 * grp:3 * grp + 3]
        q = q_ref[pl.ds(base, ATTN_BLOCK), :]
        q2 = jnp.concatenate([q * hm0, q * hm1], axis=0)
        q2 = jnp.concatenate([q2, row_onehot], axis=1)
        if has_prev:
            kk = k_ref[pl.ds(base - ATTN_BLOCK, 2 * ATTN_BLOCK), :]
            vv = v_ref[pl.ds(base - ATTN_BLOCK, 2 * ATTN_BLOCK), :]
            kk = jnp.concatenate([kk, mask_both], axis=1)
        else:
            kk = k_ref[pl.ds(base, ATTN_BLOCK), :]
            vv = v_ref[pl.ds(base, ATTN_BLOCK), :]
            kk = jnp.concatenate([kk, cur_mask], axis=1)
        s = lax.dot_general(q2, kk, (((1,), (1,)), ((), ())), preferred_element_type=F32)
        return s, vv

    def finish(grp, s, vv, nat_start, dil):
        m = jnp.max(s, axis=1, keepdims=True)
        pb = jnp.exp2(s - m).astype(BF16)
        ones = jnp.ones_like(vv)
        o0 = _dot(pb[:ATTN_BLOCK], jnp.concatenate([vv * hm0, ones], axis=1))
        o1 = _dot(pb[ATTN_BLOCK:], jnp.concatenate([vv * hm1, ones], axis=1))
        acc = o0[:, :LANES] + o1[:, :LANES]
        mb = jnp.where(head0, m[:ATTN_BLOCK], m[ATTN_BLOCK:])
        lb = jnp.where(head0, o0[:, LANES:], o1[:, LANES:])
        if dil == 1:
            rows = pl.ds(nat_start, ATTN_BLOCK)
        else:
            rows = pl.ds(nat_start, ATTN_BLOCK, stride=dil)
        acc_s[grp][rows, :] = acc
        m_s[grp][rows, :] = mb
        l_s[grp][rows, :] = lb

    def run_blocks(specs):
        staged = [scores(grp, base, has_prev) for grp, base, _, _, has_prev in specs]
        for (grp, _, nat_start, dil, _), (s, vv) in zip(specs, staged):
            finish(grp, s, vv, nat_start, dil)

    for grp, dil in enumerate(ATTN_DILATIONS):
        seq_rows = SEQ // dil
        n_blocks = seq_rows // ATTN_BLOCK
        if n_blocks == 1:
            unroll = _largest_divisor(dil, ATTN_UNROLL)

            def singles(k, carry, grp=grp, dil=dil, unroll=unroll):
                rs = [k * unroll + j for j in range(unroll)]
                run_blocks([(grp, pl.multiple_of(r * ATTN_BLOCK, ATTN_BLOCK), r, dil, False) for r in rs])
                return carry
            lax.fori_loop(0, dil // unroll, singles, 0)
            continue
        run_blocks([(grp, r * seq_rows, r, dil, False) for r in range(dil)])

        n_banded = dil * (n_blocks - 1)
        unroll = _largest_divisor(n_banded, ATTN_UNROLL)

        def banded(k, carry, grp=grp, dil=dil, seq_rows=seq_rows, n_blocks=n_blocks, unroll=unroll):
            specs = []
            for j in range(unroll):
                flat = k * unroll + j
                r = flat // (n_blocks - 1)
                nb = 1 + flat % (n_blocks - 1)
                base = pl.multiple_of(r * seq_rows + nb * ATTN_BLOCK, ATTN_BLOCK)
                nat = nb * (ATTN_BLOCK * dil) + r
                specs.append((grp, base, pl.multiple_of(nat, ATTN_BLOCK) if dil == 1 else nat, dil, True))
            run_blocks(specs)
            return carry
        lax.fori_loop(0, n_banded // unroll, banded, 0)

    def merge(c, carry):
        rows = pl.ds(pl.multiple_of(c * ATTN_BLOCK, ATTN_BLOCK), ATTN_BLOCK)
        ms = [m_s[g][rows, :] for g in range(3)]
        top = jnp.maximum(jnp.maximum(ms[0], ms[1]), ms[2])
        num = jnp.zeros((ATTN_BLOCK, LANES), F32)
        den = jnp.zeros((ATTN_BLOCK, LANES), F32)
        for g in range(3):
            w = jnp.exp2(ms[g] - top)
            num = num + w * acc_s[g][rows, :]
            den = den + w * l_s[g][rows, :]
        o_ref[rows, :] = (num / den).astype(BF16)
        return carry
    lax.fori_loop(0, SEQ // ATTN_BLOCK, merge, 0)


def _attention(qkvs, n):
    n_pairs = D_MODEL // LANES
    in_specs, args = [], []
    for qkv in qkvs:
        for part in range(3):
            in_specs.append(pl.BlockSpec((SEQ, LANES), lambda b, hp, part=part: (b, part * n_pairs + hp)))
            args.append(qkv)
    scratch = [pltpu.VMEM((SEQ, LANES), F32) for _ in range(9)]
    return pl.pallas_call(
        _attn_kernel,
        out_shape=jax.ShapeDtypeStruct((n, D_MODEL), BF16),
        grid=(n // SEQ, n_pairs),
        in_specs=in_specs,
        out_specs=pl.BlockSpec((SEQ, LANES), lambda b, hp: (b, hp)),
        scratch_shapes=scratch,
        compiler_params=_params(("arbitrary", "arbitrary")),
        name="dilated_attention",
    )(*args)


def _proj_ln_kernel(x_ref, a_ref, w_ref, g_ref, b_ref, o_ref):
    h = _dot(a_ref[...], w_ref[...])
    o_ref[...] = _layer_norm(DN_ALPHA * x_ref[...] + h, g_ref[...], b_ref[...])


def _proj_ln(x, a, w, g, b, *, tm=512):
    n = x.shape[0]
    vec = pl.BlockSpec((1, D_MODEL), lambda i: (0, 0))
    return pl.pallas_call(
        _proj_ln_kernel,
        out_shape=jax.ShapeDtypeStruct((n, D_MODEL), F32),
        grid=(n // tm,),
        in_specs=[pl.BlockSpec((tm, D_MODEL), lambda i: (i, 0)),
                  pl.BlockSpec((tm, D_MODEL), lambda i: (i, 0)),
                  pl.BlockSpec((D_MODEL, D_MODEL), lambda i: (0, 0)),
                  vec, vec],
        out_specs=pl.BlockSpec((tm, D_MODEL), lambda i: (i, 0)),
        compiler_params=_params(("arbitrary",)),
        name="out_proj_ln",
    )(x, a, w.astype(BF16), _row(g), _row(b))


def _attn_layer(x, tabs, w_qkv, w_o, g, b):
    n = x.shape[0]
    w = w_qkv.astype(BF16)
    qkvs = [_qkv_group(x, tabs, w[:, grp * 3 * D_MODEL:(grp + 1) * 3 * D_MODEL], dil)
            for grp, dil in enumerate(ATTN_DILATIONS)]
    return _proj_ln(x, _attention(qkvs, n), w_o, g, b)


def kernel(x, positions, l0_pool_w_in, l0_pool_w_grp, l0_pool_scale, l0_ln1_g, l0_ln1_b, l0_ffn_w_gate, l0_ffn_w_up, l0_ffn_w_down, l0_ln2_g, l0_ln2_b, l1_attn_w_qkv, l1_attn_w_o, l1_ln1_g, l1_ln1_b, l1_moe_w_router, l1_moe_w_gate, l1_moe_w_up, l1_moe_w_down, l1_ln2_g, l1_ln2_b, l2_conv_w_in, l2_conv_w, l2_conv_w_out, l2_ln1_g, l2_ln1_b, l2_ffn_w_gate, l2_ffn_w_up, l2_ffn_w_down, l2_ln2_g, l2_ln2_b, l3_pool_w_in, l3_pool_w_grp, l3_pool_scale, l3_ln1_g, l3_ln1_b, l3_moe_w_router, l3_moe_w_gate, l3_moe_w_up, l3_moe_w_down, l3_ln2_g, l3_ln2_b):
    batch, seq, d = x.shape
    h = x.reshape(batch * seq, d)
    tabs = _rope_tables(positions)
    h = _pool_layer(h, l0_pool_w_in, l0_pool_w_grp, l0_pool_scale, l0_ln1_g, l0_ln1_b)
    h = _ffn_layer(h, l0_ffn_w_gate, l0_ffn_w_up, l0_ffn_w_down, l0_ln2_g, l0_ln2_b)
    h = _attn_layer(h, tabs, l1_attn_w_qkv, l1_attn_w_o, l1_ln1_g, l1_ln1_b)
    h = _moe_layer(h, l1_moe_w_router, l1_moe_w_gate, l1_moe_w_up, l1_moe_w_down, l1_ln2_g, l1_ln2_b)
    h = _conv_layer(h, l2_conv_w_in, l2_conv_w, l2_conv_w_out, l2_ln1_g, l2_ln1_b)
    h = _ffn_layer(h, l2_ffn_w_gate, l2_ffn_w_up, l2_ffn_w_down, l2_ln2_g, l2_ln2_b)
    h = _pool_layer(h, l3_pool_w_in, l3_pool_w_grp, l3_pool_scale, l3_ln1_g, l3_ln1_b)
    h = _moe_layer(h, l3_moe_w_router, l3_moe_w_gate, l3_moe_w_up, l3_moe_w_down, l3_ln2_g, l3_ln2_b)
    return h.reshape(batch, seq, d)
```

```python
import functools

import jax
import jax.numpy as jnp
from jax import lax
from jax.experimental import pallas as pl
from jax.experimental.pallas import tpu as pltpu

D_MODEL = 1024
SEQ = 2048
DEPTH = 4
POOL_WINDOWS = (2, 4, 8, 16)
POOL_GROUP_DIM = D_MODEL // len(POOL_WINDOWS)
ATTN_DILATIONS = (1, 4, 16)
ATTN_BLOCK = 128
ATTN_UNROLL = 16
HEAD_DIM = 64
ROT_DIM = HEAD_DIM // 4
ROPE_THETA = 500000.0
LOG2_E = 1.4426950408889634
CONV_WIDTH = 3
N_EXPERTS = 8
MOE_TILE = 1024
META_E1, META_E2, META_R1, META_R2, META_G1, META_G2 = range(6)
DN_ALPHA = (2 * DEPTH) ** 0.25
LN_EPS = 1e-5

LANES = 128
HALO = 16
NEG_BIG = -1e30
VMEM_LIMIT = 56 * 1024 * 1024

F32 = jnp.float32
BF16 = jnp.bfloat16


def _params(semantics, vmem=VMEM_LIMIT):
    return pltpu.CompilerParams(dimension_semantics=semantics, vmem_limit_bytes=vmem)


def _dot(a, b):
    return jnp.dot(a, b, preferred_element_type=F32)


def _layer_norm(z, g, b):
    mu = jnp.mean(z, axis=-1, keepdims=True)
    zc = z - mu
    var = jnp.mean(zc * zc, axis=-1, keepdims=True)
    return zc * lax.rsqrt(var + LN_EPS) * g + b


def _row(v):
    return v.reshape(1, -1)


def _pool_kernel(x_ref, w_in_ref, w_grp_ref, scale_ref, g_ref, b_ref, o_ref, halo_ref, *, tm, tiles_per_seq):
    i = pl.program_id(0)
    x = x_ref[...]
    u = _dot(x.astype(BF16), w_in_ref[...])

    @pl.when(i % tiles_per_seq == 0)
    def _():
        halo_ref[...] = jnp.zeros_like(halo_ref)

    buf = jnp.concatenate([halo_ref[...], u], axis=0)
    halo_ref[...] = u[tm - HALO:, :]
    t = (i % tiles_per_seq) * tm + lax.broadcasted_iota(jnp.int32, (tm, 1), 0)
    outs = []
    for grp, w in enumerate(POOL_WINDOWS):
        cols = slice(grp * POOL_GROUP_DIM, (grp + 1) * POOL_GROUP_DIM)
        s = buf[:, cols]
        k = 1
        while k < w:
            s = s + pltpu.roll(s, k, 0)
            k *= 2
        cnt = jnp.minimum(t + 1, w).astype(F32)
        pooled = s[HALO:, :] / cnt - u[:, cols]
        outs.append(_dot(pooled.astype(BF16), w_grp_ref[grp]))
    h = jnp.concatenate(outs, axis=1) * scale_ref[...]
    o_ref[...] = _layer_norm(DN_ALPHA * x + h, g_ref[...], b_ref[...])


def _pool_layer(x, w_in, w_grp, scale, g, b, *, tm=512):
    n = x.shape[0]
    kern = functools.partial(_pool_kernel, tm=tm, tiles_per_seq=SEQ // tm)
    vec = pl.BlockSpec((1, D_MODEL), lambda i: (0, 0))
    return pl.pallas_call(
        kern,
        out_shape=jax.ShapeDtypeStruct((n, D_MODEL), F32),
        grid=(n // tm,),
        in_specs=[pl.BlockSpec((tm, D_MODEL), lambda i: (i, 0)),
                  pl.BlockSpec((D_MODEL, D_MODEL), lambda i: (0, 0)),
                  pl.BlockSpec((len(POOL_WINDOWS), POOL_GROUP_DIM, POOL_GROUP_DIM), lambda i: (0, 0, 0)),
                  vec, vec, vec],
        out_specs=pl.BlockSpec((tm, D_MODEL), lambda i: (i, 0)),
        scratch_shapes=[pltpu.VMEM((HALO, D_MODEL), F32)],
        compiler_params=_params(("arbitrary",)),
        name="pool_mixer_ln",
    )(x, w_in.astype(BF16), w_grp.astype(BF16), _row(scale), _row(g), _row(b))


def _conv_kernel(x_ref, w_in_ref, cw_ref, w_out_ref, g_ref, b_ref, o_ref, halo_ref, *, tm, tiles_per_seq):
    i = pl.program_id(0)
    x = x_ref[...]
    proj = _dot(x.astype(BF16), w_in_ref[...])
    gate_b = proj[:, :D_MODEL]
    z = proj[:, D_MODEL:2 * D_MODEL] * proj[:, 2 * D_MODEL:]

    @pl.when(i % tiles_per_seq == 0)
    def _():
        halo_ref[...] = jnp.zeros_like(halo_ref)

    buf = jnp.concatenate([halo_ref[...], z], axis=0)
    halo_ref[...] = z[tm - HALO:, :]
    conv = cw_ref[0:1, :] * z
    for j in range(1, CONV_WIDTH):
        conv = conv + cw_ref[j:j + 1, :] * pltpu.roll(buf, j, 0)[HALO:, :]
    h = _dot((gate_b * conv).astype(BF16), w_out_ref[...])
    o_ref[...] = _layer_norm(DN_ALPHA * x + h, g_ref[...], b_ref[...])


def _conv_layer(x, w_in, conv_w, w_out, g, b, *, tm=512):
    n = x.shape[0]
    kern = functools.partial(_conv_kernel, tm=tm, tiles_per_seq=SEQ // tm)
    vec = pl.BlockSpec((1, D_MODEL), lambda i: (0, 0))
    return pl.pallas_call(
        kern,
        out_shape=jax.ShapeDtypeStruct((n, D_MODEL), F32),
        grid=(n // tm,),
        in_specs=[pl.BlockSpec((tm, D_MODEL), lambda i: (i, 0)),
                  pl.BlockSpec((D_MODEL, 3 * D_MODEL), lambda i: (0, 0)),
                  pl.BlockSpec((CONV_WIDTH, D_MODEL), lambda i: (0, 0)),
                  pl.BlockSpec((D_MODEL, D_MODEL), lambda i: (0, 0)),
                  vec, vec],
        out_specs=pl.BlockSpec((tm, D_MODEL), lambda i: (i, 0)),
        scratch_shapes=[pltpu.VMEM((HALO, D_MODEL), F32)],
        compiler_params=_params(("arbitrary",)),
        name="conv_mixer_ln",
    )(x, w_in.astype(BF16), conv_w, w_out.astype(BF16), _row(g), _row(b))


def _ffn_kernel(x_ref, wg_ref, wu_ref, wd_ref, g_ref, b_ref, o_ref, xb_ref, acc_ref):
    f = pl.program_id(1)

    @pl.when(f == 0)
    def _():
        xb_ref[...] = x_ref[...].astype(BF16)
        acc_ref[...] = jnp.zeros_like(acc_ref)

    xb = xb_ref[...]
    gate = _dot(xb, wg_ref[...])
    up = _dot(xb, wu_ref[...])
    h = gate * jax.nn.sigmoid(gate) * up
    acc_ref[...] += _dot(h.astype(BF16), wd_ref[...])

    @pl.when(f == pl.num_programs(1) - 1)
    def _():
        o_ref[...] = _layer_norm(DN_ALPHA * x_ref[...] + acc_ref[...], g_ref[...], b_ref[...])


def _ffn_layer(x, w_gate, w_up, w_down, g, b, *, tm=512, tf=1408):
    n = x.shape[0]
    d_ff = w_gate.shape[1]
    vec = pl.BlockSpec((1, D_MODEL), lambda i, f: (0, 0))
    return pl.pallas_call(
        _ffn_kernel,
        out_shape=jax.ShapeDtypeStruct((n, D_MODEL), F32),
        grid=(n // tm, d_ff // tf),
        in_specs=[pl.BlockSpec((tm, D_MODEL), lambda i, f: (i, 0)),
                  pl.BlockSpec((D_MODEL, tf), lambda i, f: (0, f)),
                  pl.BlockSpec((D_MODEL, tf), lambda i, f: (0, f)),
                  pl.BlockSpec((tf, D_MODEL), lambda i, f: (f, 0)),
                  vec, vec],
        out_specs=pl.BlockSpec((tm, D_MODEL), lambda i, f: (i, 0)),
        scratch_shapes=[pltpu.VMEM((tm, D_MODEL), BF16), pltpu.VMEM((tm, D_MODEL), F32)],
        compiler_params=_params(("arbitrary", "arbitrary")),
        name="swiglu_ln",
    )(x, w_gate.astype(BF16), w_up.astype(BF16), w_down.astype(BF16), _row(g), _row(b))


CHUNKS = D_MODEL // LANES


def _store_token_tiles(ref, y):
    for c in range(CHUNKS):
        ref[pl.ds(c, y.shape[0], stride=CHUNKS), :] = y[:, c * LANES:(c + 1) * LANES]


def _load_token_tiles(ref, rows):
    return jnp.concatenate([ref[pl.ds(c, rows, stride=CHUNKS), :] for c in range(CHUNKS)], axis=1)


def _router_kernel(x_ref, w_ref, meta_ref, cnt_ref, xt_ref, run_ref):
    i = pl.program_id(0)

    @pl.when(i == 0)
    def _():
        run_ref[...] = jnp.zeros_like(run_ref)

    _store_token_tiles(xt_ref, x_ref[...])
    logits = jnp.dot(x_ref[...], w_ref[...], preferred_element_type=F32, precision=lax.Precision.HIGHEST)
    tm = logits.shape[0]
    lane = lax.broadcasted_iota(jnp.int32, logits.shape, 1)
    logits = jnp.where(lane < N_EXPERTS, logits, -jnp.inf)
    v1 = jnp.max(logits, axis=1, keepdims=True)
    i1 = jnp.min(jnp.where(logits == v1, lane, LANES), axis=1, keepdims=True)
    rest = jnp.where(lane == i1, -jnp.inf, logits)
    v2 = jnp.max(rest, axis=1, keepdims=True)
    i2 = jnp.min(jnp.where(rest == v2, lane, LANES), axis=1, keepdims=True)
    e2 = jnp.exp(v2 - v1)
    g1 = 1.0 / (1.0 + e2)
    g2 = e2 / (1.0 + e2)

    sel = jnp.where(lane == i1, 1.0, jnp.where(lane == i2, 1.0, 0.0))
    before = (lax.broadcasted_iota(jnp.int32, (tm, tm), 1) < lax.broadcasted_iota(jnp.int32, (tm, tm), 0))
    rank = run_ref[0:1, :] + _dot(before.astype(BF16), sel.astype(BF16))
    r1 = jnp.sum(jnp.where(lane == i1, rank, 0.0), axis=1, keepdims=True)
    r2 = jnp.sum(jnp.where(lane == i2, rank, 0.0), axis=1, keepdims=True)
    run_ref[...] = run_ref[...] + jnp.sum(sel, axis=0, keepdims=True)
    cnt_ref[...] = run_ref[...]
    meta = jnp.zeros_like(logits)
    for k, val in enumerate((i1.astype(F32), i2.astype(F32), r1, r2, g1, g2)):
        meta = jnp.where(lane == k, val, meta)
    meta_ref[...] = meta


def _router(x, w_router, *, tm=512):
    n = x.shape[0]
    w = jnp.pad(w_router, ((0, 0), (0, LANES - N_EXPERTS)))
    return pl.pallas_call(
        _router_kernel,
        out_shape=(jax.ShapeDtypeStruct((n, LANES), F32), jax.ShapeDtypeStruct((8, LANES), F32),
                   jax.ShapeDtypeStruct((n * CHUNKS, LANES), F32)),
        grid=(n // tm,),
        in_specs=[pl.BlockSpec((tm, D_MODEL), lambda i: (i, 0)),
                  pl.BlockSpec((D_MODEL, LANES), lambda i: (0, 0))],
        out_specs=(pl.BlockSpec((tm, LANES), lambda i: (i, 0)), pl.BlockSpec((8, LANES), lambda i: (0, 0)),
                   pl.BlockSpec((tm * CHUNKS, LANES), lambda i: (i, 0))),
        scratch_shapes=[pltpu.VMEM((8, LANES), F32)],
        compiler_params=_params(("arbitrary",)),
        name="router_top2",
    )(x, w)


def _routing_tables(meta, counts, tm):
    n = meta.shape[0]
    cnt = counts[0, :N_EXPERTS].astype(jnp.int32)
    padded = (cnt + MOE_TILE - 1) // MOE_TILE * MOE_TILE
    ends = jnp.cumsum(padded)
    offs = ends - padded
    sel = meta[:, :4].astype(jnp.int32)
    pos = offs[sel[:, :2]] + sel[:, 2:]
    pos = pos.reshape(n // tm, tm, 2).transpose(0, 2, 1).reshape(n // tm, 2 * tm)
    max_tiles = 2 * n // MOE_TILE + N_EXPERTS
    first_row = jnp.arange(max_tiles, dtype=jnp.int32) * MOE_TILE
    tile_expert = jnp.minimum(jnp.sum(first_row[:, None] >= ends[None, :], axis=1), N_EXPERTS - 1).astype(jnp.int32)
    return pos, offs, ends, tile_expert, ends[-1:] // MOE_TILE, max_tiles


ZERO_ROWS = 128
DMA_GROUP = 16


def _token_rows(ref, start, rows):
    return ref.at[pl.ds(pl.multiple_of(start * CHUNKS, CHUNKS), rows * CHUNKS)]


def _dispatch_kernel(offs_ref, ends_ref, idx_hbm, xt_hbm, xs_hbm, idx_smem, zero_ref, sem_idx, sem_rows, *, tm):
    i = pl.program_id(0)
    idx_copy = pltpu.make_async_copy(idx_hbm.at[i], idx_smem, sem_idx)
    idx_copy.start()

    @pl.when(i == 0)
    def _():
        zero_ref[...] = jnp.zeros_like(zero_ref)

        def clear_tile(start):
            for k in range(MOE_TILE // ZERO_ROWS):
                clear = pltpu.make_async_copy(zero_ref, _token_rows(xs_hbm, start + k * ZERO_ROWS, ZERO_ROWS), sem_rows)
                clear.start()
                clear.wait()

        for e in range(N_EXPERTS):
            @pl.when(ends_ref[e] > offs_ref[e])
            def _():
                clear_tile(ends_ref[e] - MOE_TILE)

        def clear_unused(j, carry):
            clear_tile(j * MOE_TILE)
            return carry
        lax.fori_loop(ends_ref[N_EXPERTS - 1] // MOE_TILE, xs_hbm.shape[0] // (MOE_TILE * CHUNKS), clear_unused, 0)

    idx_copy.wait()

    def send(grp, carry):
        ts = [grp * DMA_GROUP + j for j in range(DMA_GROUP)]
        dst = [(idx_smem[t], idx_smem[tm + t]) for t in ts]
        for t, (p1, p2) in zip(ts, dst):
            src = _token_rows(xt_hbm, i * tm + t, 1)
            pltpu.make_async_copy(src, _token_rows(xs_hbm, p1, 1), sem_rows).start()
            pltpu.make_async_copy(src, _token_rows(xs_hbm, p2, 1), sem_rows).start()
        return carry
    lax.fori_loop(0, tm // DMA_GROUP, send, 0)
    for _ in range(2):
        pltpu.make_async_copy(_token_rows(xt_hbm, 0, tm), _token_rows(xs_hbm, 0, tm), sem_rows).wait()


def _dispatch(xt, idx, offs, ends, max_tiles, *, tm):
    n = xt.shape[0] // CHUNKS
    kern = functools.partial(_dispatch_kernel, tm=tm)
    return pl.pallas_call(
        kern,
        out_shape=jax.ShapeDtypeStruct((max_tiles * MOE_TILE * CHUNKS, LANES), F32),
        grid_spec=pltpu.PrefetchScalarGridSpec(
            num_scalar_prefetch=2,
            grid=(n // tm,),
            in_specs=[pl.BlockSpec(memory_space=pl.ANY), pl.BlockSpec(memory_space=pl.ANY)],
            out_specs=pl.BlockSpec(memory_space=pl.ANY),
            scratch_shapes=[pltpu.SMEM((2 * tm,), jnp.int32), pltpu.VMEM((ZERO_ROWS * CHUNKS, LANES), F32),
                            pltpu.SemaphoreType.DMA, pltpu.SemaphoreType.DMA]),
        compiler_params=_params(("arbitrary",)),
        name="moe_dispatch",
    )(offs, ends, idx, xt)


def _expert_kernel(te_ref, nu_ref, x_ref, wg_ref, wu_ref, wd_ref, o_ref, xb_ref, acc_ref):
    i = pl.program_id(0)
    f = pl.program_id(1)
    last_f = pl.num_programs(1) - 1

    @pl.when(i < nu_ref[0])
    def _():
        @pl.when(f == 0)
        def _():
            xb_ref[...] = _load_token_tiles(x_ref, MOE_TILE).astype(BF16)
            acc_ref[...] = jnp.zeros_like(acc_ref)

        xb = xb_ref[...]
        gate = _dot(xb, wg_ref[...].astype(BF16))
        up = _dot(xb, wu_ref[...].astype(BF16))
        h = gate * jax.nn.sigmoid(gate) * up
        acc_ref[...] += _dot(h.astype(BF16), wd_ref[...].astype(BF16))

        @pl.when(f == last_f)
        def _():
            _store_token_tiles(o_ref, acc_ref[...])

    @pl.when((i >= nu_ref[0]) & (f == 0))
    def _():
        o_ref[...] = jnp.zeros_like(o_ref)


def _experts(xs, tile_expert, n_used, w_gate, w_up, w_down, max_tiles, *, tf=512):
    d_ff = w_gate.shape[2]
    n_f = d_ff // tf

    def used_tile(i, nu):
        return jnp.minimum(i, jnp.maximum(nu[0] - 1, 0))

    def row_map(i, f, te, nu):
        return used_tile(i, nu), 0

    def out_map(i, f, te, nu):
        return i, 0

    def up_map(i, f, te, nu):
        return te[used_tile(i, nu)], 0, jnp.where(i < nu[0], f, n_f - 1)

    def down_map(i, f, te, nu):
        return te[used_tile(i, nu)], jnp.where(i < nu[0], f, n_f - 1), 0

    return pl.pallas_call(
        _expert_kernel,
        out_shape=jax.ShapeDtypeStruct((max_tiles * MOE_TILE * CHUNKS, LANES), F32),
        grid_spec=pltpu.PrefetchScalarGridSpec(
            num_scalar_prefetch=2,
            grid=(max_tiles, n_f),
            in_specs=[pl.BlockSpec((MOE_TILE * CHUNKS, LANES), row_map),
                      pl.BlockSpec((None, D_MODEL, tf), up_map),
                      pl.BlockSpec((None, D_MODEL, tf), up_map),
                      pl.BlockSpec((None, tf, D_MODEL), down_map)],
            out_specs=pl.BlockSpec((MOE_TILE * CHUNKS, LANES), out_map),
            scratch_shapes=[pltpu.VMEM((MOE_TILE, D_MODEL), BF16), pltpu.VMEM((MOE_TILE, D_MODEL), F32)]),
        compiler_params=_params(("arbitrary", "arbitrary")),
        name="moe_experts",
    )(tile_expert, n_used, xs, w_gate, w_up, w_down)


def _combine_kernel(idx_hbm, x_ref, meta_ref, y_hbm, g_ref, b_ref, o_ref, idx_smem, buf_ref, sem_idx, sem_rows, *, tm):
    i = pl.program_id(0)
    idx_copy = pltpu.make_async_copy(idx_hbm.at[i], idx_smem, sem_idx)
    idx_copy.start()
    idx_copy.wait()

    def fetch(grp, carry):
        ts = [grp * DMA_GROUP + j for j in range(DMA_GROUP)]
        src = [(idx_smem[t], idx_smem[tm + t]) for t in ts]
        for t, (p1, p2) in zip(ts, src):
            pltpu.make_async_copy(_token_rows(y_hbm, p1, 1), _token_rows(buf_ref.at[0], t, 1), sem_rows).start()
            pltpu.make_async_copy(_token_rows(y_hbm, p2, 1), _token_rows(buf_ref.at[1], t, 1), sem_rows).start()
        return carry
    lax.fori_loop(0, tm // DMA_GROUP, fetch, 0)
    for k in range(2):
        pltpu.make_async_copy(_token_rows(y_hbm, 0, tm), buf_ref.at[k], sem_rows).wait()

    meta = meta_ref[...]
    lane = lax.broadcasted_iota(jnp.int32, meta.shape, 1)
    g1 = jnp.sum(jnp.where(lane == META_G1, meta, 0.0), axis=1, keepdims=True)
    g2 = jnp.sum(jnp.where(lane == META_G2, meta, 0.0), axis=1, keepdims=True)
    mix = g1 * _load_token_tiles(buf_ref.at[0], tm) + g2 * _load_token_tiles(buf_ref.at[1], tm)
    o_ref[...] = _layer_norm(DN_ALPHA * x_ref[...] + mix, g_ref[...], b_ref[...])


def _combine(x, meta, y, idx, g, b, *, tm):
    n = x.shape[0]
    kern = functools.partial(_combine_kernel, tm=tm)
    vec = pl.BlockSpec((1, D_MODEL), lambda i: (0, 0))
    return pl.pallas_call(
        kern,
        out_shape=jax.ShapeDtypeStruct((n, D_MODEL), F32),
        grid=(n // tm,),
        in_specs=[pl.BlockSpec(memory_space=pl.ANY),
                  pl.BlockSpec((tm, D_MODEL), lambda i: (i, 0)),
                  pl.BlockSpec((tm, LANES), lambda i: (i, 0)),
                  pl.BlockSpec(memory_space=pl.ANY),
                  vec, vec],
        out_specs=pl.BlockSpec((tm, D_MODEL), lambda i: (i, 0)),
        scratch_shapes=[pltpu.SMEM((2 * tm,), jnp.int32), pltpu.VMEM((2, tm * CHUNKS, LANES), F32),
                        pltpu.SemaphoreType.DMA, pltpu.SemaphoreType.DMA],
        compiler_params=_params(("arbitrary",)),
        name="moe_combine_ln",
    )(idx, x, meta, y, _row(g), _row(b))


def _moe_layer(x, w_router, w_gate, w_up, w_down, g, b, *, tm=512):
    meta, counts, xt = _router(x, w_router)
    idx, offs, ends, tile_expert, n_used, max_tiles = _routing_tables(meta, counts, tm)
    xs = _dispatch(xt, idx, offs, ends, max_tiles, tm=tm)
    y = _experts(xs, tile_expert, n_used, w_gate, w_up, w_down, max_tiles)
    return _combine(x, meta, y, idx, g, b, tm=tm)


def _rope_kernel(pos_ref, invf_ref, c_ref, s1_ref, s2_ref):
    ang = pos_ref[...].astype(F32) * invf_ref[...]
    c = jnp.cos(ang)
    s = jnp.sin(ang)
    dd = lax.broadcasted_iota(jnp.int32, ang.shape, 1) % HEAD_DIM
    c_ref[...] = c
    s1_ref[...] = jnp.where(dd < ROT_DIM // 2, -s, 0.0)
    s2_ref[...] = jnp.where((dd >= ROT_DIM // 2) & (dd < ROT_DIM), s, 0.0)


def _rope_tables(positions, *, tm=2048):
    n = positions.size
    half = ROT_DIM // 2
    inv_freq = ROPE_THETA ** (-(jnp.arange(0, ROT_DIM, 2, dtype=F32) / ROT_DIM))
    per_head = jnp.concatenate([inv_freq, inv_freq, jnp.zeros((HEAD_DIM - 2 * half,), F32)])
    invf = jnp.tile(per_head, LANES // HEAD_DIM).reshape(1, LANES)
    out = jax.ShapeDtypeStruct((n, LANES), F32)
    spec = pl.BlockSpec((tm, LANES), lambda i: (i, 0))
    return pl.pallas_call(
        _rope_kernel,
        out_shape=(out, out, out),
        grid=(n // tm,),
        in_specs=[pl.BlockSpec((tm, 1), lambda i: (i, 0)), pl.BlockSpec((1, LANES), lambda i: (0, 0))],
        out_specs=(spec, spec, spec),
        compiler_params=_params(("arbitrary",)),
        name="rope_tables",
    )(positions.reshape(n, 1), invf)


def _qkv_kernel(x_ref, c_ref, s1_ref, s2_ref, w_ref, o_ref, *, n_res, rows):
    for r in range(n_res):
        xb = x_ref[:, r * D_MODEL:(r + 1) * D_MODEL].astype(BF16)
        y = _dot(xb, w_ref[...])
        c = c_ref[:, r * LANES:(r + 1) * LANES]
        s1 = s1_ref[:, r * LANES:(r + 1) * LANES]
        s2 = s2_ref[:, r * LANES:(r + 1) * LANES]
        for part in range(2):
            scale = HEAD_DIM ** -0.5 * LOG2_E if part == 0 else 1.0
            for blk in range(D_MODEL // LANES):
                lo = part * D_MODEL + blk * LANES
                t = y[:, lo:lo + LANES]
                rot = t * c + pltpu.roll(t, LANES - ROT_DIM // 2, 1) * s1 + pltpu.roll(t, ROT_DIM // 2, 1) * s2
                o_ref[r * rows:(r + 1) * rows, lo:lo + LANES] = (rot * scale).astype(BF16)
        o_ref[r * rows:(r + 1) * rows, 2 * D_MODEL:] = y[:, 2 * D_MODEL:].astype(BF16)


def _qkv_group(x, tabs, w, dil, *, tm=512):
    n = x.shape[0]
    seq_rows = SEQ // dil
    rows = min(tm, seq_rows)
    n_res = tm // rows
    seq_blocks = seq_rows // rows
    steps_per_seq = SEQ // tm
    assert n_res == 1 or seq_blocks == 1

    def view_map(i):
        s = i % steps_per_seq
        return (i // steps_per_seq) * seq_blocks + s % seq_blocks, s // seq_blocks

    xv = x.reshape(n // dil, dil * D_MODEL)
    tv = [t.reshape(n // dil, dil * LANES) for t in tabs]
    kern = functools.partial(_qkv_kernel, n_res=n_res, rows=rows)
    tab_spec = pl.BlockSpec((rows, n_res * LANES), view_map)
    return pl.pallas_call(
        kern,
        out_shape=jax.ShapeDtypeStruct((n, 3 * D_MODEL), BF16),
        grid=(n // tm,),
        in_specs=[pl.BlockSpec((rows, n_res * D_MODEL), view_map),
                  tab_spec, tab_spec, tab_spec,
                  pl.BlockSpec((D_MODEL, 3 * D_MODEL), lambda i: (0, 0))],
        out_specs=pl.BlockSpec((tm, 3 * D_MODEL), lambda i: (i, 0)),
        compiler_params=_params(("arbitrary",)),
        name="qkv_proj_dil%d" % dil,
    )(xv, *tv, w)


def _largest_divisor(n, cap):
    return max(d for d in range(1, cap + 1) if n % d == 0)


def _attn_kernel(*refs):
    qkv = refs[:9]
    o_ref = refs[9]
    acc_s, m_s, l_s = refs[10:13], refs[13:16], refs[16:19]

    lane = lax.broadcasted_iota(jnp.int32, (1, LANES), 1)
    head0 = lane < HEAD_DIM
    hm0 = head0.astype(BF16)
    hm1 = 1.0 - hm0
    qi = lax.broadcasted_iota(jnp.int32, (ATTN_BLOCK, ATTN_BLOCK), 0)
    kj = lax.broadcasted_iota(jnp.int32, (ATTN_BLOCK, ATTN_BLOCK), 1)
    cur_mask = jnp.where(qi <= kj, 0.0, NEG_BIG).astype(BF16)
    prev_mask = jnp.where(qi >= kj, 0.0, NEG_BIG).astype(BF16)
    mask_both = jnp.concatenate([prev_mask, cur_mask], axis=0)
    row_onehot = (qi == kj).astype(BF16)
    row_onehot = jnp.concatenate([row_onehot, row_onehot], axis=0)

    def scores(grp, base, has_prev):
        q_ref, k_ref, v_ref = qkv[3 * grp:3 * grp + 3]
        q = q_ref[pl.ds(base, ATTN_BLOCK), :]
        q2 = jnp.concatenate([q * hm0, q * hm1], axis=0)
        q2 = jnp.concatenate([q2, row_onehot], axis=1)
        if has_prev:
            kk = k_ref[pl.ds(base - ATTN_BLOCK, 2 * ATTN_BLOCK), :]
            vv = v_ref[pl.ds(base - ATTN_BLOCK, 2 * ATTN_BLOCK), :]
            kk = jnp.concatenate([kk, mask_both], axis=1)
        else:
            kk = k_ref[pl.ds(base, ATTN_BLOCK), :]
            vv = v_ref[pl.ds(base, ATTN_BLOCK), :]
            kk = jnp.concatenate([kk, cur_mask], axis=1)
        s = lax.dot_general(q2, kk, (((1,), (1,)), ((), ())), preferred_element_type=F32)
        return s, vv

    def finish(grp, s, vv, nat_start, dil):
        m = jnp.max(s, axis=1, keepdims=True)
        pb = jnp.exp2(s - m).astype(BF16)
        ones = jnp.ones_like(vv)
        o0 = _dot(pb[:ATTN_BLOCK], jnp.concatenate([vv * hm0, ones], axis=1))
        o1 = _dot(pb[ATTN_BLOCK:], jnp.concatenate([vv * hm1, ones], axis=1))
        acc = o0[:, :LANES] + o1[:, :LANES]
        mb = jnp.where(head0, m[:ATTN_BLOCK], m[ATTN_BLOCK:])
        lb = jnp.where(head0, o0[:, LANES:], o1[:, LANES:])
        if dil == 1:
            rows = pl.ds(nat_start, ATTN_BLOCK)
        else:
            rows = pl.ds(nat_start, ATTN_BLOCK, stride=dil)
        acc_s[grp][rows, :] = acc
        m_s[grp][rows, :] = mb
        l_s[grp][rows, :] = lb

    def run_blocks(specs):
        staged = [scores(grp, base, has_prev) for grp, base, _, _, has_prev in specs]
        for (grp, _, nat_start, dil, _), (s, vv) in zip(specs, staged):
            finish(grp, s, vv, nat_start, dil)

    for grp, dil in enumerate(ATTN_DILATIONS):
        seq_rows = SEQ // dil
        n_blocks = seq_rows // ATTN_BLOCK
        if n_blocks == 1:
            unroll = _largest_divisor(dil, ATTN_UNROLL)

            def singles(k, carry, grp=grp, dil=dil, unroll=unroll):
                rs = [k * unroll + j for j in range(unroll)]
                run_blocks([(grp, pl.multiple_of(r * ATTN_BLOCK, ATTN_BLOCK), r, dil, False) for r in rs])
                return carry
            lax.fori_loop(0, dil // unroll, singles, 0)
            continue
        run_blocks([(grp, r * seq_rows, r, dil, False) for r in range(dil)])

        n_banded = dil * (n_blocks - 1)
        unroll = _largest_divisor(n_banded, ATTN_UNROLL)

        def banded(k, carry, grp=grp, dil=dil, seq_rows=seq_rows, n_blocks=n_blocks, unroll=unroll):
            specs = []
            for j in range(unroll):
                flat = k * unroll + j
                r = flat // (n_blocks - 1)
                nb = 1 + flat % (n_blocks - 1)
                base = pl.multiple_of(r * seq_rows + nb * ATTN_BLOCK, ATTN_BLOCK)
                nat = nb * (ATTN_BLOCK * dil) + r
                specs.append((grp, base, pl.multiple_of(nat, ATTN_BLOCK) if dil == 1 else nat, dil, True))
            run_blocks(specs)
            return carry
        lax.fori_loop(0, n_banded // unroll, banded, 0)

    def merge(c, carry):
        rows = pl.ds(pl.multiple_of(c * ATTN_BLOCK, ATTN_BLOCK), ATTN_BLOCK)
        ms = [m_s[g][rows, :] for g in range(3)]
        top = jnp.maximum(jnp.maximum(ms[0], ms[1]), ms[2])
        num = jnp.zeros((ATTN_BLOCK, LANES), F32)
        den = jnp.zeros((ATTN_BLOCK, LANES), F32)
        for g in range(3):
            w = jnp.exp2(ms[g] - top)
            num = num + w * acc_s[g][rows, :]
            den = den + w * l_s[g][rows, :]
        o_ref[rows, :] = (num / den).astype(BF16)
        return carry
    lax.fori_loop(0, SEQ // ATTN_BLOCK, merge, 0)


def _attention(qkvs, n):
    n_pairs = D_MODEL // LANES
    in_specs, args = [], []
    for qkv in qkvs:
        for part in range(3):
            in_specs.append(pl.BlockSpec((SEQ, LANES), lambda b, hp, part=part: (b, part * n_pairs + hp)))
            args.append(qkv)
    scratch = [pltpu.VMEM((SEQ, LANES), F32) for _ in range(9)]
    return pl.pallas_call(
        _attn_kernel,
        out_shape=jax.ShapeDtypeStruct((n, D_MODEL), BF16),
        grid=(n // SEQ, n_pairs),
        in_specs=in_specs,
        out_specs=pl.BlockSpec((SEQ, LANES), lambda b, hp: (b, hp)),
        scratch_shapes=scratch,
        compiler_params=_params(("arbitrary", "arbitrary")),
        name="dilated_attention",
    )(*args)


def _proj_ln_kernel(x_ref, a_ref, w_ref, g_ref, b_ref, o_ref):
    h = _dot(a_ref[...], w_ref[...])
    o_ref[...] = _layer_norm(DN_ALPHA * x_ref[...] + h, g_ref[...], b_ref[...])


def _proj_ln(x, a, w, g, b, *, tm=512):
    n = x.shape[0]
    vec = pl.BlockSpec((1, D_MODEL), lambda i: (0, 0))
    return pl.pallas_call(
        _proj_ln_kernel,
        out_shape=jax.ShapeDtypeStruct((n, D_MODEL), F32),
        grid=(n // tm,),
        in_specs=[pl.BlockSpec((tm, D_MODEL), lambda i: (i, 0)),
                  pl.BlockSpec((tm, D_MODEL), lambda i: (i, 0)),
                  pl.BlockSpec((D_MODEL, D_MODEL), lambda i: (0, 0)),
                  vec, vec],
        out_specs=pl.BlockSpec((tm, D_MODEL), lambda i: (i, 0)),
        compiler_params=_params(("arbitrary",)),
        name="out_proj_ln",
    )(x, a, w.astype(BF16), _row(g), _row(b))


def _attn_layer(x, tabs, w_qkv, w_o, g, b):
    n = x.shape[0]
    w = w_qkv.astype(BF16)
    qkvs = [_qkv_group(x, tabs, w[:, grp * 3 * D_MODEL:(grp + 1) * 3 * D_MODEL], dil)
            for grp, dil in enumerate(ATTN_DILATIONS)]
    return _proj_ln(x, _attention(qkvs, n), w_o, g, b)


def kernel(x, positions, l0_pool_w_in, l0_pool_w_grp, l0_pool_scale, l0_ln1_g, l0_ln1_b, l0_ffn_w_gate, l0_ffn_w_up, l0_ffn_w_down, l0_ln2_g, l0_ln2_b, l1_attn_w_qkv, l1_attn_w_o, l1_ln1_g, l1_ln1_b, l1_moe_w_router, l1_moe_w_gate, l1_moe_w_up, l1_moe_w_down, l1_ln2_g, l1_ln2_b, l2_conv_w_in, l2_conv_w, l2_conv_w_out, l2_ln1_g, l2_ln1_b, l2_ffn_w_gate, l2_ffn_w_up, l2_ffn_w_down, l2_ln2_g, l2_ln2_b, l3_pool_w_in, l3_pool_w_grp, l3_pool_scale, l3_ln1_g, l3_ln1_b, l3_moe_w_router, l3_moe_w_gate, l3_moe_w_up, l3_moe_w_down, l3_ln2_g, l3_ln2_b):
    batch, seq, d = x.shape
    h = x.reshape(batch * seq, d)
    tabs = _rope_tables(positions)
    h = _pool_layer(h, l0_pool_w_in, l0_pool_w_grp, l0_pool_scale, l0_ln1_g, l0_ln1_b)
    h = _ffn_layer(h, l0_ffn_w_gate, l0_ffn_w_up, l0_ffn_w_down, l0_ln2_g, l0_ln2_b)
    h = _attn_layer(h, tabs, l1_attn_w_qkv, l1_attn_w_o, l1_ln1_g, l1_ln1_b)
    h = _moe_layer(h, l1_moe_w_router, l1_moe_w_gate, l1_moe_w_up, l1_moe_w_down, l1_ln2_g, l1_ln2_b)
    h = _conv_layer(h, l2_conv_w_in, l2_conv_w, l2_conv_w_out, l2_ln1_g, l2_ln1_b)
    h = _ffn_layer(h, l2_ffn_w_gate, l2_ffn_w_up, l2_ffn_w_down, l2_ln2_g, l2_ln2_b)
    h = _pool_layer(h, l3_pool_w_in, l3_pool_w_grp, l3_pool_scale, l3_ln1_g, l3_ln1_b)
    h = _moe_layer(h, l3_moe_w_router, l3_moe_w_gate, l3_moe_w_up, l3_moe_w_down, l3_ln2_g, l3_ln2_b)
    return h.reshape(batch, seq, d)
```

```python
import functools

import jax
import jax.numpy as jnp
from jax import lax
from jax.experimental import pallas as pl
from jax.experimental.pallas import tpu as pltpu

D_MODEL = 1024
SEQ = 2048
DEPTH = 4
POOL_WINDOWS = (2, 4, 8, 16)
POOL_GROUP_DIM = D_MODEL // len(POOL_WINDOWS)
ATTN_DILATIONS = (1, 4, 16)
ATTN_BLOCK = 128
ATTN_UNROLL = 16
HEAD_DIM = 64
ROT_DIM = HEAD_DIM // 4
ROPE_THETA = 500000.0
LOG2_E = 1.4426950408889634
CONV_WIDTH = 3
N_EXPERTS = 8
MOE_TILE = 1024
META_E1, META_E2, META_R1, META_R2, META_G1, META_G2 = range(6)
DN_ALPHA = (2 * DEPTH) ** 0.25
LN_EPS = 1e-5

LANES = 128
HALO = 16
NEG_BIG = -1e30
VMEM_LIMIT = 56 * 1024 * 1024

F32 = jnp.float32
BF16 = jnp.bfloat16


def _params(semantics, vmem=VMEM_LIMIT):
    return pltpu.CompilerParams(dimension_semantics=semantics, vmem_limit_bytes=vmem)


def _dot(a, b):
    return jnp.dot(a, b, preferred_element_type=F32)


def _layer_norm(z, g, b):
    mu = jnp.mean(z, axis=-1, keepdims=True)
    zc = z - mu
    var = jnp.mean(zc * zc, axis=-1, keepdims=True)
    return zc * lax.rsqrt(var + LN_EPS) * g + b


def _row(v):
    return v.reshape(1, -1)


def _pool_kernel(x_ref, w_in_ref, w_grp_ref, scale_ref, g_ref, b_ref, o_ref, halo_ref, *, tm, tiles_per_seq):
    i = pl.program_id(0)
    x = x_ref[...]
    u = _dot(x.astype(BF16), w_in_ref[...])

    @pl.when(i % tiles_per_seq == 0)
    def _():
        halo_ref[...] = jnp.zeros_like(halo_ref)

    buf = jnp.concatenate([halo_ref[...], u], axis=0)
    halo_ref[...] = u[tm - HALO:, :]
    t = (i % tiles_per_seq) * tm + lax.broadcasted_iota(jnp.int32, (tm, 1), 0)
    outs = []
    for grp, w in enumerate(POOL_WINDOWS):
        cols = slice(grp * POOL_GROUP_DIM, (grp + 1) * POOL_GROUP_DIM)
        s = buf[:, cols]
        k = 1
        while k < w:
            s = s + pltpu.roll(s, k, 0)
            k *= 2
        cnt = jnp.minimum(t + 1, w).astype(F32)
        pooled = s[HALO:, :] / cnt - u[:, cols]
        outs.append(_dot(pooled.astype(BF16), w_grp_ref[grp]))
    h = jnp.concatenate(outs, axis=1) * scale_ref[...]
    o_ref[...] = _layer_norm(DN_ALPHA * x + h, g_ref[...], b_ref[...])


def _pool_layer(x, w_in, w_grp, scale, g, b, *, tm=512):
    n = x.shape[0]
    kern = functools.partial(_pool_kernel, tm=tm, tiles_per_seq=SEQ // tm)
    vec = pl.BlockSpec((1, D_MODEL), lambda i: (0, 0))
    return pl.pallas_call(
        kern,
        out_shape=jax.ShapeDtypeStruct((n, D_MODEL), F32),
        grid=(n // tm,),
        in_specs=[pl.BlockSpec((tm, D_MODEL), lambda i: (i, 0)),
                  pl.BlockSpec((D_MODEL, D_MODEL), lambda i: (0, 0)),
                  pl.BlockSpec((len(POOL_WINDOWS), POOL_GROUP_DIM, POOL_GROUP_DIM), lambda i: (0, 0, 0)),
                  vec, vec, vec],
        out_specs=pl.BlockSpec((tm, D_MODEL), lambda i: (i, 0)),
        scratch_shapes=[pltpu.VMEM((HALO, D_MODEL), F32)],
        compiler_params=_params(("arbitrary",)),
        name="pool_mixer_ln",
    )(x, w_in.astype(BF16), w_grp.astype(BF16), _row(scale), _row(g), _row(b))


def _conv_kernel(x_ref, w_in_ref, cw_ref, w_out_ref, g_ref, b_ref, o_ref, halo_ref, *, tm, tiles_per_seq):
    i = pl.program_id(0)
    x = x_ref[...]
    proj = _dot(x.astype(BF16), w_in_ref[...])
    gate_b = proj[:, :D_MODEL]
    z = proj[:, D_MODEL:2 * D_MODEL] * proj[:, 2 * D_MODEL:]

    @pl.when(i % tiles_per_seq == 0)
    def _():
        halo_ref[...] = jnp.zeros_like(halo_ref)

    buf = jnp.concatenate([halo_ref[...], z], axis=0)
    halo_ref[...] = z[tm - HALO:, :]
    conv = cw_ref[0:1, :] * z
    for j in range(1, CONV_WIDTH):
        conv = conv + cw_ref[j:j + 1, :] * pltpu.roll(buf, j, 0)[HALO:, :]
    h = _dot((gate_b * conv).astype(BF16), w_out_ref[...])
    o_ref[...] = _layer_norm(DN_ALPHA * x + h, g_ref[...], b_ref[...])


def _conv_layer(x, w_in, conv_w, w_out, g, b, *, tm=512):
    n = x.shape[0]
    kern = functools.partial(_conv_kernel, tm=tm, tiles_per_seq=SEQ // tm)
    vec = pl.BlockSpec((1, D_MODEL), lambda i: (0, 0))
    return pl.pallas_call(
        kern,
        out_shape=jax.ShapeDtypeStruct((n, D_MODEL), F32),
        grid=(n // tm,),
        in_specs=[pl.BlockSpec((tm, D_MODEL), lambda i: (i, 0)),
                  pl.BlockSpec((D_MODEL, 3 * D_MODEL), lambda i: (0, 0)),
                  pl.BlockSpec((CONV_WIDTH, D_MODEL), lambda i: (0, 0)),
                  pl.BlockSpec((D_MODEL, D_MODEL), lambda i: (0, 0)),
                  vec, vec],
        out_specs=pl.BlockSpec((tm, D_MODEL), lambda i: (i, 0)),
        scratch_shapes=[pltpu.VMEM((HALO, D_MODEL), F32)],
        compiler_params=_params(("arbitrary",)),
        name="conv_mixer_ln",
    )(x, w_in.astype(BF16), conv_w, w_out.astype(BF16), _row(g), _row(b))


def _ffn_kernel(x_ref, wg_ref, wu_ref, wd_ref, g_ref, b_ref, o_ref, xb_ref, acc_ref):
    f = pl.program_id(1)

    @pl.when(f == 0)
    def _():
        xb_ref[...] = x_ref[...].astype(BF16)
        acc_ref[...] = jnp.zeros_like(acc_ref)

    xb = xb_ref[...]
    gate = _dot(xb, wg_ref[...])
    up = _dot(xb, wu_ref[...])
    h = gate * jax.nn.sigmoid(gate) * up
    acc_ref[...] += _dot(h.astype(BF16), wd_ref[...])

    @pl.when(f == pl.num_programs(1) - 1)
    def _():
        o_ref[...] = _layer_norm(DN_ALPHA * x_ref[...] + acc_ref[...], g_ref[...], b_ref[...])


def _ffn_layer(x, w_gate, w_up, w_down, g, b, *, tm=512, tf=1408):
    n = x.shape[0]
    d_ff = w_gate.shape[1]
    vec = pl.BlockSpec((1, D_MODEL), lambda i, f: (0, 0))
    return pl.pallas_call(
        _ffn_kernel,
        out_shape=jax.ShapeDtypeStruct((n, D_MODEL), F32),
        grid=(n // tm, d_ff // tf),
        in_specs=[pl.BlockSpec((tm, D_MODEL), lambda i, f: (i, 0)),
                  pl.BlockSpec((D_MODEL, tf), lambda i, f: (0, f)),
                  pl.BlockSpec((D_MODEL, tf), lambda i, f: (0, f)),
                  pl.BlockSpec((tf, D_MODEL), lambda i, f: (f, 0)),
                  vec, vec],
        out_specs=pl.BlockSpec((tm, D_MODEL), lambda i, f: (i, 0)),
        scratch_shapes=[pltpu.VMEM((tm, D_MODEL), BF16), pltpu.VMEM((tm, D_MODEL), F32)],
        compiler_params=_params(("arbitrary", "arbitrary")),
        name="swiglu_ln",
    )(x, w_gate.astype(BF16), w_up.astype(BF16), w_down.astype(BF16), _row(g), _row(b))


CHUNKS = D_MODEL // LANES


def _store_token_tiles(ref, y):
    for c in range(CHUNKS):
        ref[pl.ds(c, y.shape[0], stride=CHUNKS), :] = y[:, c * LANES:(c + 1) * LANES]


def _load_token_tiles(ref, rows):
    return jnp.concatenate([ref[pl.ds(c, rows, stride=CHUNKS), :] for c in range(CHUNKS)], axis=1)


def _router_kernel(x_ref, w_ref, meta_ref, cnt_ref, run_ref):
    i = pl.program_id(0)

    @pl.when(i == 0)
    def _():
        run_ref[...] = jnp.zeros_like(run_ref)

    logits = jnp.dot(x_ref[...], w_ref[...], preferred_element_type=F32, precision=lax.Precision.HIGHEST)
    tm = logits.shape[0]
    lane = lax.broadcasted_iota(jnp.int32, logits.shape, 1)
    logits = jnp.where(lane < N_EXPERTS, logits, -jnp.inf)
    v1 = jnp.max(logits, axis=1, keepdims=True)
    i1 = jnp.min(jnp.where(logits == v1, lane, LANES), axis=1, keepdims=True)
    rest = jnp.where(lane == i1, -jnp.inf, logits)
    v2 = jnp.max(rest, axis=1, keepdims=True)
    i2 = jnp.min(jnp.where(rest == v2, lane, LANES), axis=1, keepdims=True)
    e2 = jnp.exp(v2 - v1)
    g1 = 1.0 / (1.0 + e2)
    g2 = e2 / (1.0 + e2)

    sel = jnp.where(lane == i1, 1.0, jnp.where(lane == i2, 1.0, 0.0))
    before = (lax.broadcasted_iota(jnp.int32, (tm, tm), 1) < lax.broadcasted_iota(jnp.int32, (tm, tm), 0))
    rank = run_ref[0:1, :] + _dot(before.astype(BF16), sel.astype(BF16))
    r1 = jnp.sum(jnp.where(lane == i1, rank, 0.0), axis=1, keepdims=True)
    r2 = jnp.sum(jnp.where(lane == i2, rank, 0.0), axis=1, keepdims=True)
    run_ref[...] = run_ref[...] + jnp.sum(sel, axis=0, keepdims=True)
    cnt_ref[...] = run_ref[...]
    meta = jnp.zeros_like(logits)
    for k, val in enumerate((i1.astype(F32), i2.astype(F32), r1, r2, g1, g2)):
        meta = jnp.where(lane == k, val, meta)
    meta_ref[...] = meta


def _router(x, w_router, *, tm=512):
    n = x.shape[0]
    w = jnp.pad(w_router, ((0, 0), (0, LANES - N_EXPERTS)))
    return pl.pallas_call(
        _router_kernel,
        out_shape=(jax.ShapeDtypeStruct((n, LANES), F32), jax.ShapeDtypeStruct((8, LANES), F32)),
        grid=(n // tm,),
        in_specs=[pl.BlockSpec((tm, D_MODEL), lambda i: (i, 0)),
                  pl.BlockSpec((D_MODEL, LANES), lambda i: (0, 0))],
        out_specs=(pl.BlockSpec((tm, LANES), lambda i: (i, 0)), pl.BlockSpec((8, LANES), lambda i: (0, 0))),
        scratch_shapes=[pltpu.VMEM((8, LANES), F32)],
        compiler_params=_params(("arbitrary",)),
        name="router_top2",
    )(x, w)


def _routing_tables(meta, counts, tm):
    n = meta.shape[0]
    cnt = counts[0, :N_EXPERTS].astype(jnp.int32)
    padded = (cnt + MOE_TILE - 1) // MOE_TILE * MOE_TILE
    ends = jnp.cumsum(padded)
    offs = ends - padded
    sel = meta[:, :4].astype(jnp.int32)
    pos = offs[sel[:, :2]] + sel[:, 2:]
    pos = pos.reshape(n // tm, tm, 2).transpose(0, 2, 1).reshape(n // tm, 2 * tm)
    max_tiles = 2 * n // MOE_TILE + N_EXPERTS
    first_row = jnp.arange(max_tiles, dtype=jnp.int32) * MOE_TILE
    tile_expert = jnp.minimum(jnp.sum(first_row[:, None] >= ends[None, :], axis=1), N_EXPERTS - 1).astype(jnp.int32)
    return pos, offs, ends, tile_expert, ends[-1:] // MOE_TILE, max_tiles


ZERO_ROWS = 128
DMA_GROUP = 16


def _token_rows(ref, start, rows):
    return ref.at[pl.ds(pl.multiple_of(start * CHUNKS, CHUNKS), rows * CHUNKS)]


def _dispatch_kernel(offs_ref, ends_ref, idx_hbm, x_ref, xs_hbm, idx_smem, xt_ref, zero_ref, sem_idx, sem_rows, *, tm):
    i = pl.program_id(0)
    idx_copy = pltpu.make_async_copy(idx_hbm.at[i], idx_smem, sem_idx)
    idx_copy.start()

    @pl.when(i == 0)
    def _():
        zero_ref[...] = jnp.zeros_like(zero_ref)

        def clear_tile(start):
            for k in range(MOE_TILE // ZERO_ROWS):
                clear = pltpu.make_async_copy(zero_ref, _token_rows(xs_hbm, start + k * ZERO_ROWS, ZERO_ROWS), sem_rows)
                clear.start()
                clear.wait()

        for e in range(N_EXPERTS):
            @pl.when(ends_ref[e] > offs_ref[e])
            def _():
                clear_tile(ends_ref[e] - MOE_TILE)

        def clear_unused(j, carry):
            clear_tile(j * MOE_TILE)
            return carry
        lax.fori_loop(ends_ref[N_EXPERTS - 1] // MOE_TILE, xs_hbm.shape[0] // (MOE_TILE * CHUNKS), clear_unused, 0)

    _store_token_tiles(xt_ref, x_ref[...])
    idx_copy.wait()

    def send(grp, carry):
        ts = [grp * DMA_GROUP + j for j in range(DMA_GROUP)]
        dst = [(idx_smem[t], idx_smem[tm + t]) for t in ts]
        for t, (p1, p2) in zip(ts, dst):
            src = _token_rows(xt_ref, t, 1)
            pltpu.make_async_copy(src, _token_rows(xs_hbm, p1, 1), sem_rows).start()
            pltpu.make_async_copy(src, _token_rows(xs_hbm, p2, 1), sem_rows).start()
        return carry
    lax.fori_loop(0, tm // DMA_GROUP, send, 0)
    for _ in range(2):
        pltpu.make_async_copy(xt_ref, _token_rows(xs_hbm, 0, tm), sem_rows).wait()


def _dispatch(x, idx, offs, ends, max_tiles, *, tm):
    n = x.shape[0]
    kern = functools.partial(_dispatch_kernel, tm=tm)
    return pl.pallas_call(
        kern,
        out_shape=jax.ShapeDtypeStruct((max_tiles * MOE_TILE * CHUNKS, LANES), F32),
        grid_spec=pltpu.PrefetchScalarGridSpec(
            num_scalar_prefetch=2,
            grid=(n // tm,),
            in_specs=[pl.BlockSpec(memory_space=pl.ANY),
                      pl.BlockSpec((tm, D_MODEL), lambda i, offs, ends: (i, 0))],
            out_specs=pl.BlockSpec(memory_space=pl.ANY),
            scratch_shapes=[pltpu.SMEM((2 * tm,), jnp.int32), pltpu.VMEM((tm * CHUNKS, LANES), F32),
                            pltpu.VMEM((ZERO_ROWS * CHUNKS, LANES), F32),
                            pltpu.SemaphoreType.DMA, pltpu.SemaphoreType.DMA]),
        compiler_params=_params(("arbitrary",)),
        name="moe_dispatch",
    )(offs, ends, idx, x)


def _expert_kernel(te_ref, nu_ref, x_ref, wg_ref, wu_ref, wd_ref, o_ref, xb_ref, acc_ref):
    i = pl.program_id(0)
    f = pl.program_id(1)
    last_f = pl.num_programs(1) - 1

    @pl.when(i < nu_ref[0])
    def _():
        @pl.when(f == 0)
        def _():
            xb_ref[...] = _load_token_tiles(x_ref, MOE_TILE).astype(BF16)
            acc_ref[...] = jnp.zeros_like(acc_ref)

        xb = xb_ref[...]
        gate = _dot(xb, wg_ref[...].astype(BF16))
        up = _dot(xb, wu_ref[...].astype(BF16))
        h = gate * jax.nn.sigmoid(gate) * up
        acc_ref[...] += _dot(h.astype(BF16), wd_ref[...].astype(BF16))

        @pl.when(f == last_f)
        def _():
            _store_token_tiles(o_ref, acc_ref[...])

    @pl.when((i >= nu_ref[0]) & (f == 0))
    def _():
        o_ref[...] = jnp.zeros_like(o_ref)


def _experts(xs, tile_expert, n_used, w_gate, w_up, w_down, max_tiles, *, tf=512):
    d_ff = w_gate.shape[2]
    n_f = d_ff // tf

    def used_tile(i, nu):
        return jnp.minimum(i, jnp.maximum(nu[0] - 1, 0))

    def row_map(i, f, te, nu):
        return used_tile(i, nu), 0

    def out_map(i, f, te, nu):
        return i, 0

    def up_map(i, f, te, nu):
        return te[used_tile(i, nu)], 0, jnp.where(i < nu[0], f, n_f - 1)

    def down_map(i, f, te, nu):
        return te[used_tile(i, nu)], jnp.where(i < nu[0], f, n_f - 1), 0

    return pl.pallas_call(
        _expert_kernel,
        out_shape=jax.ShapeDtypeStruct((max_tiles * MOE_TILE * CHUNKS, LANES), F32),
        grid_spec=pltpu.PrefetchScalarGridSpec(
            num_scalar_prefetch=2,
            grid=(max_tiles, n_f),
            in_specs=[pl.BlockSpec((MOE_TILE * CHUNKS, LANES), row_map),
                      pl.BlockSpec((None, D_MODEL, tf), up_map),
                      pl.BlockSpec((None, D_MODEL, tf), up_map),
                      pl.BlockSpec((None, tf, D_MODEL), down_map)],
            out_specs=pl.BlockSpec((MOE_TILE * CHUNKS, LANES), out_map),
            scratch_shapes=[pltpu.VMEM((MOE_TILE, D_MODEL), BF16), pltpu.VMEM((MOE_TILE, D_MODEL), F32)]),
        compiler_params=_params(("arbitrary", "arbitrary")),
        name="moe_experts",
    )(tile_expert, n_used, xs, w_gate, w_up, w_down)


def _combine_kernel(idx_hbm, x_ref, meta_ref, y_hbm, g_ref, b_ref, o_ref, idx_smem, buf_ref, sem_idx, sem_rows, *, tm):
    i = pl.program_id(0)
    idx_copy = pltpu.make_async_copy(idx_hbm.at[i], idx_smem, sem_idx)
    idx_copy.start()
    idx_copy.wait()

    def fetch(grp, carry):
        ts = [grp * DMA_GROUP + j for j in range(DMA_GROUP)]
        src = [(idx_smem[t], idx_smem[tm + t]) for t in ts]
        for t, (p1, p2) in zip(ts, src):
            pltpu.make_async_copy(_token_rows(y_hbm, p1, 1), _token_rows(buf_ref.at[0], t, 1), sem_rows).start()
            pltpu.make_async_copy(_token_rows(y_hbm, p2, 1), _token_rows(buf_ref.at[1], t, 1), sem_rows).start()
        return carry
    lax.fori_loop(0, tm // DMA_GROUP, fetch, 0)
    for k in range(2):
        pltpu.make_async_copy(_token_rows(y_hbm, 0, tm), buf_ref.at[k], sem_rows).wait()

    meta = meta_ref[...]
    lane = lax.broadcasted_iota(jnp.int32, meta.shape, 1)
    g1 = jnp.sum(jnp.where(lane == META_G1, meta, 0.0), axis=1, keepdims=True)
    g2 = jnp.sum(jnp.where(lane == META_G2, meta, 0.0), axis=1, keepdims=True)
    mix = g1 * _load_token_tiles(buf_ref.at[0], tm) + g2 * _load_token_tiles(buf_ref.at[1], tm)
    o_ref[...] = _layer_norm(DN_ALPHA * x_ref[...] + mix, g_ref[...], b_ref[...])


def _combine(x, meta, y, idx, g, b, *, tm):
    n = x.shape[0]
    kern = functools.partial(_combine_kernel, tm=tm)
    vec = pl.BlockSpec((1, D_MODEL), lambda i: (0, 0))
    return pl.pallas_call(
        kern,
        out_shape=jax.ShapeDtypeStruct((n, D_MODEL), F32),
        grid=(n // tm,),
        in_specs=[pl.BlockSpec(memory_space=pl.ANY),
                  pl.BlockSpec((tm, D_MODEL), lambda i: (i, 0)),
                  pl.BlockSpec((tm, LANES), lambda i: (i, 0)),
                  pl.BlockSpec(memory_space=pl.ANY),
                  vec, vec],
        out_specs=pl.BlockSpec((tm, D_MODEL), lambda i: (i, 0)),
        scratch_shapes=[pltpu.SMEM((2 * tm,), jnp.int32), pltpu.VMEM((2, tm * CHUNKS, LANES), F32),
                        pltpu.SemaphoreType.DMA, pltpu.SemaphoreType.DMA],
        compiler_params=_params(("arbitrary",)),
        name="moe_combine_ln",
    )(idx, x, meta, y, _row(g), _row(b))


def _moe_layer(x, w_router, w_gate, w_up, w_down, g, b, *, tm=512):
    meta, counts = _router(x, w_router)
    idx, offs, ends, tile_expert, n_used, max_tiles = _routing_tables(meta, counts, tm)
    xs = _dispatch(x, idx, offs, ends, max_tiles, tm=tm)
    y = _experts(xs, tile_expert, n_used, w_gate, w_up, w_down, max_tiles)
    return _combine(x, meta, y, idx, g, b, tm=tm)


def _rope_kernel(pos_ref, invf_ref, c_ref, s1_ref, s2_ref):
    ang = pos_ref[...].astype(F32) * invf_ref[...]
    c = jnp.cos(ang)
    s = jnp.sin(ang)
    dd = lax.broadcasted_iota(jnp.int32, ang.shape, 1) % HEAD_DIM
    c_ref[...] = c
    s1_ref[...] = jnp.where(dd < ROT_DIM // 2, -s, 0.0)
    s2_ref[...] = jnp.where((dd >= ROT_DIM // 2) & (dd < ROT_DIM), s, 0.0)


def _rope_tables(positions, *, tm=2048):
    n = positions.size
    half = ROT_DIM // 2
    inv_freq = ROPE_THETA ** (-(jnp.arange(0, ROT_DIM, 2, dtype=F32) / ROT_DIM))
    per_head = jnp.concatenate([inv_freq, inv_freq, jnp.zeros((HEAD_DIM - 2 * half,), F32)])
    invf = jnp.tile(per_head, LANES // HEAD_DIM).reshape(1, LANES)
    out = jax.ShapeDtypeStruct((n, LANES), F32)
    spec = pl.BlockSpec((tm, LANES), lambda i: (i, 0))
    return pl.pallas_call(
        _rope_kernel,
        out_shape=(out, out, out),
        grid=(n // tm,),
        in_specs=[pl.BlockSpec((tm, 1), lambda i: (i, 0)), pl.BlockSpec((1, LANES), lambda i: (0, 0))],
        out_specs=(spec, spec, spec),
        compiler_params=_params(("arbitrary",)),
        name="rope_tables",
    )(positions.reshape(n, 1), invf)


def _qkv_kernel(x_ref, c_ref, s1_ref, s2_ref, w_ref, o_ref, *, n_res, rows):
    for r in range(n_res):
        xb = x_ref[:, r * D_MODEL:(r + 1) * D_MODEL].astype(BF16)
        y = _dot(xb, w_ref[...])
        c = c_ref[:, r * LANES:(r + 1) * LANES]
        s1 = s1_ref[:, r * LANES:(r + 1) * LANES]
        s2 = s2_ref[:, r * LANES:(r + 1) * LANES]
        for part in range(2):
            scale = HEAD_DIM ** -0.5 * LOG2_E if part == 0 else 1.0
            for blk in range(D_MODEL // LANES):
                lo = part * D_MODEL + blk * LANES
                t = y[:, lo:lo + LANES]
                rot = t * c + pltpu.roll(t, LANES - ROT_DIM // 2, 1) * s1 + pltpu.roll(t, ROT_DIM // 2, 1) * s2
                o_ref[r * rows:(r + 1) * rows, lo:lo + LANES] = (rot * scale).astype(BF16)
        o_ref[r * rows:(r + 1) * rows, 2 * D_MODEL:] = y[:, 2 * D_MODEL:].astype(BF16)


def _qkv_group(x, tabs, w, dil, *, tm=512):
    n = x.shape[0]
    seq_rows = SEQ // dil
    rows = min(tm, seq_rows)
    n_res = tm // rows
    seq_blocks = seq_rows // rows
    steps_per_seq = SEQ // tm
    assert n_res == 1 or seq_blocks == 1

    def view_map(i):
        s = i % steps_per_seq
        return (i // steps_per_seq) * seq_blocks + s % seq_blocks, s // seq_blocks

    xv = x.reshape(n // dil, dil * D_MODEL)
    tv = [t.reshape(n // dil, dil * LANES) for t in tabs]
    kern = functools.partial(_qkv_kernel, n_res=n_res, rows=rows)
    tab_spec = pl.BlockSpec((rows, n_res * LANES), view_map)
    return pl.pallas_call(
        kern,
        out_shape=jax.ShapeDtypeStruct((n, 3 * D_MODEL), BF16),
        grid=(n // tm,),
        in_specs=[pl.BlockSpec((rows, n_res * D_MODEL), view_map),
                  tab_spec, tab_spec, tab_spec,
                  pl.BlockSpec((D_MODEL, 3 * D_MODEL), lambda i: (0, 0))],
        out_specs=pl.BlockSpec((tm, 3 * D_MODEL), lambda i: (i, 0)),
        compiler_params=_params(("arbitrary",)),
        name="qkv_proj_dil%d" % dil,
    )(xv, *tv, w)


def _largest_divisor(n, cap):
    return max(d for d in range(1, cap + 1) if n % d == 0)


def _attn_kernel(*refs):
    qkv = refs[:9]
    o_ref = refs[9]
    acc_s, m_s, l_s = refs[10:13], refs[13:16], refs[16:19]

    lane = lax.broadcasted_iota(jnp.int32, (1, LANES), 1)
    head0 = lane < HEAD_DIM
    hm0 = head0.astype(BF16)
    hm1 = 1.0 - hm0
    qi = lax.broadcasted_iota(jnp.int32, (ATTN_BLOCK, ATTN_BLOCK), 0)
    kj = lax.broadcasted_iota(jnp.int32, (ATTN_BLOCK, ATTN_BLOCK), 1)
    cur_mask = jnp.where(qi <= kj, 0.0, NEG_BIG).astype(BF16)
    prev_mask = jnp.where(qi >= kj, 0.0, NEG_BIG).astype(BF16)
    mask_both = jnp.concatenate([prev_mask, cur_mask], axis=0)
    row_onehot = (qi == kj).astype(BF16)
    row_onehot = jnp.concatenate([row_onehot, row_onehot], axis=0)

    def scores(grp, base, has_prev):
        q_ref, k_ref, v_ref = qkv[3 * grp:3 * grp + 3]
        q = q_ref[pl.ds(base, ATTN_BLOCK), :]
        q2 = jnp.concatenate([q * hm0, q * hm1], axis=0)
        q2 = jnp.concatenate([q2, row_onehot], axis=1)
        if has_prev:
            kk = k_ref[pl.ds(base - ATTN_BLOCK, 2 * ATTN_BLOCK), :]
            vv = v_ref[pl.ds(base - ATTN_BLOCK, 2 * ATTN_BLOCK), :]
            kk = jnp.concatenate([kk, mask_both], axis=1)
        else:
            kk = k_ref[pl.ds(base, ATTN_BLOCK), :]
            vv = v_ref[pl.ds(base, ATTN_BLOCK), :]
            kk = jnp.concatenate([kk, cur_mask], axis=1)
        s = lax.dot_general(q2, kk, (((1,), (1,)), ((), ())), preferred_element_type=F32)
        return s, vv

    def finish(grp, s, vv, nat_start, dil):
        m = jnp.max(s, axis=1, keepdims=True)
        pb = jnp.exp2(s - m).astype(BF16)
        ones = jnp.ones_like(vv)
        o0 = _dot(pb[:ATTN_BLOCK], jnp.concatenate([vv * hm0, ones], axis=1))
        o1 = _dot(pb[ATTN_BLOCK:], jnp.concatenate([vv * hm1, ones], axis=1))
        acc = o0[:, :LANES] + o1[:, :LANES]
        mb = jnp.where(head0, m[:ATTN_BLOCK], m[ATTN_BLOCK:])
        lb = jnp.where(head0, o0[:, LANES:], o1[:, LANES:])
        if dil == 1:
            rows = pl.ds(nat_start, ATTN_BLOCK)
        else:
            rows = pl.ds(nat_start, ATTN_BLOCK, stride=dil)
        acc_s[grp][rows, :] = acc
        m_s[grp][rows, :] = mb
        l_s[grp][rows, :] = lb

    def run_blocks(specs):
        staged = [scores(grp, base, has_prev) for grp, base, _, _, has_prev in specs]
        for (grp, _, nat_start, dil, _), (s, vv) in zip(specs, staged):
            finish(grp, s, vv, nat_start, dil)

    for grp, dil in enumerate(ATTN_DILATIONS):
        seq_rows = SEQ // dil
        n_blocks = seq_rows // ATTN_BLOCK
        if n_blocks == 1:
            unroll = _largest_divisor(dil, ATTN_UNROLL)

            def singles(k, carry, grp=grp, dil=dil, unroll=unroll):
                rs = [k * unroll + j for j in range(unroll)]
                run_blocks([(grp, pl.multiple_of(r * ATTN_BLOCK, ATTN_BLOCK), r, dil, False) for r in rs])
                return carry
            lax.fori_loop(0, dil // unroll, singles, 0)
            continue
        run_blocks([(grp, r * seq_rows, r, dil, False) for r in range(dil)])

        n_banded = dil * (n_blocks - 1)
        unroll = _largest_divisor(n_banded, ATTN_UNROLL)

        def banded(k, carry, grp=grp, dil=dil, seq_rows=seq_rows, n_blocks=n_blocks, unroll=unroll):
            specs = []
            for j in range(unroll):
                flat = k * unroll + j
                r = flat // (n_blocks - 1)
                nb = 1 + flat % (n_blocks - 1)
                base = pl.multiple_of(r * seq_rows + nb * ATTN_BLOCK, ATTN_BLOCK)
                nat = nb * (ATTN_BLOCK * dil) + r
                specs.append((grp, base, pl.multiple_of(nat, ATTN_BLOCK) if dil == 1 else nat, dil, True))
            run_blocks(specs)
            return carry
        lax.fori_loop(0, n_banded // unroll, banded, 0)

    def merge(c, carry):
        rows = pl.ds(pl.multiple_of(c * ATTN_BLOCK, ATTN_BLOCK), ATTN_BLOCK)
        ms = [m_s[g][rows, :] for g in range(3)]
        top = jnp.maximum(jnp.maximum(ms[0], ms[1]), ms[2])
        num = jnp.zeros((ATTN_BLOCK, LANES), F32)
        den = jnp.zeros((ATTN_BLOCK, LANES), F32)
        for g in range(3):
            w = jnp.exp2(ms[g] - top)
            num = num + w * acc_s[g][rows, :]
            den = den + w * l_s[g][rows, :]
        o_ref[rows, :] = (num / den).astype(BF16)
        return carry
    lax.fori_loop(0, SEQ // ATTN_BLOCK, merge, 0)


def _attention(qkvs, n):
    n_pairs = D_MODEL // LANES
    in_specs, args = [], []
    for qkv in qkvs:
        for part in range(3):
            in_specs.append(pl.BlockSpec((SEQ, LANES), lambda b, hp, part=part: (b, part * n_pairs + hp)))
            args.append(qkv)
    scratch = [pltpu.VMEM((SEQ, LANES), F32) for _ in range(9)]
    return pl.pallas_call(
        _attn_kernel,
        out_shape=jax.ShapeDtypeStruct((n, D_MODEL), BF16),
        grid=(n // SEQ, n_pairs),
        in_specs=in_specs,
        out_specs=pl.BlockSpec((SEQ, LANES), lambda b, hp: (b, hp)),
        scratch_shapes=scratch,
        compiler_params=_params(("arbitrary", "arbitrary")),
        name="dilated_attention",
    )(*args)


def _proj_ln_kernel(x_ref, a_ref, w_ref, g_ref, b_ref, o_ref):
    h = _dot(a_ref[...], w_ref[...])
    o_ref[...] = _layer_norm(DN_ALPHA * x_ref[...] + h, g_ref[...], b_ref[...])


def _proj_ln(x, a, w, g, b, *, tm=512):
    n = x.shape[0]
    vec = pl.BlockSpec((1, D_MODEL), lambda i: (0, 0))
    return pl.pallas_call(
        _proj_ln_kernel,
        out_shape=jax.ShapeDtypeStruct((n, D_MODEL), F32),
        grid=(n // tm,),
        in_specs=[pl.BlockSpec((tm, D_MODEL), lambda i: (i, 0)),
                  pl.BlockSpec((tm, D_MODEL), lambda i: (i, 0)),
                  pl.BlockSpec((D_MODEL, D_MODEL), lambda i: (0, 0)),
                  vec, vec],
        out_specs=pl.BlockSpec((tm, D_MODEL), lambda i: (i, 0)),
        compiler_params=_params(("arbitrary",)),
        name="out_proj_ln",
    )(x, a, w.astype(BF16), _row(g), _row(b))


def _attn_layer(x, tabs, w_qkv, w_o, g, b):
    n = x.shape[0]
    w = w_qkv.astype(BF16)
    qkvs = [_qkv_group(x, tabs, w[:, grp * 3 * D_MODEL:(grp + 1) * 3 * D_MODEL], dil)
            for grp, dil in enumerate(ATTN_DILATIONS)]
    return _proj_ln(x, _attention(qkvs, n), w_o, g, b)


def kernel(x, positions, l0_pool_w_in, l0_pool_w_grp, l0_pool_scale, l0_ln1_g, l0_ln1_b, l0_ffn_w_gate, l0_ffn_w_up, l0_ffn_w_down, l0_ln2_g, l0_ln2_b, l1_attn_w_qkv, l1_attn_w_o, l1_ln1_g, l1_ln1_b, l1_moe_w_router, l1_moe_w_gate, l1_moe_w_up, l1_moe_w_down, l1_ln2_g, l1_ln2_b, l2_conv_w_in, l2_conv_w, l2_conv_w_out, l2_ln1_g, l2_ln1_b, l2_ffn_w_gate, l2_ffn_w_up, l2_ffn_w_down, l2_ln2_g, l2_ln2_b, l3_pool_w_in, l3_pool_w_grp, l3_pool_scale, l3_ln1_g, l3_ln1_b, l3_moe_w_router, l3_moe_w_gate, l3_moe_w_up, l3_moe_w_down, l3_ln2_g, l3_ln2_b):
    batch, seq, d = x.shape
    h = x.reshape(batch * seq, d)
    tabs = _rope_tables(positions)
    h = _pool_layer(h, l0_pool_w_in, l0_pool_w_grp, l0_pool_scale, l0_ln1_g, l0_ln1_b)
    h = _ffn_layer(h, l0_ffn_w_gate, l0_ffn_w_up, l0_ffn_w_down, l0_ln2_g, l0_ln2_b)
    h = _attn_layer(h, tabs, l1_attn_w_qkv, l1_attn_w_o, l1_ln1_g, l1_ln1_b)
    h = _moe_layer(h, l1_moe_w_router, l1_moe_w_gate, l1_moe_w_up, l1_moe_w_down, l1_ln2_g, l1_ln2_b)
    h = _conv_layer(h, l2_conv_w_in, l2_conv_w, l2_conv_w_out, l2_ln1_g, l2_ln1_b)
    h = _ffn_layer(h, l2_ffn_w_gate, l2_ffn_w_up, l2_ffn_w_down, l2_ln2_g, l2_ln2_b)
    h = _pool_layer(h, l3_pool_w_in, l3_pool_w_grp, l3_pool_scale, l3_ln1_g, l3_ln1_b)
    h = _moe_layer(h, l3_moe_w_router, l3_moe_w_gate, l3_moe_w_up, l3_moe_w_down, l3_ln2_g, l3_ln2_b)
    return h.reshape(batch, seq, d)
```

```python
import functools

import jax
import jax.numpy as jnp
from jax import lax
from jax.experimental import pallas as pl
from jax.experimental.pallas import tpu as pltpu

D_MODEL = 1024
SEQ = 2048
DEPTH = 4
POOL_WINDOWS = (2, 4, 8, 16)
POOL_GROUP_DIM = D_MODEL // len(POOL_WINDOWS)
ATTN_DILATIONS = (1, 4, 16)
ATTN_BLOCK = 128
ATTN_UNROLL = 16
HEAD_DIM = 64
ROT_DIM = HEAD_DIM // 4
ROPE_THETA = 500000.0
LOG2_E = 1.4426950408889634
CONV_WIDTH = 3
N_EXPERTS = 8
MOE_TILE = 1024
META_E1, META_E2, META_R1, META_R2, META_G1, META_G2 = range(6)
DN_ALPHA = (2 * DEPTH) ** 0.25
LN_EPS = 1e-5

LANES = 128
HALO = 16
NEG_BIG = -1e30
VMEM_LIMIT = 56 * 1024 * 1024

F32 = jnp.float32
BF16 = jnp.bfloat16


def _params(semantics, vmem=VMEM_LIMIT):
    return pltpu.CompilerParams(dimension_semantics=semantics, vmem_limit_bytes=vmem)


def _dot(a, b):
    return jnp.dot(a, b, preferred_element_type=F32)


def _layer_norm(z, g, b):
    mu = jnp.mean(z, axis=-1, keepdims=True)
    zc = z - mu
    var = jnp.mean(zc * zc, axis=-1, keepdims=True)
    return zc * lax.rsqrt(var + LN_EPS) * g + b


def _row(v):
    return v.reshape(1, -1)


def _pool_kernel(x_ref, w_in_ref, w_grp_ref, scale_ref, g_ref, b_ref, o_ref, halo_ref, *, tm, tiles_per_seq):
    i = pl.program_id(0)
    x = x_ref[...]
    u = _dot(x.astype(BF16), w_in_ref[...])

    @pl.when(i % tiles_per_seq == 0)
    def _():
        halo_ref[...] = jnp.zeros_like(halo_ref)

    buf = jnp.concatenate([halo_ref[...], u], axis=0)
    halo_ref[...] = u[tm - HALO:, :]
    t = (i % tiles_per_seq) * tm + lax.broadcasted_iota(jnp.int32, (tm, 1), 0)
    outs = []
    for grp, w in enumerate(POOL_WINDOWS):
        cols = slice(grp * POOL_GROUP_DIM, (grp + 1) * POOL_GROUP_DIM)
        s = buf[:, cols]
        k = 1
        while k < w:
            s = s + pltpu.roll(s, k, 0)
            k *= 2
        cnt = jnp.minimum(t + 1, w).astype(F32)
        pooled = s[HALO:, :] / cnt - u[:, cols]
        outs.append(_dot(pooled.astype(BF16), w_grp_ref[grp]))
    h = jnp.concatenate(outs, axis=1) * scale_ref[...]
    o_ref[...] = _layer_norm(DN_ALPHA * x + h, g_ref[...], b_ref[...])


def _pool_layer(x, w_in, w_grp, scale, g, b, *, tm=512):
    n = x.shape[0]
    kern = functools.partial(_pool_kernel, tm=tm, tiles_per_seq=SEQ // tm)
    vec = pl.BlockSpec((1, D_MODEL), lambda i: (0, 0))
    return pl.pallas_call(
        kern,
        out_shape=jax.ShapeDtypeStruct((n, D_MODEL), F32),
        grid=(n // tm,),
        in_specs=[pl.BlockSpec((tm, D_MODEL), lambda i: (i, 0)),
                  pl.BlockSpec((D_MODEL, D_MODEL), lambda i: (0, 0)),
                  pl.BlockSpec((len(POOL_WINDOWS), POOL_GROUP_DIM, POOL_GROUP_DIM), lambda i: (0, 0, 0)),
                  vec, vec, vec],
        out_specs=pl.BlockSpec((tm, D_MODEL), lambda i: (i, 0)),
        scratch_shapes=[pltpu.VMEM((HALO, D_MODEL), F32)],
        compiler_params=_params(("arbitrary",)),
        name="pool_mixer_ln",
    )(x, w_in.astype(BF16), w_grp.astype(BF16), _row(scale), _row(g), _row(b))


def _conv_kernel(x_ref, w_in_ref, cw_ref, w_out_ref, g_ref, b_ref, o_ref, halo_ref, *, tm, tiles_per_seq):
    i = pl.program_id(0)
    x = x_ref[...]
    proj = _dot(x.astype(BF16), w_in_ref[...])
    gate_b = proj[:, :D_MODEL]
    z = proj[:, D_MODEL:2 * D_MODEL] * proj[:, 2 * D_MODEL:]

    @pl.when(i % tiles_per_seq == 0)
    def _():
        halo_ref[...] = jnp.zeros_like(halo_ref)

    buf = jnp.concatenate([halo_ref[...], z], axis=0)
    halo_ref[...] = z[tm - HALO:, :]
    conv = cw_ref[0:1, :] * z
    for j in range(1, CONV_WIDTH):
        conv = conv + cw_ref[j:j + 1, :] * pltpu.roll(buf, j, 0)[HALO:, :]
    h = _dot((gate_b * conv).astype(BF16), w_out_ref[...])
    o_ref[...] = _layer_norm(DN_ALPHA * x + h, g_ref[...], b_ref[...])


def _conv_layer(x, w_in, conv_w, w_out, g, b, *, tm=512):
    n = x.shape[0]
    kern = functools.partial(_conv_kernel, tm=tm, tiles_per_seq=SEQ // tm)
    vec = pl.BlockSpec((1, D_MODEL), lambda i: (0, 0))
    return pl.pallas_call(
        kern,
        out_shape=jax.ShapeDtypeStruct((n, D_MODEL), F32),
        grid=(n // tm,),
        in_specs=[pl.BlockSpec((tm, D_MODEL), lambda i: (i, 0)),
                  pl.BlockSpec((D_MODEL, 3 * D_MODEL), lambda i: (0, 0)),
                  pl.BlockSpec((CONV_WIDTH, D_MODEL), lambda i: (0, 0)),
                  pl.BlockSpec((D_MODEL, D_MODEL), lambda i: (0, 0)),
                  vec, vec],
        out_specs=pl.BlockSpec((tm, D_MODEL), lambda i: (i, 0)),
        scratch_shapes=[pltpu.VMEM((HALO, D_MODEL), F32)],
        compiler_params=_params(("arbitrary",)),
        name="conv_mixer_ln",
    )(x, w_in.astype(BF16), conv_w, w_out.astype(BF16), _row(g), _row(b))


def _ffn_kernel(x_ref, wg_ref, wu_ref, wd_ref, g_ref, b_ref, o_ref, xb_ref, acc_ref):
    f = pl.program_id(1)

    @pl.when(f == 0)
    def _():
        xb_ref[...] = x_ref[...].astype(BF16)
        acc_ref[...] = jnp.zeros_like(acc_ref)

    xb = xb_ref[...]
    gate = _dot(xb, wg_ref[...])
    up = _dot(xb, wu_ref[...])
    h = gate * jax.nn.sigmoid(gate) * up
    acc_ref[...] += _dot(h.astype(BF16), wd_ref[...])

    @pl.when(f == pl.num_programs(1) - 1)
    def _():
        o_ref[...] = _layer_norm(DN_ALPHA * x_ref[...] + acc_ref[...], g_ref[...], b_ref[...])


def _ffn_layer(x, w_gate, w_up, w_down, g, b, *, tm=512, tf=1408):
    n = x.shape[0]
    d_ff = w_gate.shape[1]
    vec = pl.BlockSpec((1, D_MODEL), lambda i, f: (0, 0))
    return pl.pallas_call(
        _ffn_kernel,
        out_shape=jax.ShapeDtypeStruct((n, D_MODEL), F32),
        grid=(n // tm, d_ff // tf),
        in_specs=[pl.BlockSpec((tm, D_MODEL), lambda i, f: (i, 0)),
                  pl.BlockSpec((D_MODEL, tf), lambda i, f: (0, f)),
                  pl.BlockSpec((D_MODEL, tf), lambda i, f: (0, f)),
                  pl.BlockSpec((tf, D_MODEL), lambda i, f: (f, 0)),
                  vec, vec],
        out_specs=pl.BlockSpec((tm, D_MODEL), lambda i, f: (i, 0)),
        scratch_shapes=[pltpu.VMEM((tm, D_MODEL), BF16), pltpu.VMEM((tm, D_MODEL), F32)],
        compiler_params=_params(("arbitrary", "arbitrary")),
        name="swiglu_ln",
    )(x, w_gate.astype(BF16), w_up.astype(BF16), w_down.astype(BF16), _row(g), _row(b))


CHUNKS = D_MODEL // LANES


def _store_token_tiles(ref, y):
    for c in range(CHUNKS):
        ref[pl.ds(c, y.shape[0], stride=CHUNKS), :] = y[:, c * LANES:(c + 1) * LANES]


def _load_token_tiles(ref, rows):
    return jnp.concatenate([ref[pl.ds(c, rows, stride=CHUNKS), :] for c in range(CHUNKS)], axis=1)


def _router_kernel(x_ref, w_ref, meta_ref, cnt_ref, run_ref):
    i = pl.program_id(0)

    @pl.when(i == 0)
    def _():
        run_ref[...] = jnp.zeros_like(run_ref)

    x = x_ref[...]
    w = w_ref[...]
    xh = x.astype(BF16)
    xl = (x - xh.astype(F32)).astype(BF16)
    wh = w.astype(BF16)
    wl = (w - wh.astype(F32)).astype(BF16)
    logits = _dot(xh, wh) + (_dot(xl, wh) + _dot(xh, wl))
    tm = logits.shape[0]
    lane = lax.broadcasted_iota(jnp.int32, logits.shape, 1)
    logits = jnp.where(lane < N_EXPERTS, logits, -jnp.inf)
    v1 = jnp.max(logits, axis=1, keepdims=True)
    i1 = jnp.min(jnp.where(logits == v1, lane, LANES), axis=1, keepdims=True)
    rest = jnp.where(lane == i1, -jnp.inf, logits)
    v2 = jnp.max(rest, axis=1, keepdims=True)
    i2 = jnp.min(jnp.where(rest == v2, lane, LANES), axis=1, keepdims=True)
    e2 = jnp.exp(v2 - v1)
    g1 = 1.0 / (1.0 + e2)
    g2 = e2 / (1.0 + e2)

    sel = jnp.where(lane == i1, 1.0, jnp.where(lane == i2, 1.0, 0.0))
    before = (lax.broadcasted_iota(jnp.int32, (tm, tm), 1) < lax.broadcasted_iota(jnp.int32, (tm, tm), 0))
    rank = run_ref[0:1, :] + _dot(before.astype(BF16), sel.astype(BF16))
    r1 = jnp.sum(jnp.where(lane == i1, rank, 0.0), axis=1, keepdims=True)
    r2 = jnp.sum(jnp.where(lane == i2, rank, 0.0), axis=1, keepdims=True)
    run_ref[...] = run_ref[...] + jnp.sum(sel, axis=0, keepdims=True)
    cnt_ref[...] = run_ref[...]
    meta = jnp.zeros_like(logits)
    for k, val in enumerate((i1.astype(F32), i2.astype(F32), r1, r2, g1, g2)):
        meta = jnp.where(lane == k, val, meta)
    meta_ref[...] = meta


def _router(x, w_router, *, tm=512):
    n = x.shape[0]
    w = jnp.pad(w_router, ((0, 0), (0, LANES - N_EXPERTS)))
    return pl.pallas_call(
        _router_kernel,
        out_shape=(jax.ShapeDtypeStruct((n, LANES), F32), jax.ShapeDtypeStruct((8, LANES), F32)),
        grid=(n // tm,),
        in_specs=[pl.BlockSpec((tm, D_MODEL), lambda i: (i, 0)),
                  pl.BlockSpec((D_MODEL, LANES), lambda i: (0, 0))],
        out_specs=(pl.BlockSpec((tm, LANES), lambda i: (i, 0)), pl.BlockSpec((8, LANES), lambda i: (0, 0))),
        scratch_shapes=[pltpu.VMEM((8, LANES), F32)],
        compiler_params=_params(("arbitrary",)),
        name="router_top2",
    )(x, w)


def _routing_tables(meta, counts, tm):
    n = meta.shape[0]
    cnt = counts[0, :N_EXPERTS].astype(jnp.int32)
    padded = (cnt + MOE_TILE - 1) // MOE_TILE * MOE_TILE
    ends = jnp.cumsum(padded)
    offs = ends - padded
    sel = meta[:, :4].astype(jnp.int32)
    pos = offs[sel[:, :2]] + sel[:, 2:]
    pos = pos.reshape(n // tm, tm, 2).transpose(0, 2, 1).reshape(n // tm, 2 * tm)
    max_tiles = 2 * n // MOE_TILE + N_EXPERTS
    first_row = jnp.arange(max_tiles, dtype=jnp.int32) * MOE_TILE
    tile_expert = jnp.minimum(jnp.sum(first_row[:, None] >= ends[None, :], axis=1), N_EXPERTS - 1).astype(jnp.int32)
    return pos, offs, ends, tile_expert, ends[-1:] // MOE_TILE, max_tiles


ZERO_ROWS = 128
DMA_GROUP = 16


def _token_rows(ref, start, rows):
    return ref.at[pl.ds(pl.multiple_of(start * CHUNKS, CHUNKS), rows * CHUNKS)]


def _dispatch_kernel(offs_ref, ends_ref, idx_hbm, x_ref, xs_hbm, idx_smem, xt_ref, zero_ref, sem_idx, sem_rows, *, tm):
    i = pl.program_id(0)
    idx_copy = pltpu.make_async_copy(idx_hbm.at[i], idx_smem, sem_idx)
    idx_copy.start()

    @pl.when(i == 0)
    def _():
        zero_ref[...] = jnp.zeros_like(zero_ref)

        def clear_tile(start):
            for k in range(MOE_TILE // ZERO_ROWS):
                clear = pltpu.make_async_copy(zero_ref, _token_rows(xs_hbm, start + k * ZERO_ROWS, ZERO_ROWS), sem_rows)
                clear.start()
                clear.wait()

        for e in range(N_EXPERTS):
            @pl.when(ends_ref[e] > offs_ref[e])
            def _():
                clear_tile(ends_ref[e] - MOE_TILE)

        def clear_unused(j, carry):
            clear_tile(j * MOE_TILE)
            return carry
        lax.fori_loop(ends_ref[N_EXPERTS - 1] // MOE_TILE, xs_hbm.shape[0] // (MOE_TILE * CHUNKS), clear_unused, 0)

    _store_token_tiles(xt_ref, x_ref[...])
    idx_copy.wait()

    def send(grp, carry):
        ts = [grp * DMA_GROUP + j for j in range(DMA_GROUP)]
        dst = [(idx_smem[t], idx_smem[tm + t]) for t in ts]
        for t, (p1, p2) in zip(ts, dst):
            src = _token_rows(xt_ref, t, 1)
            pltpu.make_async_copy(src, _token_rows(xs_hbm, p1, 1), sem_rows).start(priority=0)
            pltpu.make_async_copy(src, _token_rows(xs_hbm, p2, 1), sem_rows).start(priority=1)
        return carry
    lax.fori_loop(0, tm // DMA_GROUP, send, 0)
    for _ in range(2):
        pltpu.make_async_copy(xt_ref, _token_rows(xs_hbm, 0, tm), sem_rows).wait()


def _dispatch(x, idx, offs, ends, max_tiles, *, tm):
    n = x.shape[0]
    kern = functools.partial(_dispatch_kernel, tm=tm)
    return pl.pallas_call(
        kern,
        out_shape=jax.ShapeDtypeStruct((max_tiles * MOE_TILE * CHUNKS, LANES), F32),
        grid_spec=pltpu.PrefetchScalarGridSpec(
            num_scalar_prefetch=2,
            grid=(n // tm,),
            in_specs=[pl.BlockSpec(memory_space=pl.ANY),
                      pl.BlockSpec((tm, D_MODEL), lambda i, offs, ends: (i, 0))],
            out_specs=pl.BlockSpec(memory_space=pl.ANY),
            scratch_shapes=[pltpu.SMEM((2 * tm,), jnp.int32), pltpu.VMEM((tm * CHUNKS, LANES), F32),
                            pltpu.VMEM((ZERO_ROWS * CHUNKS, LANES), F32),
                            pltpu.SemaphoreType.DMA, pltpu.SemaphoreType.DMA]),
        compiler_params=_params(("arbitrary",)),
        name="moe_dispatch",
    )(offs, ends, idx, x)


def _expert_kernel(te_ref, nu_ref, x_ref, wg_ref, wu_ref, wd_ref, o_ref, xb_ref, acc_ref):
    i = pl.program_id(0)
    f = pl.program_id(1)
    last_f = pl.num_programs(1) - 1

    @pl.when(i < nu_ref[0])
    def _():
        @pl.when(f == 0)
        def _():
            xb_ref[...] = _load_token_tiles(x_ref, MOE_TILE).astype(BF16)
            acc_ref[...] = jnp.zeros_like(acc_ref)

        xb = xb_ref[...]
        gate = _dot(xb, wg_ref[...].astype(BF16))
        up = _dot(xb, wu_ref[...].astype(BF16))
        h = gate * jax.nn.sigmoid(gate) * up
        acc_ref[...] += _dot(h.astype(BF16), wd_ref[...].astype(BF16))

        @pl.when(f == last_f)
        def _():
            _store_token_tiles(o_ref, acc_ref[...])

    @pl.when((i >= nu_ref[0]) & (f == 0))
    def _():
        o_ref[...] = jnp.zeros_like(o_ref)


def _experts(xs, tile_expert, n_used, w_gate, w_up, w_down, max_tiles, *, tf=512):
    d_ff = w_gate.shape[2]
    n_f = d_ff // tf

    def used_tile(i, nu):
        return jnp.minimum(i, jnp.maximum(nu[0] - 1, 0))

    def row_map(i, f, te, nu):
        return used_tile(i, nu), 0

    def out_map(i, f, te, nu):
        return i, 0

    def up_map(i, f, te, nu):
        return te[used_tile(i, nu)], 0, jnp.where(i < nu[0], f, n_f - 1)

    def down_map(i, f, te, nu):
        return te[used_tile(i, nu)], jnp.where(i < nu[0], f, n_f - 1), 0

    return pl.pallas_call(
        _expert_kernel,
        out_shape=jax.ShapeDtypeStruct((max_tiles * MOE_TILE * CHUNKS, LANES), F32),
        grid_spec=pltpu.PrefetchScalarGridSpec(
            num_scalar_prefetch=2,
            grid=(max_tiles, n_f),
            in_specs=[pl.BlockSpec((MOE_TILE * CHUNKS, LANES), row_map),
                      pl.BlockSpec((None, D_MODEL, tf), up_map),
                      pl.BlockSpec((None, D_MODEL, tf), up_map),
                      pl.BlockSpec((None, tf, D_MODEL), down_map)],
            out_specs=pl.BlockSpec((MOE_TILE * CHUNKS, LANES), out_map),
            scratch_shapes=[pltpu.VMEM((MOE_TILE, D_MODEL), BF16), pltpu.VMEM((MOE_TILE, D_MODEL), F32)]),
        compiler_params=_params(("arbitrary", "arbitrary")),
        name="moe_experts",
    )(tile_expert, n_used, xs, w_gate, w_up, w_down)


def _combine_kernel(idx_hbm, x_ref, meta_ref, y_hbm, g_ref, b_ref, o_ref, idx_smem, buf_ref, sem_idx, sem_rows, *, tm):
    i = pl.program_id(0)
    idx_copy = pltpu.make_async_copy(idx_hbm.at[i], idx_smem, sem_idx)
    idx_copy.start()
    idx_copy.wait()

    def fetch(grp, carry):
        ts = [grp * DMA_GROUP + j for j in range(DMA_GROUP)]
        src = [(idx_smem[t], idx_smem[tm + t]) for t in ts]
        for t, (p1, p2) in zip(ts, src):
            pltpu.make_async_copy(_token_rows(y_hbm, p1, 1), _token_rows(buf_ref.at[0], t, 1), sem_rows).start(priority=0)
            pltpu.make_async_copy(_token_rows(y_hbm, p2, 1), _token_rows(buf_ref.at[1], t, 1), sem_rows).start(priority=1)
        return carry
    lax.fori_loop(0, tm // DMA_GROUP, fetch, 0)
    for k in range(2):
        pltpu.make_async_copy(_token_rows(y_hbm, 0, tm), buf_ref.at[k], sem_rows).wait()

    meta = meta_ref[...]
    lane = lax.broadcasted_iota(jnp.int32, meta.shape, 1)
    g1 = jnp.sum(jnp.where(lane == META_G1, meta, 0.0), axis=1, keepdims=True)
    g2 = jnp.sum(jnp.where(lane == META_G2, meta, 0.0), axis=1, keepdims=True)
    mix = g1 * _load_token_tiles(buf_ref.at[0], tm) + g2 * _load_token_tiles(buf_ref.at[1], tm)
    o_ref[...] = _layer_norm(DN_ALPHA * x_ref[...] + mix, g_ref[...], b_ref[...])


def _combine(x, meta, y, idx, g, b, *, tm):
    n = x.shape[0]
    kern = functools.partial(_combine_kernel, tm=tm)
    vec = pl.BlockSpec((1, D_MODEL), lambda i: (0, 0))
    return pl.pallas_call(
        kern,
        out_shape=jax.ShapeDtypeStruct((n, D_MODEL), F32),
        grid=(n // tm,),
        in_specs=[pl.BlockSpec(memory_space=pl.ANY),
                  pl.BlockSpec((tm, D_MODEL), lambda i: (i, 0)),
                  pl.BlockSpec((tm, LANES), lambda i: (i, 0)),
                  pl.BlockSpec(memory_space=pl.ANY),
                  vec, vec],
        out_specs=pl.BlockSpec((tm, D_MODEL), lambda i: (i, 0)),
        scratch_shapes=[pltpu.SMEM((2 * tm,), jnp.int32), pltpu.VMEM((2, tm * CHUNKS, LANES), F32),
                        pltpu.SemaphoreType.DMA, pltpu.SemaphoreType.DMA],
        compiler_params=_params(("arbitrary",)),
        name="moe_combine_ln",
    )(idx, x, meta, y, _row(g), _row(b))


def _moe_layer(x, w_router, w_gate, w_up, w_down, g, b, *, tm=512):
    meta, counts = _router(x, w_router)
    idx, offs, ends, tile_expert, n_used, max_tiles = _routing_tables(meta, counts, tm)
    xs = _dispatch(x, idx, offs, ends, max_tiles, tm=tm)
    y = _experts(xs, tile_expert, n_used, w_gate, w_up, w_down, max_tiles)
    return _combine(x, meta, y, idx, g, b, tm=tm)


def _rope_kernel(pos_ref, invf_ref, c_ref, s1_ref, s2_ref):
    ang = pos_ref[...].astype(F32) * invf_ref[...]
    c = jnp.cos(ang)
    s = jnp.sin(ang)
    dd = lax.broadcasted_iota(jnp.int32, ang.shape, 1) % HEAD_DIM
    c_ref[...] = c
    s1_ref[...] = jnp.where(dd < ROT_DIM // 2, -s, 0.0)
    s2_ref[...] = jnp.where((dd >= ROT_DIM // 2) & (dd < ROT_DIM), s, 0.0)


def _rope_tables(positions, *, tm=2048):
    n = positions.size
    half = ROT_DIM // 2
    inv_freq = ROPE_THETA ** (-(jnp.arange(0, ROT_DIM, 2, dtype=F32) / ROT_DIM))
    per_head = jnp.concatenate([inv_freq, inv_freq, jnp.zeros((HEAD_DIM - 2 * half,), F32)])
    invf = jnp.tile(per_head, LANES // HEAD_DIM).reshape(1, LANES)
    out = jax.ShapeDtypeStruct((n, LANES), F32)
    spec = pl.BlockSpec((tm, LANES), lambda i: (i, 0))
    return pl.pallas_call(
        _rope_kernel,
        out_shape=(out, out, out),
        grid=(n // tm,),
        in_specs=[pl.BlockSpec((tm, 1), lambda i: (i, 0)), pl.BlockSpec((1, LANES), lambda i: (0, 0))],
        out_specs=(spec, spec, spec),
        compiler_params=_params(("arbitrary",)),
        name="rope_tables",
    )(positions.reshape(n, 1), invf)


QKV_TILE = 512


def _qkv_kernel(x_ref, c_ref, s1_ref, s2_ref, w_ref, o_ref, *, dil):
    tm = x_ref.shape[0]
    chunk = tm // dil
    xb = x_ref[...].astype(BF16)
    if dil == 1:
        c, s1, s2 = c_ref[...], s1_ref[...], s2_ref[...]
    else:
        dst = lax.broadcasted_iota(jnp.int32, (tm, tm), 0)
        tok = lax.broadcasted_iota(jnp.int32, (tm, tm), 1)
        perm = (tok == (dst % chunk) * dil + dst // chunk).astype(BF16)
        xb = _dot(perm, xb).astype(BF16)
        c, s1, s2 = (jnp.concatenate([t[pl.ds(r, chunk, stride=dil), :] for r in range(dil)], axis=0)
                     for t in (c_ref, s1_ref, s2_ref))
    y = _dot(xb, w_ref[...])
    for part in range(2):
        scale = HEAD_DIM ** -0.5 * LOG2_E if part == 0 else 1.0
        for blk in range(D_MODEL // LANES):
            lo = part * D_MODEL + blk * LANES
            t = y[:, lo:lo + LANES]
            rot = t * c + pltpu.roll(t, LANES - ROT_DIM // 2, 1) * s1 + pltpu.roll(t, ROT_DIM // 2, 1) * s2
            o_ref[:, lo:lo + LANES] = (rot * scale).astype(BF16)
    o_ref[:, 2 * D_MODEL:] = y[:, 2 * D_MODEL:].astype(BF16)


def _qkv_group(x, tabs, w, dil):
    n = x.shape[0]
    tm = QKV_TILE
    kern = functools.partial(_qkv_kernel, dil=dil)
    tab_spec = pl.BlockSpec((tm, LANES), lambda i: (i, 0))
    return pl.pallas_call(
        kern,
        out_shape=jax.ShapeDtypeStruct((n, 3 * D_MODEL), BF16),
        grid=(n // tm,),
        in_specs=[pl.BlockSpec((tm, D_MODEL), lambda i: (i, 0)),
                  tab_spec, tab_spec, tab_spec,
                  pl.BlockSpec((D_MODEL, 3 * D_MODEL), lambda i: (0, 0))],
        out_specs=pl.BlockSpec((tm, 3 * D_MODEL), lambda i: (i, 0)),
        compiler_params=_params(("arbitrary",)),
        name="qkv_proj_dil%d" % dil,
    )(x, *tabs, w)


def _attn_kernel(*refs):
    qkv = refs[:9]
    o_ref = refs[9]
    acc_s, m_s, l_s = refs[10:13], refs[13:16], refs[16:19]

    lane = lax.broadcasted_iota(jnp.int32, (1, LANES), 1)
    head0 = lane < HEAD_DIM
    hm0 = head0.astype(BF16)
    hm1 = 1.0 - hm0
    qi = lax.broadcasted_iota(jnp.int32, (ATTN_BLOCK, ATTN_BLOCK), 0)
    kj = lax.broadcasted_iota(jnp.int32, (ATTN_BLOCK, ATTN_BLOCK), 1)
    cur_mask = jnp.where(qi <= kj, 0.0, NEG_BIG).astype(BF16)
    prev_mask = jnp.where(qi >= kj, 0.0, NEG_BIG).astype(BF16)
    mask_both = jnp.concatenate([prev_mask, cur_mask], axis=0)
    row_onehot = (qi == kj).astype(BF16)
    row_onehot = jnp.concatenate([row_onehot, row_onehot], axis=0)

    def block_rows(ref, dil, r, nb):
        chunk = QKV_TILE // dil
        if chunk >= ATTN_BLOCK:
            first = nb * ATTN_BLOCK
            base = first // chunk * QKV_TILE + r * chunk + first % chunk
            return ref[base:base + ATTN_BLOCK, :]
        pieces = ATTN_BLOCK // chunk
        starts = [(nb * pieces + m) * QKV_TILE + r * chunk for m in range(pieces)]
        return jnp.concatenate([ref[s:s + chunk, :] for s in starts], axis=0)

    def scores(grp, dil, r, nb):
        q_ref, k_ref, v_ref = qkv[3 * grp:3 * grp + 3]
        q = block_rows(q_ref, dil, r, nb)
        q2 = jnp.concatenate([q * hm0, q * hm1], axis=0)
        q2 = jnp.concatenate([q2, row_onehot], axis=1)
        kk = block_rows(k_ref, dil, r, nb)
        vv = block_rows(v_ref, dil, r, nb)
        if nb > 0:
            kk = jnp.concatenate([block_rows(k_ref, dil, r, nb - 1), kk], axis=0)
            vv = jnp.concatenate([block_rows(v_ref, dil, r, nb - 1), vv], axis=0)
            kk = jnp.concatenate([kk, mask_both], axis=1)
        else:
            kk = jnp.concatenate([kk, cur_mask], axis=1)
        s = lax.dot_general(q2, kk, (((1,), (1,)), ((), ())), preferred_element_type=F32)
        return s, vv

    def finish(grp, s, vv, nat_start, dil):
        m = jnp.max(s, axis=1, keepdims=True)
        pb = jnp.exp2(s - m).astype(BF16)
        ones = jnp.ones_like(vv)
        o0 = _dot(pb[:ATTN_BLOCK], jnp.concatenate([vv * hm0, ones], axis=1))
        o1 = _dot(pb[ATTN_BLOCK:], jnp.concatenate([vv * hm1, ones], axis=1))
        acc = o0[:, :LANES] + o1[:, :LANES]
        mb = jnp.where(head0, m[:ATTN_BLOCK], m[ATTN_BLOCK:])
        lb = jnp.where(head0, o0[:, LANES:], o1[:, LANES:])
        if dil == 1:
            rows = pl.ds(nat_start, ATTN_BLOCK)
        else:
            rows = pl.ds(nat_start, ATTN_BLOCK, stride=dil)
        acc_s[grp][rows, :] = acc
        m_s[grp][rows, :] = mb
        l_s[grp][rows, :] = lb

    for grp, dil in enumerate(ATTN_DILATIONS):
        n_blocks = SEQ // dil // ATTN_BLOCK
        blocks = [(r, 0) for r in range(dil)] + [(r, nb) for r in range(dil) for nb in range(1, n_blocks)]
        for lo in range(0, len(blocks), ATTN_UNROLL):
            batch = blocks[lo:lo + ATTN_UNROLL]
            staged = [scores(grp, dil, r, nb) for r, nb in batch]
            for (r, nb), (s, vv) in zip(batch, staged):
                finish(grp, s, vv, nb * ATTN_BLOCK * dil + r, dil)

    def merge(c, carry):
        rows = pl.ds(pl.multiple_of(c * ATTN_BLOCK, ATTN_BLOCK), ATTN_BLOCK)
        ms = [m_s[g][rows, :] for g in range(3)]
        top = jnp.maximum(jnp.maximum(ms[0], ms[1]), ms[2])
        num = jnp.zeros((ATTN_BLOCK, LANES), F32)
        den = jnp.zeros((ATTN_BLOCK, LANES), F32)
        for g in range(3):
            w = jnp.exp2(ms[g] - top)
            num = num + w * acc_s[g][rows, :]
            den = den + w * l_s[g][rows, :]
        o_ref[rows, :] = (num / den).astype(BF16)
        return carry
    lax.fori_loop(0, SEQ // ATTN_BLOCK, merge, 0)


def _attention(qkvs, n):
    n_pairs = D_MODEL // LANES
    in_specs, args = [], []
    for qkv in qkvs:
        for part in range(3):
            in_specs.append(pl.BlockSpec((SEQ, LANES), lambda b, hp, part=part: (b, part * n_pairs + hp)))
            args.append(qkv)
    scratch = [pltpu.VMEM((SEQ, LANES), F32) for _ in range(9)]
    return pl.pallas_call(
        _attn_kernel,
        out_shape=jax.ShapeDtypeStruct((n, D_MODEL), BF16),
        grid=(n // SEQ, n_pairs),
        in_specs=in_specs,
        out_specs=pl.BlockSpec((SEQ, LANES), lambda b, hp: (b, hp)),
        scratch_shapes=scratch,
        compiler_params=_params(("arbitrary", "arbitrary")),
        name="dilated_attention",
    )(*args)


def _proj_ln_kernel(x_ref, a_ref, w_ref, g_ref, b_ref, o_ref):
    h = _dot(a_ref[...], w_ref[...])
    o_ref[...] = _layer_norm(DN_ALPHA * x_ref[...] + h, g_ref[...], b_ref[...])


def _proj_ln(x, a, w, g, b, *, tm=512):
    n = x.shape[0]
    vec = pl.BlockSpec((1, D_MODEL), lambda i: (0, 0))
    return pl.pallas_call(
        _proj_ln_kernel,
        out_shape=jax.ShapeDtypeStruct((n, D_MODEL), F32),
        grid=(n // tm,),
        in_specs=[pl.BlockSpec((tm, D_MODEL), lambda i: (i, 0)),
                  pl.BlockSpec((tm, D_MODEL), lambda i: (i, 0)),
                  pl.BlockSpec((D_MODEL, D_MODEL), lambda i: (0, 0)),
                  vec, vec],
        out_specs=pl.BlockSpec((tm, D_MODEL), lambda i: (i, 0)),
        compiler_params=_params(("arbitrary",)),
        name="out_proj_ln",
    )(x, a, w.astype(BF16), _row(g), _row(b))


def _attn_layer(x, tabs, w_qkv, w_o, g, b):
    n = x.shape[0]
    w = w_qkv.astype(BF16)
    qkvs = [_qkv_group(x, tabs, w[:, grp * 3 * D_MODEL:(grp + 1) * 3 * D_MODEL], dil)
            for grp, dil in enumerate(ATTN_DILATIONS)]
    return _proj_ln(x, _attention(qkvs, n), w_o, g, b)


def kernel(x, positions, l0_pool_w_in, l0_pool_w_grp, l0_pool_scale, l0_ln1_g, l0_ln1_b, l0_ffn_w_gate, l0_ffn_w_up, l0_ffn_w_down, l0_ln2_g, l0_ln2_b, l1_attn_w_qkv, l1_attn_w_o, l1_ln1_g, l1_ln1_b, l1_moe_w_router, l1_moe_w_gate, l1_moe_w_up, l1_moe_w_down, l1_ln2_g, l1_ln2_b, l2_conv_w_in, l2_conv_w, l2_conv_w_out, l2_ln1_g, l2_ln1_b, l2_ffn_w_gate, l2_ffn_w_up, l2_ffn_w_down, l2_ln2_g, l2_ln2_b, l3_pool_w_in, l3_pool_w_grp, l3_pool_scale, l3_ln1_g, l3_ln1_b, l3_moe_w_router, l3_moe_w_gate, l3_moe_w_up, l3_moe_w_down, l3_ln2_g, l3_ln2_b):
    batch, seq, d = x.shape
    h = x.reshape(batch * seq, d)
    tabs = _rope_tables(positions)
    h = _pool_layer(h, l0_pool_w_in, l0_pool_w_grp, l0_pool_scale, l0_ln1_g, l0_ln1_b)
    h = _ffn_layer(h, l0_ffn_w_gate, l0_ffn_w_up, l0_ffn_w_down, l0_ln2_g, l0_ln2_b)
    h = _attn_layer(h, tabs, l1_attn_w_qkv, l1_attn_w_o, l1_ln1_g, l1_ln1_b)
    h = _moe_layer(h, l1_moe_w_router, l1_moe_w_gate, l1_moe_w_up, l1_moe_w_down, l1_ln2_g, l1_ln2_b)
    h = _conv_layer(h, l2_conv_w_in, l2_conv_w, l2_conv_w_out, l2_ln1_g, l2_ln1_b)
    h = _ffn_layer(h, l2_ffn_w_gate, l2_ffn_w_up, l2_ffn_w_down, l2_ln2_g, l2_ln2_b)
    h = _pool_layer(h, l3_pool_w_in, l3_pool_w_grp, l3_pool_scale, l3_ln1_g, l3_ln1_b)
    h = _moe_layer(h, l3_moe_w_router, l3_moe_w_gate, l3_moe_w_up, l3_moe_w_down, l3_ln2_g, l3_ln2_b)
    return h.reshape(batch, seq, d)
```

```python
import functools

import jax
import jax.numpy as jnp
from jax import lax
from jax.experimental import pallas as pl
from jax.experimental.pallas import tpu as pltpu

D_MODEL = 1024
SEQ = 2048
DEPTH = 4
POOL_WINDOWS = (2, 4, 8, 16)
POOL_GROUP_DIM = D_MODEL // len(POOL_WINDOWS)
ATTN_DILATIONS = (1, 4, 16)
ATTN_BLOCK = 128
ATTN_UNROLL = 16
HEAD_DIM = 64
ROT_DIM = HEAD_DIM // 4
ROPE_THETA = 500000.0
LOG2_E = 1.4426950408889634
CONV_WIDTH = 3
N_EXPERTS = 8
MOE_TILE = 1024
META_E1, META_E2, META_R1, META_R2, META_G1, META_G2 = range(6)
DN_ALPHA = (2 * DEPTH) ** 0.25
LN_EPS = 1e-5

LANES = 128
HALO = 16
NEG_BIG = -1e30
VMEM_LIMIT = 56 * 1024 * 1024

F32 = jnp.float32
BF16 = jnp.bfloat16


def _params(semantics, vmem=VMEM_LIMIT):
    return pltpu.CompilerParams(dimension_semantics=semantics, vmem_limit_bytes=vmem)


def _dot(a, b):
    return jnp.dot(a, b, preferred_element_type=F32)


def _layer_norm(z, g, b):
    mu = jnp.mean(z, axis=-1, keepdims=True)
    zc = z - mu
    var = jnp.mean(zc * zc, axis=-1, keepdims=True)
    return zc * lax.rsqrt(var + LN_EPS) * g + b


def _row(v):
    return v.reshape(1, -1)


def _pool_kernel(x_ref, w_in_ref, w_grp_ref, scale_ref, g_ref, b_ref, o_ref, halo_ref, *, tm, tiles_per_seq):
    i = pl.program_id(0)
    x = x_ref[...]
    u = _dot(x.astype(BF16), w_in_ref[...])

    @pl.when(i % tiles_per_seq == 0)
    def _():
        halo_ref[...] = jnp.zeros_like(halo_ref)

    buf = jnp.concatenate([halo_ref[...], u], axis=0)
    halo_ref[...] = u[tm - HALO:, :]
    t = (i % tiles_per_seq) * tm + lax.broadcasted_iota(jnp.int32, (tm, 1), 0)
    outs = []
    for grp, w in enumerate(POOL_WINDOWS):
        cols = slice(grp * POOL_GROUP_DIM, (grp + 1) * POOL_GROUP_DIM)
        s = buf[:, cols]
        k = 1
        while k < w:
            s = s + pltpu.roll(s, k, 0)
            k *= 2
        cnt = jnp.minimum(t + 1, w).astype(F32)
        pooled = s[HALO:, :] / cnt - u[:, cols]
        outs.append(_dot(pooled.astype(BF16), w_grp_ref[grp]))
    h = jnp.concatenate(outs, axis=1) * scale_ref[...]
    o_ref[...] = _layer_norm(DN_ALPHA * x + h, g_ref[...], b_ref[...])


def _pool_layer(x, w_in, w_grp, scale, g, b, *, tm=512):
    n = x.shape[0]
    kern = functools.partial(_pool_kernel, tm=tm, tiles_per_seq=SEQ // tm)
    vec = pl.BlockSpec((1, D_MODEL), lambda i: (0, 0))
    return pl.pallas_call(
        kern,
        out_shape=jax.ShapeDtypeStruct((n, D_MODEL), F32),
        grid=(n // tm,),
        in_specs=[pl.BlockSpec((tm, D_MODEL), lambda i: (i, 0)),
                  pl.BlockSpec((D_MODEL, D_MODEL), lambda i: (0, 0)),
                  pl.BlockSpec((len(POOL_WINDOWS), POOL_GROUP_DIM, POOL_GROUP_DIM), lambda i: (0, 0, 0)),
                  vec, vec, vec],
        out_specs=pl.BlockSpec((tm, D_MODEL), lambda i: (i, 0)),
        scratch_shapes=[pltpu.VMEM((HALO, D_MODEL), F32)],
        compiler_params=_params(("arbitrary",)),
        name="pool_mixer_ln",
    )(x, w_in.astype(BF16), w_grp.astype(BF16), _row(scale), _row(g), _row(b))


def _conv_kernel(x_ref, w_in_ref, cw_ref, w_out_ref, g_ref, b_ref, o_ref, halo_ref, *, tm, tiles_per_seq):
    i = pl.program_id(0)
    x = x_ref[...]
    proj = _dot(x.astype(BF16), w_in_ref[...])
    gate_b = proj[:, :D_MODEL]
    z = proj[:, D_MODEL:2 * D_MODEL] * proj[:, 2 * D_MODEL:]

    @pl.when(i % tiles_per_seq == 0)
    def _():
        halo_ref[...] = jnp.zeros_like(halo_ref)

    buf = jnp.concatenate([halo_ref[...], z], axis=0)
    halo_ref[...] = z[tm - HALO:, :]
    conv = cw_ref[0:1, :] * z
    for j in range(1, CONV_WIDTH):
        conv = conv + cw_ref[j:j + 1, :] * pltpu.roll(buf, j, 0)[HALO:, :]
    h = _dot((gate_b * conv).astype(BF16), w_out_ref[...])
    o_ref[...] = _layer_norm(DN_ALPHA * x + h, g_ref[...], b_ref[...])


def _conv_layer(x, w_in, conv_w, w_out, g, b, *, tm=512):
    n = x.shape[0]
    kern = functools.partial(_conv_kernel, tm=tm, tiles_per_seq=SEQ // tm)
    vec = pl.BlockSpec((1, D_MODEL), lambda i: (0, 0))
    return pl.pallas_call(
        kern,
        out_shape=jax.ShapeDtypeStruct((n, D_MODEL), F32),
        grid=(n // tm,),
        in_specs=[pl.BlockSpec((tm, D_MODEL), lambda i: (i, 0)),
                  pl.BlockSpec((D_MODEL, 3 * D_MODEL), lambda i: (0, 0)),
                  pl.BlockSpec((CONV_WIDTH, D_MODEL), lambda i: (0, 0)),
                  pl.BlockSpec((D_MODEL, D_MODEL), lambda i: (0, 0)),
                  vec, vec],
        out_specs=pl.BlockSpec((tm, D_MODEL), lambda i: (i, 0)),
        scratch_shapes=[pltpu.VMEM((HALO, D_MODEL), F32)],
        compiler_params=_params(("arbitrary",)),
        name="conv_mixer_ln",
    )(x, w_in.astype(BF16), conv_w, w_out.astype(BF16), _row(g), _row(b))


FFN_CHUNK = 256


def _ffn_kernel(x_ref, wg_ref, wu_ref, wd_ref, g_ref, b_ref, o_ref):
    x = x_ref[...]
    xb = x.astype(BF16)
    acc = jnp.zeros(x.shape, F32)
    for lo in range(0, wg_ref.shape[1], FFN_CHUNK):
        gate = _dot(xb, wg_ref[:, lo:lo + FFN_CHUNK])
        up = _dot(xb, wu_ref[:, lo:lo + FFN_CHUNK])
        h = gate * jax.nn.sigmoid(gate) * up
        acc = acc + _dot(h.astype(BF16), wd_ref[lo:lo + FFN_CHUNK, :])
    o_ref[...] = _layer_norm(DN_ALPHA * x + acc, g_ref[...], b_ref[...])


def _ffn_layer(x, w_gate, w_up, w_down, g, b, *, tm=1024):
    n = x.shape[0]
    d_ff = w_gate.shape[1]
    vec = pl.BlockSpec((1, D_MODEL), lambda i: (0, 0))
    once = pl.Buffered(1)
    return pl.pallas_call(
        _ffn_kernel,
        out_shape=jax.ShapeDtypeStruct((n, D_MODEL), F32),
        grid=(n // tm,),
        in_specs=[pl.BlockSpec((tm, D_MODEL), lambda i: (i, 0)),
                  pl.BlockSpec((D_MODEL, d_ff), lambda i: (0, 0), pipeline_mode=once),
                  pl.BlockSpec((D_MODEL, d_ff), lambda i: (0, 0), pipeline_mode=once),
                  pl.BlockSpec((d_ff, D_MODEL), lambda i: (0, 0), pipeline_mode=once),
                  vec, vec],
        out_specs=pl.BlockSpec((tm, D_MODEL), lambda i: (i, 0)),
        compiler_params=_params(("arbitrary",)),
        name="swiglu_ln",
    )(x, w_gate.astype(BF16), w_up.astype(BF16), w_down.astype(BF16), _row(g), _row(b))


CHUNKS = D_MODEL // LANES


def _store_token_tiles(ref, y):
    for c in range(CHUNKS):
        ref[pl.ds(c, y.shape[0], stride=CHUNKS), :] = y[:, c * LANES:(c + 1) * LANES]


def _load_token_tiles(ref, rows):
    return jnp.concatenate([ref[pl.ds(c, rows, stride=CHUNKS), :] for c in range(CHUNKS)], axis=1)


def _router_kernel(x_ref, w_ref, meta_ref, cnt_ref, run_ref):
    i = pl.program_id(0)

    @pl.when(i == 0)
    def _():
        run_ref[...] = jnp.zeros_like(run_ref)

    x = x_ref[...]
    w = w_ref[...]
    xh = x.astype(BF16)
    xl = (x - xh.astype(F32)).astype(BF16)
    wh = w.astype(BF16)
    wl = (w - wh.astype(F32)).astype(BF16)
    logits = _dot(xh, wh) + (_dot(xl, wh) + _dot(xh, wl))
    tm = logits.shape[0]
    lane = lax.broadcasted_iota(jnp.int32, logits.shape, 1)
    logits = jnp.where(lane < N_EXPERTS, logits, -jnp.inf)
    v1 = jnp.max(logits, axis=1, keepdims=True)
    i1 = jnp.min(jnp.where(logits == v1, lane, LANES), axis=1, keepdims=True)
    rest = jnp.where(lane == i1, -jnp.inf, logits)
    v2 = jnp.max(rest, axis=1, keepdims=True)
    i2 = jnp.min(jnp.where(rest == v2, lane, LANES), axis=1, keepdims=True)
    e2 = jnp.exp(v2 - v1)
    g1 = 1.0 / (1.0 + e2)
    g2 = e2 / (1.0 + e2)

    sel = jnp.where(lane == i1, 1.0, jnp.where(lane == i2, 1.0, 0.0))
    before = (lax.broadcasted_iota(jnp.int32, (tm, tm), 1) < lax.broadcasted_iota(jnp.int32, (tm, tm), 0))
    rank = run_ref[0:1, :] + _dot(before.astype(BF16), sel.astype(BF16))
    r1 = jnp.sum(jnp.where(lane == i1, rank, 0.0), axis=1, keepdims=True)
    r2 = jnp.sum(jnp.where(lane == i2, rank, 0.0), axis=1, keepdims=True)
    run_ref[...] = run_ref[...] + jnp.sum(sel, axis=0, keepdims=True)
    cnt_ref[...] = run_ref[...]
    meta = jnp.zeros_like(logits)
    for k, val in enumerate((i1.astype(F32), i2.astype(F32), r1, r2, g1, g2)):
        meta = jnp.where(lane == k, val, meta)
    meta_ref[...] = meta


def _router(x, w_router, *, tm=512):
    n = x.shape[0]
    w = jnp.pad(w_router, ((0, 0), (0, LANES - N_EXPERTS)))
    return pl.pallas_call(
        _router_kernel,
        out_shape=(jax.ShapeDtypeStruct((n, LANES), F32), jax.ShapeDtypeStruct((8, LANES), F32)),
        grid=(n // tm,),
        in_specs=[pl.BlockSpec((tm, D_MODEL), lambda i: (i, 0)),
                  pl.BlockSpec((D_MODEL, LANES), lambda i: (0, 0))],
        out_specs=(pl.BlockSpec((tm, LANES), lambda i: (i, 0)), pl.BlockSpec((8, LANES), lambda i: (0, 0))),
        scratch_shapes=[pltpu.VMEM((8, LANES), F32)],
        compiler_params=_params(("arbitrary",)),
        name="router_top2",
    )(x, w)


def _routing_tables(meta, counts, tm):
    n = meta.shape[0]
    cnt = counts[0, :N_EXPERTS].astype(jnp.int32)
    padded = (cnt + MOE_TILE - 1) // MOE_TILE * MOE_TILE
    ends = jnp.cumsum(padded)
    offs = ends - padded
    sel = meta[:, :4].astype(jnp.int32)
    pos = offs[sel[:, :2]] + sel[:, 2:]
    pos = pos.reshape(n // tm, tm, 2).transpose(0, 2, 1).reshape(2 * n)
    max_tiles = 2 * n // MOE_TILE + N_EXPERTS
    first_row = jnp.arange(max_tiles, dtype=jnp.int32) * MOE_TILE
    tile_expert = jnp.minimum(jnp.sum(first_row[:, None] >= ends[None, :], axis=1), N_EXPERTS - 1).astype(jnp.int32)
    return pos, offs, ends, tile_expert, ends[-1:] // MOE_TILE, max_tiles


ZERO_ROWS = 128
DMA_GROUP = 16


def _token_rows(ref, start, rows):
    return ref.at[pl.ds(pl.multiple_of(start * CHUNKS, CHUNKS), rows * CHUNKS)]


def _dispatch_kernel(offs_ref, ends_ref, pos_ref, x_ref, xs_hbm, xt_ref, zero_ref, sem_zero, sem_rows, *, tm):
    i = pl.program_id(0)
    last = pl.num_programs(0) - 1

    @pl.when(i == 0)
    def _():
        zero_ref[...] = jnp.zeros_like(zero_ref)

        def clear_tile(start):
            for k in range(MOE_TILE // ZERO_ROWS):
                clear = pltpu.make_async_copy(zero_ref, _token_rows(xs_hbm, start + k * ZERO_ROWS, ZERO_ROWS), sem_zero)
                clear.start()
                clear.wait()

        for e in range(N_EXPERTS):
            @pl.when(ends_ref[e] > offs_ref[e])
            def _():
                clear_tile(ends_ref[e] - MOE_TILE)

        def clear_unused(j, carry):
            clear_tile(j * MOE_TILE)
            return carry
        lax.fori_loop(ends_ref[N_EXPERTS - 1] // MOE_TILE, xs_hbm.shape[0] // (MOE_TILE * CHUNKS), clear_unused, 0)

    def wait_rows(slot):
        for _ in range(2):
            pltpu.make_async_copy(xt_ref.at[slot], _token_rows(xs_hbm, 0, tm), sem_rows.at[slot]).wait()

    def step(slot):
        xt = xt_ref.at[slot]
        _store_token_tiles(xt, x_ref[...])
        base = i * (2 * tm)

        def send(grp, carry):
            ts = [grp * DMA_GROUP + j for j in range(DMA_GROUP)]
            dst = [(pos_ref[base + t], pos_ref[base + tm + t]) for t in ts]
            for t, (p1, p2) in zip(ts, dst):
                src = _token_rows(xt, t, 1)
                pltpu.make_async_copy(src, _token_rows(xs_hbm, p1, 1), sem_rows.at[slot]).start(priority=0)
                pltpu.make_async_copy(src, _token_rows(xs_hbm, p2, 1), sem_rows.at[slot]).start(priority=1)
            return carry
        lax.fori_loop(0, tm // DMA_GROUP, send, 0)

        @pl.when(i > 0)
        def _():
            wait_rows(1 - slot)

        @pl.when(i == last)
        def _():
            wait_rows(slot)

    for slot in range(2):
        pl.when(i % 2 == slot)(functools.partial(step, slot))


def _dispatch(x, pos, offs, ends, max_tiles, *, tm):
    n = x.shape[0]
    kern = functools.partial(_dispatch_kernel, tm=tm)
    return pl.pallas_call(
        kern,
        out_shape=jax.ShapeDtypeStruct((max_tiles * MOE_TILE * CHUNKS, LANES), F32),
        grid_spec=pltpu.PrefetchScalarGridSpec(
            num_scalar_prefetch=3,
            grid=(n // tm,),
            in_specs=[pl.BlockSpec((tm, D_MODEL), lambda i, offs, ends, pos: (i, 0))],
            out_specs=pl.BlockSpec(memory_space=pl.ANY),
            scratch_shapes=[pltpu.VMEM((2, tm * CHUNKS, LANES), F32), pltpu.VMEM((ZERO_ROWS * CHUNKS, LANES), F32),
                            pltpu.SemaphoreType.DMA, pltpu.SemaphoreType.DMA((2,))]),
        compiler_params=_params(("arbitrary",)),
        name="moe_dispatch",
    )(offs, ends, pos, x)


def _expert_kernel(te_ref, nu_ref, x_ref, wg_ref, wu_ref, wd_ref, o_ref, xb_ref, acc_ref):
    i = pl.program_id(0)
    f = pl.program_id(1)
    last_f = pl.num_programs(1) - 1

    @pl.when(i < nu_ref[0])
    def _():
        @pl.when(f == 0)
        def _():
            xb_ref[...] = _load_token_tiles(x_ref, MOE_TILE).astype(BF16)
            acc_ref[...] = jnp.zeros_like(acc_ref)

        xb = xb_ref[...]
        gate = _dot(xb, wg_ref[...].astype(BF16))
        up = _dot(xb, wu_ref[...].astype(BF16))
        h = gate * jax.nn.sigmoid(gate) * up
        acc_ref[...] += _dot(h.astype(BF16), wd_ref[...].astype(BF16))

        @pl.when(f == last_f)
        def _():
            _store_token_tiles(o_ref, acc_ref[...])

    @pl.when((i >= nu_ref[0]) & (f == 0))
    def _():
        o_ref[...] = jnp.zeros_like(o_ref)


def _experts(xs, tile_expert, n_used, w_gate, w_up, w_down, max_tiles, *, tf=512):
    d_ff = w_gate.shape[2]
    n_f = d_ff // tf

    def used_tile(i, nu):
        return jnp.minimum(i, jnp.maximum(nu[0] - 1, 0))

    def row_map(i, f, te, nu):
        return used_tile(i, nu), 0

    def out_map(i, f, te, nu):
        return i, 0

    def up_map(i, f, te, nu):
        return te[used_tile(i, nu)], 0, jnp.where(i < nu[0], f, n_f - 1)

    def down_map(i, f, te, nu):
        return te[used_tile(i, nu)], jnp.where(i < nu[0], f, n_f - 1), 0

    return pl.pallas_call(
        _expert_kernel,
        out_shape=jax.ShapeDtypeStruct((max_tiles * MOE_TILE * CHUNKS, LANES), F32),
        grid_spec=pltpu.PrefetchScalarGridSpec(
            num_scalar_prefetch=2,
            grid=(max_tiles, n_f),
            in_specs=[pl.BlockSpec((MOE_TILE * CHUNKS, LANES), row_map),
                      pl.BlockSpec((None, D_MODEL, tf), up_map),
                      pl.BlockSpec((None, D_MODEL, tf), up_map),
                      pl.BlockSpec((None, tf, D_MODEL), down_map)],
            out_specs=pl.BlockSpec((MOE_TILE * CHUNKS, LANES), out_map),
            scratch_shapes=[pltpu.VMEM((MOE_TILE, D_MODEL), BF16), pltpu.VMEM((MOE_TILE, D_MODEL), F32)]),
        compiler_params=_params(("arbitrary", "arbitrary")),
        name="moe_experts",
    )(tile_expert, n_used, xs, w_gate, w_up, w_down)


def _combine_kernel(pos_ref, x_ref, meta_ref, y_hbm, g_ref, b_ref, o_ref, buf_ref, sem_rows, *, tm):
    i = pl.program_id(0)
    n_steps = pl.num_programs(0)

    def fetch_rows(tile, slot):
        base = tile * (2 * tm)

        def fetch(grp, carry):
            ts = [grp * DMA_GROUP + j for j in range(DMA_GROUP)]
            src = [(pos_ref[base + t], pos_ref[base + tm + t]) for t in ts]
            for t, (p1, p2) in zip(ts, src):
                pltpu.make_async_copy(_token_rows(y_hbm, p1, 1), _token_rows(buf_ref.at[slot, 0], t, 1),
                                      sem_rows.at[slot]).start(priority=0)
                pltpu.make_async_copy(_token_rows(y_hbm, p2, 1), _token_rows(buf_ref.at[slot, 1], t, 1),
                                      sem_rows.at[slot]).start(priority=1)
            return carry
        lax.fori_loop(0, tm // DMA_GROUP, fetch, 0)

    @pl.when(i == 0)
    def _():
        fetch_rows(0, 0)

    def step(slot):
        @pl.when(i + 1 < n_steps)
        def _():
            fetch_rows(i + 1, 1 - slot)

        for k in range(2):
            pltpu.make_async_copy(_token_rows(y_hbm, 0, tm), buf_ref.at[slot, k], sem_rows.at[slot]).wait()
        meta = meta_ref[...]
        lane = lax.broadcasted_iota(jnp.int32, meta.shape, 1)
        g1 = jnp.sum(jnp.where(lane == META_G1, meta, 0.0), axis=1, keepdims=True)
        g2 = jnp.sum(jnp.where(lane == META_G2, meta, 0.0), axis=1, keepdims=True)
        mix = g1 * _load_token_tiles(buf_ref.at[slot, 0], tm) + g2 * _load_token_tiles(buf_ref.at[slot, 1], tm)
        o_ref[...] = _layer_norm(DN_ALPHA * x_ref[...] + mix, g_ref[...], b_ref[...])

    for slot in range(2):
        pl.when(i % 2 == slot)(functools.partial(step, slot))


def _combine(x, meta, y, pos, g, b, *, tm):
    n = x.shape[0]
    kern = functools.partial(_combine_kernel, tm=tm)
    vec = pl.BlockSpec((1, D_MODEL), lambda i, pos: (0, 0))
    return pl.pallas_call(
        kern,
        out_shape=jax.ShapeDtypeStruct((n, D_MODEL), F32),
        grid_spec=pltpu.PrefetchScalarGridSpec(
            num_scalar_prefetch=1,
            grid=(n // tm,),
            in_specs=[pl.BlockSpec((tm, D_MODEL), lambda i, pos: (i, 0)),
                      pl.BlockSpec((tm, LANES), lambda i, pos: (i, 0)),
                      pl.BlockSpec(memory_space=pl.ANY),
                      vec, vec],
            out_specs=pl.BlockSpec((tm, D_MODEL), lambda i, pos: (i, 0)),
            scratch_shapes=[pltpu.VMEM((2, 2, tm * CHUNKS, LANES), F32), pltpu.SemaphoreType.DMA((2,))]),
        compiler_params=_params(("arbitrary",)),
        name="moe_combine_ln",
    )(pos, x, meta, y, _row(g), _row(b))


def _moe_layer(x, w_router, w_gate, w_up, w_down, g, b, *, tm=512):
    meta, counts = _router(x, w_router)
    pos, offs, ends, tile_expert, n_used, max_tiles = _routing_tables(meta, counts, tm)
    xs = _dispatch(x, pos, offs, ends, max_tiles, tm=tm)
    y = _experts(xs, tile_expert, n_used, w_gate, w_up, w_down, max_tiles)
    return _combine(x, meta, y, pos, g, b, tm=tm)


def _rope_kernel(pos_ref, invf_ref, c_ref, s1_ref, s2_ref):
    ang = pos_ref[...].astype(F32) * invf_ref[...]
    c = jnp.cos(ang)
    s = jnp.sin(ang)
    dd = lax.broadcasted_iota(jnp.int32, ang.shape, 1) % HEAD_DIM
    c_ref[...] = c
    s1_ref[...] = jnp.where(dd < ROT_DIM // 2, -s, 0.0)
    s2_ref[...] = jnp.where((dd >= ROT_DIM // 2) & (dd < ROT_DIM), s, 0.0)


def _rope_tables(positions, *, tm=2048):
    n = positions.size
    half = ROT_DIM // 2
    inv_freq = ROPE_THETA ** (-(jnp.arange(0, ROT_DIM, 2, dtype=F32) / ROT_DIM))
    per_head = jnp.concatenate([inv_freq, inv_freq, jnp.zeros((HEAD_DIM - 2 * half,), F32)])
    invf = jnp.tile(per_head, LANES // HEAD_DIM).reshape(1, LANES)
    out = jax.ShapeDtypeStruct((n, LANES), F32)
    spec = pl.BlockSpec((tm, LANES), lambda i: (i, 0))
    return pl.pallas_call(
        _rope_kernel,
        out_shape=(out, out, out),
        grid=(n // tm,),
        in_specs=[pl.BlockSpec((tm, 1), lambda i: (i, 0)), pl.BlockSpec((1, LANES), lambda i: (0, 0))],
        out_specs=(spec, spec, spec),
        compiler_params=_params(("arbitrary",)),
        name="rope_tables",
    )(positions.reshape(n, 1), invf)


QKV_TILE = 512


def _qkv_kernel(x_ref, c_ref, s1_ref, s2_ref, w_ref, o_ref, *, dil):
    tm = x_ref.shape[0]
    chunk = tm // dil
    xb = x_ref[...].astype(BF16)
    if dil == 1:
        c, s1, s2 = c_ref[...], s1_ref[...], s2_ref[...]
    else:
        dst = lax.broadcasted_iota(jnp.int32, (tm, tm), 0)
        tok = lax.broadcasted_iota(jnp.int32, (tm, tm), 1)
        perm = (tok == (dst % chunk) * dil + dst // chunk).astype(BF16)
        xb = _dot(perm, xb).astype(BF16)
        c, s1, s2 = (jnp.concatenate([t[pl.ds(r, chunk, stride=dil), :] for r in range(dil)], axis=0)
                     for t in (c_ref, s1_ref, s2_ref))
    y = _dot(xb, w_ref[...])
    for part in range(2):
        scale = HEAD_DIM ** -0.5 * LOG2_E if part == 0 else 1.0
        for blk in range(D_MODEL // LANES):
            lo = part * D_MODEL + blk * LANES
            t = y[:, lo:lo + LANES]
            rot = t * c + pltpu.roll(t, LANES - ROT_DIM // 2, 1) * s1 + pltpu.roll(t, ROT_DIM // 2, 1) * s2
            o_ref[:, lo:lo + LANES] = (rot * scale).astype(BF16)
    o_ref[:, 2 * D_MODEL:] = y[:, 2 * D_MODEL:].astype(BF16)


def _qkv_group(x, tabs, w, dil):
    n = x.shape[0]
    tm = QKV_TILE
    kern = functools.partial(_qkv_kernel, dil=dil)
    tab_spec = pl.BlockSpec((tm, LANES), lambda i: (i, 0))
    return pl.pallas_call(
        kern,
        out_shape=jax.ShapeDtypeStruct((n, 3 * D_MODEL), BF16),
        grid=(n // tm,),
        in_specs=[pl.BlockSpec((tm, D_MODEL), lambda i: (i, 0)),
                  tab_spec, tab_spec, tab_spec,
                  pl.BlockSpec((D_MODEL, 3 * D_MODEL), lambda i: (0, 0))],
        out_specs=pl.BlockSpec((tm, 3 * D_MODEL), lambda i: (i, 0)),
        compiler_params=_params(("arbitrary",)),
        name="qkv_proj_dil%d" % dil,
    )(x, *tabs, w)


def _attn_kernel(*refs):
    qkv = refs[:9]
    o_ref = refs[9]
    acc_s, m_s, l_s = refs[10:13], refs[13:16], refs[16:19]

    lane = lax.broadcasted_iota(jnp.int32, (1, LANES), 1)
    head0 = lane < HEAD_DIM
    hm0 = head0.astype(BF16)
    hm1 = 1.0 - hm0
    qi = lax.broadcasted_iota(jnp.int32, (ATTN_BLOCK, ATTN_BLOCK), 0)
    kj = lax.broadcasted_iota(jnp.int32, (ATTN_BLOCK, ATTN_BLOCK), 1)
    cur_mask = jnp.where(qi <= kj, 0.0, NEG_BIG).astype(BF16)
    prev_mask = jnp.where(qi >= kj, 0.0, NEG_BIG).astype(BF16)
    mask_both = jnp.concatenate([prev_mask, cur_mask], axis=0)
    row_onehot = (qi == kj).astype(BF16)
    row_onehot = jnp.concatenate([row_onehot, row_onehot], axis=0)

    def block_rows(ref, dil, r, nb):
        chunk = QKV_TILE // dil
        if chunk >= ATTN_BLOCK:
            first = nb * ATTN_BLOCK
            base = first // chunk * QKV_TILE + r * chunk + first % chunk
            return ref[base:base + ATTN_BLOCK, :]
        pieces = ATTN_BLOCK // chunk
        starts = [(nb * pieces + m) * QKV_TILE + r * chunk for m in range(pieces)]
        return jnp.concatenate([ref[s:s + chunk, :] for s in starts], axis=0)

    def scores(grp, dil, r, nb):
        q_ref, k_ref, v_ref = qkv[3 * grp:3 * grp + 3]
        q = block_rows(q_ref, dil, r, nb)
        q2 = jnp.concatenate([q * hm0, q * hm1], axis=0)
        q2 = jnp.concatenate([q2, row_onehot], axis=1)
        kk = block_rows(k_ref, dil, r, nb)
        vv = block_rows(v_ref, dil, r, nb)
        if nb > 0:
            kk = jnp.concatenate([block_rows(k_ref, dil, r, nb - 1), kk], axis=0)
            vv = jnp.concatenate([block_rows(v_ref, dil, r, nb - 1), vv], axis=0)
            kk = jnp.concatenate([kk, mask_both], axis=1)
        else:
            kk = jnp.concatenate([kk, cur_mask], axis=1)
        s = lax.dot_general(q2, kk, (((1,), (1,)), ((), ())), preferred_element_type=F32)
        return s, vv

    def finish(grp, s, vv, nat_start, dil):
        m = jnp.max(s, axis=1, keepdims=True)
        pb = jnp.exp2(s - m).astype(BF16)
        ones = jnp.ones_like(vv)
        o0 = _dot(pb[:ATTN_BLOCK], jnp.concatenate([vv * hm0, ones], axis=1))
        o1 = _dot(pb[ATTN_BLOCK:], jnp.concatenate([vv * hm1, ones], axis=1))
        acc = o0[:, :LANES] + o1[:, :LANES]
        mb = jnp.where(head0, m[:ATTN_BLOCK], m[ATTN_BLOCK:])
        lb = jnp.where(head0, o0[:, LANES:], o1[:, LANES:])
        if dil == 1:
            rows = pl.ds(nat_start, ATTN_BLOCK)
        else:
            rows = pl.ds(nat_start, ATTN_BLOCK, stride=dil)
        acc_s[grp][rows, :] = acc
        m_s[grp][rows, :] = mb
        l_s[grp][rows, :] = lb

    for grp, dil in enumerate(ATTN_DILATIONS):
        n_blocks = SEQ // dil // ATTN_BLOCK
        blocks = [(r, 0) for r in range(dil)] + [(r, nb) for r in range(dil) for nb in range(1, n_blocks)]
        for lo in range(0, len(blocks), ATTN_UNROLL):
            batch = blocks[lo:lo + ATTN_UNROLL]
            staged = [scores(grp, dil, r, nb) for r, nb in batch]
            for (r, nb), (s, vv) in zip(batch, staged):
                finish(grp, s, vv, nb * ATTN_BLOCK * dil + r, dil)

    def merge(c, carry):
        rows = pl.ds(pl.multiple_of(c * ATTN_BLOCK, ATTN_BLOCK), ATTN_BLOCK)
        ms = [m_s[g][rows, :] for g in range(3)]
        top = jnp.maximum(jnp.maximum(ms[0], ms[1]), ms[2])
        num = jnp.zeros((ATTN_BLOCK, LANES), F32)
        den = jnp.zeros((ATTN_BLOCK, LANES), F32)
        for g in range(3):
            w = jnp.exp2(ms[g] - top)
            num = num + w * acc_s[g][rows, :]
            den = den + w * l_s[g][rows, :]
        o_ref[rows, :] = (num / den).astype(BF16)
        return carry
    lax.fori_loop(0, SEQ // ATTN_BLOCK, merge, 0)


def _attention(qkvs, n):
    n_pairs = D_MODEL // LANES
    in_specs, args = [], []
    for qkv in qkvs:
        for part in range(3):
            in_specs.append(pl.BlockSpec((SEQ, LANES), lambda b, hp, part=part: (b, part * n_pairs + hp)))
            args.append(qkv)
    scratch = [pltpu.VMEM((SEQ, LANES), F32) for _ in range(9)]
    return pl.pallas_call(
        _attn_kernel,
        out_shape=jax.ShapeDtypeStruct((n, D_MODEL), BF16),
        grid=(n // SEQ, n_pairs),
        in_specs=in_specs,
        out_specs=pl.BlockSpec((SEQ, LANES), lambda b, hp: (b, hp)),
        scratch_shapes=scratch,
        compiler_params=_params(("arbitrary", "arbitrary")),
        name="dilated_attention",
    )(*args)


def _proj_ln_kernel(x_ref, a_ref, w_ref, g_ref, b_ref, o_ref):
    h = _dot(a_ref[...], w_ref[...])
    o_ref[...] = _layer_norm(DN_ALPHA * x_ref[...] + h, g_ref[...], b_ref[...])


def _proj_ln(x, a, w, g, b, *, tm=512):
    n = x.shape[0]
    vec = pl.BlockSpec((1, D_MODEL), lambda i: (0, 0))
    return pl.pallas_call(
        _proj_ln_kernel,
        out_shape=jax.ShapeDtypeStruct((n, D_MODEL), F32),
        grid=(n // tm,),
        in_specs=[pl.BlockSpec((tm, D_MODEL), lambda i: (i, 0)),
                  pl.BlockSpec((tm, D_MODEL), lambda i: (i, 0)),
                  pl.BlockSpec((D_MODEL, D_MODEL), lambda i: (0, 0)),
                  vec, vec],
        out_specs=pl.BlockSpec((tm, D_MODEL), lambda i: (i, 0)),
        compiler_params=_params(("arbitrary",)),
        name="out_proj_ln",
    )(x, a, w.astype(BF16), _row(g), _row(b))


def _attn_layer(x, tabs, w_qkv, w_o, g, b):
    n = x.shape[0]
    w = w_qkv.astype(BF16)
    qkvs = [_qkv_group(x, tabs, w[:, grp * 3 * D_MODEL:(grp + 1) * 3 * D_MODEL], dil)
            for grp, dil in enumerate(ATTN_DILATIONS)]
    return _proj_ln(x, _attention(qkvs, n), w_o, g, b)


def kernel(x, positions, l0_pool_w_in, l0_pool_w_grp, l0_pool_scale, l0_ln1_g, l0_ln1_b, l0_ffn_w_gate, l0_ffn_w_up, l0_ffn_w_down, l0_ln2_g, l0_ln2_b, l1_attn_w_qkv, l1_attn_w_o, l1_ln1_g, l1_ln1_b, l1_moe_w_router, l1_moe_w_gate, l1_moe_w_up, l1_moe_w_down, l1_ln2_g, l1_ln2_b, l2_conv_w_in, l2_conv_w, l2_conv_w_out, l2_ln1_g, l2_ln1_b, l2_ffn_w_gate, l2_ffn_w_up, l2_ffn_w_down, l2_ln2_g, l2_ln2_b, l3_pool_w_in, l3_pool_w_grp, l3_pool_scale, l3_ln1_g, l3_ln1_b, l3_moe_w_router, l3_moe_w_gate, l3_moe_w_up, l3_moe_w_down, l3_ln2_g, l3_ln2_b):
    batch, seq, d = x.shape
    h = x.reshape(batch * seq, d)
    tabs = _rope_tables(positions)
    h = _pool_layer(h, l0_pool_w_in, l0_pool_w_grp, l0_pool_scale, l0_ln1_g, l0_ln1_b)
    h = _ffn_layer(h, l0_ffn_w_gate, l0_ffn_w_up, l0_ffn_w_down, l0_ln2_g, l0_ln2_b)
    h = _attn_layer(h, tabs, l1_attn_w_qkv, l1_attn_w_o, l1_ln1_g, l1_ln1_b)
    h = _moe_layer(h, l1_moe_w_router, l1_moe_w_gate, l1_moe_w_up, l1_moe_w_down, l1_ln2_g, l1_ln2_b)
    h = _conv_layer(h, l2_conv_w_in, l2_conv_w, l2_conv_w_out, l2_ln1_g, l2_ln1_b)
    h = _ffn_layer(h, l2_ffn_w_gate, l2_ffn_w_up, l2_ffn_w_down, l2_ln2_g, l2_ln2_b)
    h = _pool_layer(h, l3_pool_w_in, l3_pool_w_grp, l3_pool_scale, l3_ln1_g, l3_ln1_b)
    h = _moe_layer(h, l3_moe_w_router, l3_moe_w_gate, l3_moe_w_up, l3_moe_w_down, l3_ln2_g, l3_ln2_b)
    return h.reshape(batch, seq, d)
```

```python
import functools

import jax
import jax.numpy as jnp
from jax import lax
from jax.experimental import pallas as pl
from jax.experimental.pallas import tpu as pltpu

D_MODEL = 1024
SEQ = 2048
DEPTH = 4
POOL_WINDOWS = (2, 4, 8, 16)
POOL_GROUP_DIM = D_MODEL // len(POOL_WINDOWS)
ATTN_DILATIONS = (1, 4, 16)
ATTN_BLOCK = 128
ATTN_UNROLL = 16
HEAD_DIM = 64
ROT_DIM = HEAD_DIM // 4
ROPE_THETA = 500000.0
LOG2_E = 1.4426950408889634
CONV_WIDTH = 3
N_EXPERTS = 8
MOE_TILE = 1024
META_E1, META_E2, META_R1, META_R2, META_G1, META_G2 = range(6)
DN_ALPHA = (2 * DEPTH) ** 0.25
LN_EPS = 1e-5

LANES = 128
HALO = 16
NEG_BIG = -1e30
VMEM_LIMIT = 56 * 1024 * 1024

F32 = jnp.float32
BF16 = jnp.bfloat16


def _params(semantics, vmem=VMEM_LIMIT):
    return pltpu.CompilerParams(dimension_semantics=semantics, vmem_limit_bytes=vmem)


def _dot(a, b):
    return jnp.dot(a, b, preferred_element_type=F32)


def _layer_norm(z, g, b):
    mu = jnp.mean(z, axis=-1, keepdims=True)
    zc = z - mu
    var = jnp.mean(zc * zc, axis=-1, keepdims=True)
    return zc * lax.rsqrt(var + LN_EPS) * g + b


def _row(v):
    return v.reshape(1, -1)


def _resident(shape, index_map):
    return pl.BlockSpec(shape, index_map, pipeline_mode=pl.Buffered(1))


def _cast_once(dst_ref, src_ref):
    @pl.when(pl.program_id(0) == 0)
    def _():
        dst_ref[...] = src_ref[...].astype(BF16)


def _pool_kernel(x_ref, w_in_f32, w_grp_f32, scale_ref, g_ref, b_ref, o_ref, halo_ref, w_in_ref, w_grp_ref, *, tm, tiles_per_seq):
    i = pl.program_id(0)
    _cast_once(w_in_ref, w_in_f32)
    _cast_once(w_grp_ref, w_grp_f32)
    x = x_ref[...]
    u = _dot(x.astype(BF16), w_in_ref[...])

    @pl.when(i % tiles_per_seq == 0)
    def _():
        halo_ref[...] = jnp.zeros_like(halo_ref)

    buf = jnp.concatenate([halo_ref[...], u], axis=0)
    halo_ref[...] = u[tm - HALO:, :]
    t = (i % tiles_per_seq) * tm + lax.broadcasted_iota(jnp.int32, (tm, 1), 0)
    outs = []
    for grp, w in enumerate(POOL_WINDOWS):
        cols = slice(grp * POOL_GROUP_DIM, (grp + 1) * POOL_GROUP_DIM)
        s = buf[:, cols]
        k = 1
        while k < w:
            s = s + pltpu.roll(s, k, 0)
            k *= 2
        cnt = jnp.minimum(t + 1, w).astype(F32)
        pooled = s[HALO:, :] / cnt - u[:, cols]
        outs.append(_dot(pooled.astype(BF16), w_grp_ref[grp]))
    h = jnp.concatenate(outs, axis=1) * scale_ref[...]
    o_ref[...] = _layer_norm(DN_ALPHA * x + h, g_ref[...], b_ref[...])


def _pool_layer(x, w_in, w_grp, scale, g, b, *, tm=512):
    n = x.shape[0]
    kern = functools.partial(_pool_kernel, tm=tm, tiles_per_seq=SEQ // tm)
    vec = pl.BlockSpec((1, D_MODEL), lambda i: (0, 0))
    return pl.pallas_call(
        kern,
        out_shape=jax.ShapeDtypeStruct((n, D_MODEL), F32),
        grid=(n // tm,),
        in_specs=[pl.BlockSpec((tm, D_MODEL), lambda i: (i, 0)),
                  _resident((D_MODEL, D_MODEL), lambda i: (0, 0)),
                  _resident((len(POOL_WINDOWS), POOL_GROUP_DIM, POOL_GROUP_DIM), lambda i: (0, 0, 0)),
                  vec, vec, vec],
        out_specs=pl.BlockSpec((tm, D_MODEL), lambda i: (i, 0)),
        scratch_shapes=[pltpu.VMEM((HALO, D_MODEL), F32), pltpu.VMEM((D_MODEL, D_MODEL), BF16),
                        pltpu.VMEM((len(POOL_WINDOWS), POOL_GROUP_DIM, POOL_GROUP_DIM), BF16)],
        compiler_params=_params(("arbitrary",)),
        name="pool_mixer_ln",
    )(x, w_in, w_grp, _row(scale), _row(g), _row(b))


def _conv_kernel(x_ref, w_in_f32, cw_ref, w_out_f32, g_ref, b_ref, o_ref, halo_ref, w_in_ref, w_out_ref, *, tm, tiles_per_seq):
    i = pl.program_id(0)
    _cast_once(w_in_ref, w_in_f32)
    _cast_once(w_out_ref, w_out_f32)
    x = x_ref[...]
    proj = _dot(x.astype(BF16), w_in_ref[...])
    gate_b = proj[:, :D_MODEL]
    z = proj[:, D_MODEL:2 * D_MODEL] * proj[:, 2 * D_MODEL:]

    @pl.when(i % tiles_per_seq == 0)
    def _():
        halo_ref[...] = jnp.zeros_like(halo_ref)

    buf = jnp.concatenate([halo_ref[...], z], axis=0)
    halo_ref[...] = z[tm - HALO:, :]
    conv = cw_ref[0:1, :] * z
    for j in range(1, CONV_WIDTH):
        conv = conv + cw_ref[j:j + 1, :] * pltpu.roll(buf, j, 0)[HALO:, :]
    h = _dot((gate_b * conv).astype(BF16), w_out_ref[...])
    o_ref[...] = _layer_norm(DN_ALPHA * x + h, g_ref[...], b_ref[...])


def _conv_layer(x, w_in, conv_w, w_out, g, b, *, tm=512):
    n = x.shape[0]
    kern = functools.partial(_conv_kernel, tm=tm, tiles_per_seq=SEQ // tm)
    vec = pl.BlockSpec((1, D_MODEL), lambda i: (0, 0))
    return pl.pallas_call(
        kern,
        out_shape=jax.ShapeDtypeStruct((n, D_MODEL), F32),
        grid=(n // tm,),
        in_specs=[pl.BlockSpec((tm, D_MODEL), lambda i: (i, 0)),
                  _resident((D_MODEL, 3 * D_MODEL), lambda i: (0, 0)),
                  pl.BlockSpec((CONV_WIDTH, D_MODEL), lambda i: (0, 0)),
                  _resident((D_MODEL, D_MODEL), lambda i: (0, 0)),
                  vec, vec],
        out_specs=pl.BlockSpec((tm, D_MODEL), lambda i: (i, 0)),
        scratch_shapes=[pltpu.VMEM((HALO, D_MODEL), F32), pltpu.VMEM((D_MODEL, 3 * D_MODEL), BF16),
                        pltpu.VMEM((D_MODEL, D_MODEL), BF16)],
        compiler_params=_params(("arbitrary",)),
        name="conv_mixer_ln",
    )(x, w_in, conv_w, w_out, _row(g), _row(b))


FFN_CHUNK = 256


def _ffn_kernel(x_ref, wg_ref, wu_ref, wd_ref, g_ref, b_ref, o_ref):
    x = x_ref[...]
    xb = x.astype(BF16)
    acc = jnp.zeros(x.shape, F32)
    for lo in range(0, wg_ref.shape[1], FFN_CHUNK):
        gate = _dot(xb, wg_ref[:, lo:lo + FFN_CHUNK])
        up = _dot(xb, wu_ref[:, lo:lo + FFN_CHUNK])
        h = gate * jax.nn.sigmoid(gate) * up
        acc = acc + _dot(h.astype(BF16), wd_ref[lo:lo + FFN_CHUNK, :])
    o_ref[...] = _layer_norm(DN_ALPHA * x + acc, g_ref[...], b_ref[...])


def _ffn_layer(x, w_gate, w_up, w_down, g, b, *, tm=1024):
    n = x.shape[0]
    d_ff = w_gate.shape[1]
    vec = pl.BlockSpec((1, D_MODEL), lambda i: (0, 0))
    once = pl.Buffered(1)
    return pl.pallas_call(
        _ffn_kernel,
        out_shape=jax.ShapeDtypeStruct((n, D_MODEL), F32),
        grid=(n // tm,),
        in_specs=[pl.BlockSpec((tm, D_MODEL), lambda i: (i, 0)),
                  pl.BlockSpec((D_MODEL, d_ff), lambda i: (0, 0), pipeline_mode=once),
                  pl.BlockSpec((D_MODEL, d_ff), lambda i: (0, 0), pipeline_mode=once),
                  pl.BlockSpec((d_ff, D_MODEL), lambda i: (0, 0), pipeline_mode=once),
                  vec, vec],
        out_specs=pl.BlockSpec((tm, D_MODEL), lambda i: (i, 0)),
        compiler_params=_params(("arbitrary",)),
        name="swiglu_ln",
    )(x, w_gate.astype(BF16), w_up.astype(BF16), w_down.astype(BF16), _row(g), _row(b))


CHUNKS = D_MODEL // LANES


def _store_token_tiles(ref, y):
    for c in range(CHUNKS):
        ref[pl.ds(c, y.shape[0], stride=CHUNKS), :] = y[:, c * LANES:(c + 1) * LANES]


def _load_token_tiles(ref, rows):
    return jnp.concatenate([ref[pl.ds(c, rows, stride=CHUNKS), :] for c in range(CHUNKS)], axis=1)


def _router_kernel(x_ref, w_ref, meta_ref, cnt_ref, run_ref):
    i = pl.program_id(0)

    @pl.when(i == 0)
    def _():
        run_ref[...] = jnp.zeros_like(run_ref)

    x = x_ref[...]
    w = w_ref[...]
    xh = x.astype(BF16)
    xl = (x - xh.astype(F32)).astype(BF16)
    wh = w.astype(BF16)
    wl = (w - wh.astype(F32)).astype(BF16)
    logits = _dot(xh, wh) + (_dot(xl, wh) + _dot(xh, wl))
    tm = logits.shape[0]
    lane = lax.broadcasted_iota(jnp.int32, logits.shape, 1)
    logits = jnp.where(lane < N_EXPERTS, logits, -jnp.inf)
    v1 = jnp.max(logits, axis=1, keepdims=True)
    i1 = jnp.min(jnp.where(logits == v1, lane, LANES), axis=1, keepdims=True)
    rest = jnp.where(lane == i1, -jnp.inf, logits)
    v2 = jnp.max(rest, axis=1, keepdims=True)
    i2 = jnp.min(jnp.where(rest == v2, lane, LANES), axis=1, keepdims=True)
    e2 = jnp.exp(v2 - v1)
    g1 = 1.0 / (1.0 + e2)
    g2 = e2 / (1.0 + e2)

    sel = jnp.where(lane == i1, 1.0, jnp.where(lane == i2, 1.0, 0.0))
    before = (lax.broadcasted_iota(jnp.int32, (tm, tm), 1) < lax.broadcasted_iota(jnp.int32, (tm, tm), 0))
    rank = run_ref[0:1, :] + _dot(before.astype(BF16), sel.astype(BF16))
    r1 = jnp.sum(jnp.where(lane == i1, rank, 0.0), axis=1, keepdims=True)
    r2 = jnp.sum(jnp.where(lane == i2, rank, 0.0), axis=1, keepdims=True)
    run_ref[...] = run_ref[...] + jnp.sum(sel, axis=0, keepdims=True)
    cnt_ref[...] = run_ref[...]
    meta = jnp.zeros_like(logits)
    for k, val in enumerate((i1.astype(F32), i2.astype(F32), r1, r2, g1, g2)):
        meta = jnp.where(lane == k, val, meta)
    meta_ref[...] = meta


def _router(x, w_router, *, tm=512):
    n = x.shape[0]
    w = jnp.pad(w_router, ((0, 0), (0, LANES - N_EXPERTS)))
    return pl.pallas_call(
        _router_kernel,
        out_shape=(jax.ShapeDtypeStruct((n, LANES), F32), jax.ShapeDtypeStruct((8, LANES), F32)),
        grid=(n // tm,),
        in_specs=[pl.BlockSpec((tm, D_MODEL), lambda i: (i, 0)),
                  pl.BlockSpec((D_MODEL, LANES), lambda i: (0, 0))],
        out_specs=(pl.BlockSpec((tm, LANES), lambda i: (i, 0)), pl.BlockSpec((8, LANES), lambda i: (0, 0))),
        scratch_shapes=[pltpu.VMEM((8, LANES), F32)],
        compiler_params=_params(("arbitrary",)),
        name="router_top2",
    )(x, w)


def _routing_tables(meta, counts, tm):
    n = meta.shape[0]
    cnt = counts[0, :N_EXPERTS].astype(jnp.int32)
    padded = (cnt + MOE_TILE - 1) // MOE_TILE * MOE_TILE
    ends = jnp.cumsum(padded)
    offs = ends - padded
    sel = meta[:, :4].astype(jnp.int32)
    pos = offs[sel[:, :2]] + sel[:, 2:]
    pos = pos.reshape(n // tm, tm, 2).transpose(0, 2, 1).reshape(2 * n)
    max_tiles = 2 * n // MOE_TILE + N_EXPERTS
    first_row = jnp.arange(max_tiles, dtype=jnp.int32) * MOE_TILE
    tile_expert = jnp.minimum(jnp.sum(first_row[:, None] >= ends[None, :], axis=1), N_EXPERTS - 1).astype(jnp.int32)
    return pos, offs, ends, tile_expert, ends[-1:] // MOE_TILE, max_tiles


ZERO_ROWS = 128
DMA_GROUP = 16


def _token_rows(ref, start, rows):
    return ref.at[pl.ds(pl.multiple_of(start * CHUNKS, CHUNKS), rows * CHUNKS)]


def _dispatch_kernel(offs_ref, ends_ref, pos_ref, x_ref, xs_hbm, xt_ref, zero_ref, sem_zero, sem_rows, *, tm):
    i = pl.program_id(0)
    last = pl.num_programs(0) - 1

    @pl.when(i == 0)
    def _():
        zero_ref[...] = jnp.zeros_like(zero_ref)

        def clear_tile(start):
            for k in range(MOE_TILE // ZERO_ROWS):
                clear = pltpu.make_async_copy(zero_ref, _token_rows(xs_hbm, start + k * ZERO_ROWS, ZERO_ROWS), sem_zero)
                clear.start()
                clear.wait()

        for e in range(N_EXPERTS):
            @pl.when(ends_ref[e] > offs_ref[e])
            def _():
                clear_tile(ends_ref[e] - MOE_TILE)

        def clear_unused(j, carry):
            clear_tile(j * MOE_TILE)
            return carry
        lax.fori_loop(ends_ref[N_EXPERTS - 1] // MOE_TILE, xs_hbm.shape[0] // (MOE_TILE * CHUNKS), clear_unused, 0)

    def wait_rows(slot):
        for _ in range(2):
            pltpu.make_async_copy(xt_ref.at[slot], _token_rows(xs_hbm, 0, tm), sem_rows.at[slot]).wait()

    def step(slot):
        xt = xt_ref.at[slot]
        _store_token_tiles(xt, x_ref[...])
        base = i * (2 * tm)

        def send(grp, carry):
            ts = [grp * DMA_GROUP + j for j in range(DMA_GROUP)]
            dst = [(pos_ref[base + t], pos_ref[base + tm + t]) for t in ts]
            for t, (p1, p2) in zip(ts, dst):
                src = _token_rows(xt, t, 1)
                pltpu.make_async_copy(src, _token_rows(xs_hbm, p1, 1), sem_rows.at[slot]).start(priority=0)
                pltpu.make_async_copy(src, _token_rows(xs_hbm, p2, 1), sem_rows.at[slot]).start(priority=1)
            return carry
        lax.fori_loop(0, tm // DMA_GROUP, send, 0)

        @pl.when(i > 0)
        def _():
            wait_rows(1 - slot)

        @pl.when(i == last)
        def _():
            wait_rows(slot)

    for slot in range(2):
        pl.when(i % 2 == slot)(functools.partial(step, slot))


def _dispatch(x, pos, offs, ends, max_tiles, *, tm):
    n = x.shape[0]
    kern = functools.partial(_dispatch_kernel, tm=tm)
    return pl.pallas_call(
        kern,
        out_shape=jax.ShapeDtypeStruct((max_tiles * MOE_TILE * CHUNKS, LANES), F32),
        grid_spec=pltpu.PrefetchScalarGridSpec(
            num_scalar_prefetch=3,
            grid=(n // tm,),
            in_specs=[pl.BlockSpec((tm, D_MODEL), lambda i, offs, ends, pos: (i, 0))],
            out_specs=pl.BlockSpec(memory_space=pl.ANY),
            scratch_shapes=[pltpu.VMEM((2, tm * CHUNKS, LANES), F32), pltpu.VMEM((ZERO_ROWS * CHUNKS, LANES), F32),
                            pltpu.SemaphoreType.DMA, pltpu.SemaphoreType.DMA((2,))]),
        compiler_params=_params(("arbitrary",)),
        name="moe_dispatch",
    )(offs, ends, pos, x)


def _expert_kernel(te_ref, nu_ref, x_ref, wg_ref, wu_ref, wd_ref, o_ref, xb_ref, acc_ref):
    i = pl.program_id(0)
    f = pl.program_id(1)
    last_f = pl.num_programs(1) - 1

    def step(first, last):
        if first:
            xb = _load_token_tiles(x_ref, MOE_TILE).astype(BF16)
            xb_ref[...] = xb
        else:
            xb = xb_ref[...]
        gate = _dot(xb, wg_ref[...].astype(BF16))
        up = _dot(xb, wu_ref[...].astype(BF16))
        h = gate * jax.nn.sigmoid(gate) * up
        part = _dot(h.astype(BF16), wd_ref[...].astype(BF16))
        acc = part if first else acc_ref[...] + part
        if last:
            _store_token_tiles(o_ref, acc)
        else:
            acc_ref[...] = acc

    @pl.when(i < nu_ref[0])
    def _():
        pl.when(f == 0)(functools.partial(step, True, False))
        pl.when((f > 0) & (f < last_f))(functools.partial(step, False, False))
        pl.when(f == last_f)(functools.partial(step, False, True))

    @pl.when((i >= nu_ref[0]) & (f == 0))
    def _():
        o_ref[...] = jnp.zeros_like(o_ref)


def _experts(xs, tile_expert, n_used, w_gate, w_up, w_down, max_tiles, *, tf=512):
    d_ff = w_gate.shape[2]
    n_f = d_ff // tf

    def used_tile(i, nu):
        return jnp.minimum(i, jnp.maximum(nu[0] - 1, 0))

    def row_map(i, f, te, nu):
        return used_tile(i, nu), 0

    def out_map(i, f, te, nu):
        return i, 0

    def up_map(i, f, te, nu):
        return te[used_tile(i, nu)], 0, jnp.where(i < nu[0], f, n_f - 1)

    def down_map(i, f, te, nu):
        return te[used_tile(i, nu)], jnp.where(i < nu[0], f, n_f - 1), 0

    return pl.pallas_call(
        _expert_kernel,
        out_shape=jax.ShapeDtypeStruct((max_tiles * MOE_TILE * CHUNKS, LANES), F32),
        grid_spec=pltpu.PrefetchScalarGridSpec(
            num_scalar_prefetch=2,
            grid=(max_tiles, n_f),
            in_specs=[pl.BlockSpec((MOE_TILE * CHUNKS, LANES), row_map),
                      pl.BlockSpec((None, D_MODEL, tf), up_map),
                      pl.BlockSpec((None, D_MODEL, tf), up_map),
                      pl.BlockSpec((None, tf, D_MODEL), down_map)],
            out_specs=pl.BlockSpec((MOE_TILE * CHUNKS, LANES), out_map),
            scratch_shapes=[pltpu.VMEM((MOE_TILE, D_MODEL), BF16), pltpu.VMEM((MOE_TILE, D_MODEL), F32)]),
        compiler_params=_params(("arbitrary", "arbitrary")),
        name="moe_experts",
    )(tile_expert, n_used, xs, w_gate, w_up, w_down)


def _combine_kernel(pos_ref, x_ref, meta_ref, y_hbm, g_ref, b_ref, o_ref, buf_ref, sem_rows, *, tm):
    i = pl.program_id(0)
    n_steps = pl.num_programs(0)

    def fetch_rows(tile, slot):
        base = tile * (2 * tm)

        def fetch(grp, carry):
            ts = [grp * DMA_GROUP + j for j in range(DMA_GROUP)]
            src = [(pos_ref[base + t], pos_ref[base + tm + t]) for t in ts]
            for t, (p1, p2) in zip(ts, src):
                pltpu.make_async_copy(_token_rows(y_hbm, p1, 1), _token_rows(buf_ref.at[slot, 0], t, 1),
                                      sem_rows.at[slot]).start(priority=0)
                pltpu.make_async_copy(_token_rows(y_hbm, p2, 1), _token_rows(buf_ref.at[slot, 1], t, 1),
                                      sem_rows.at[slot]).start(priority=1)
            return carry
        lax.fori_loop(0, tm // DMA_GROUP, fetch, 0)

    @pl.when(i == 0)
    def _():
        fetch_rows(0, 0)

    def step(slot):
        @pl.when(i + 1 < n_steps)
        def _():
            fetch_rows(i + 1, 1 - slot)

        for k in range(2):
            pltpu.make_async_copy(_token_rows(y_hbm, 0, tm), buf_ref.at[slot, k], sem_rows.at[slot]).wait()
        meta = meta_ref[...]
        lane = lax.broadcasted_iota(jnp.int32, meta.shape, 1)
        g1 = jnp.sum(jnp.where(lane == META_G1, meta, 0.0), axis=1, keepdims=True)
        g2 = jnp.sum(jnp.where(lane == META_G2, meta, 0.0), axis=1, keepdims=True)
        mix = g1 * _load_token_tiles(buf_ref.at[slot, 0], tm) + g2 * _load_token_tiles(buf_ref.at[slot, 1], tm)
        o_ref[...] = _layer_norm(DN_ALPHA * x_ref[...] + mix, g_ref[...], b_ref[...])

    for slot in range(2):
        pl.when(i % 2 == slot)(functools.partial(step, slot))


def _combine(x, meta, y, pos, g, b, *, tm):
    n = x.shape[0]
    kern = functools.partial(_combine_kernel, tm=tm)
    vec = pl.BlockSpec((1, D_MODEL), lambda i, pos: (0, 0))
    return pl.pallas_call(
        kern,
        out_shape=jax.ShapeDtypeStruct((n, D_MODEL), F32),
        grid_spec=pltpu.PrefetchScalarGridSpec(
            num_scalar_prefetch=1,
            grid=(n // tm,),
            in_specs=[pl.BlockSpec((tm, D_MODEL), lambda i, pos: (i, 0)),
                      pl.BlockSpec((tm, LANES), lambda i, pos: (i, 0)),
                      pl.BlockSpec(memory_space=pl.ANY),
                      vec, vec],
            out_specs=pl.BlockSpec((tm, D_MODEL), lambda i, pos: (i, 0)),
            scratch_shapes=[pltpu.VMEM((2, 2, tm * CHUNKS, LANES), F32), pltpu.SemaphoreType.DMA((2,))]),
        compiler_params=_params(("arbitrary",)),
        name="moe_combine_ln",
    )(pos, x, meta, y, _row(g), _row(b))


def _moe_layer(x, w_router, w_gate, w_up, w_down, g, b, *, tm=512):
    meta, counts = _router(x, w_router)
    pos, offs, ends, tile_expert, n_used, max_tiles = _routing_tables(meta, counts, tm)
    xs = _dispatch(x, pos, offs, ends, max_tiles, tm=tm)
    y = _experts(xs, tile_expert, n_used, w_gate, w_up, w_down, max_tiles)
    return _combine(x, meta, y, pos, g, b, tm=tm)


def _rope_kernel(pos_ref, invf_ref, c_ref, s1_ref, s2_ref):
    ang = pos_ref[...].astype(F32) * invf_ref[...]
    c = jnp.cos(ang)
    s = jnp.sin(ang)
    dd = lax.broadcasted_iota(jnp.int32, ang.shape, 1) % HEAD_DIM
    c_ref[...] = c
    s1_ref[...] = jnp.where(dd < ROT_DIM // 2, -s, 0.0)
    s2_ref[...] = jnp.where((dd >= ROT_DIM // 2) & (dd < ROT_DIM), s, 0.0)


def _rope_tables(positions, *, tm=2048):
    n = positions.size
    half = ROT_DIM // 2
    inv_freq = ROPE_THETA ** (-(jnp.arange(0, ROT_DIM, 2, dtype=F32) / ROT_DIM))
    per_head = jnp.concatenate([inv_freq, inv_freq, jnp.zeros((HEAD_DIM - 2 * half,), F32)])
    invf = jnp.tile(per_head, LANES // HEAD_DIM).reshape(1, LANES)
    out = jax.ShapeDtypeStruct((n, LANES), F32)
    spec = pl.BlockSpec((tm, LANES), lambda i: (i, 0))
    return pl.pallas_call(
        _rope_kernel,
        out_shape=(out, out, out),
        grid=(n // tm,),
        in_specs=[pl.BlockSpec((tm, 1), lambda i: (i, 0)), pl.BlockSpec((1, LANES), lambda i: (0, 0))],
        out_specs=(spec, spec, spec),
        compiler_params=_params(("arbitrary",)),
        name="rope_tables",
    )(positions.reshape(n, 1), invf)


QKV_TILE = 512


def _qkv_kernel(x_ref, c_ref, s1_ref, s2_ref, w_f32, o_ref, w_ref, *, dil):
    _cast_once(w_ref, w_f32)
    tm = x_ref.shape[0]
    chunk = tm // dil
    xb = x_ref[...].astype(BF16)
    if dil == 1:
        c, s1, s2 = c_ref[...], s1_ref[...], s2_ref[...]
    else:
        dst = lax.broadcasted_iota(jnp.int32, (tm, tm), 0)
        tok = lax.broadcasted_iota(jnp.int32, (tm, tm), 1)
        perm = (tok == (dst % chunk) * dil + dst // chunk).astype(BF16)
        xb = _dot(perm, xb).astype(BF16)
        c, s1, s2 = (jnp.concatenate([t[pl.ds(r, chunk, stride=dil), :] for r in range(dil)], axis=0)
                     for t in (c_ref, s1_ref, s2_ref))
    y = _dot(xb, w_ref[...])
    for part in range(2):
        scale = HEAD_DIM ** -0.5 * LOG2_E if part == 0 else 1.0
        for blk in range(D_MODEL // LANES):
            lo = part * D_MODEL + blk * LANES
            t = y[:, lo:lo + LANES]
            rot = t * c + pltpu.roll(t, LANES - ROT_DIM // 2, 1) * s1 + pltpu.roll(t, ROT_DIM // 2, 1) * s2
            o_ref[:, lo:lo + LANES] = (rot * scale).astype(BF16)
    o_ref[:, 2 * D_MODEL:] = y[:, 2 * D_MODEL:].astype(BF16)


def _qkv_group(x, tabs, w_qkv, grp, dil):
    n = x.shape[0]
    tm = QKV_TILE
    kern = functools.partial(_qkv_kernel, dil=dil)
    tab_spec = pl.BlockSpec((tm, LANES), lambda i: (i, 0))
    return pl.pallas_call(
        kern,
        out_shape=jax.ShapeDtypeStruct((n, 3 * D_MODEL), BF16),
        grid=(n // tm,),
        in_specs=[pl.BlockSpec((tm, D_MODEL), lambda i: (i, 0)),
                  tab_spec, tab_spec, tab_spec,
                  _resident((D_MODEL, 3 * D_MODEL), lambda i: (0, grp))],
        out_specs=pl.BlockSpec((tm, 3 * D_MODEL), lambda i: (i, 0)),
        scratch_shapes=[pltpu.VMEM((D_MODEL, 3 * D_MODEL), BF16)],
        compiler_params=_params(("arbitrary",)),
        name="qkv_proj_dil%d" % dil,
    )(x, *tabs, w_qkv)


def _attn_kernel(*refs):
    qkv = refs[:9]
    o_ref = refs[9]
    acc_s, m_s, l_s = refs[10:13], refs[13:16], refs[16:19]

    lane = lax.broadcasted_iota(jnp.int32, (1, LANES), 1)
    head0 = lane < HEAD_DIM
    hm0 = head0.astype(BF16)
    hm1 = 1.0 - hm0
    qi = lax.broadcasted_iota(jnp.int32, (ATTN_BLOCK, ATTN_BLOCK), 0)
    kj = lax.broadcasted_iota(jnp.int32, (ATTN_BLOCK, ATTN_BLOCK), 1)
    cur_mask = jnp.where(qi <= kj, 0.0, NEG_BIG).astype(BF16)
    prev_mask = jnp.where(qi >= kj, 0.0, NEG_BIG).astype(BF16)
    mask_both = jnp.concatenate([prev_mask, cur_mask], axis=0)
    row_onehot = (qi == kj).astype(BF16)
    row_onehot = jnp.concatenate([row_onehot, row_onehot], axis=0)

    def block_rows(ref, dil, r, nb):
        chunk = QKV_TILE // dil
        if chunk >= ATTN_BLOCK:
            first = nb * ATTN_BLOCK
            base = first // chunk * QKV_TILE + r * chunk + first % chunk
            return ref[base:base + ATTN_BLOCK, :]
        pieces = ATTN_BLOCK // chunk
        starts = [(nb * pieces + m) * QKV_TILE + r * chunk for m in range(pieces)]
        return jnp.concatenate([ref[s:s + chunk, :] for s in starts], axis=0)

    def scores(grp, dil, r, nb):
        q_ref, k_ref, v_ref = qkv[3 * grp:3 * grp + 3]
        q = block_rows(q_ref, dil, r, nb)
        q2 = jnp.concatenate([q * hm0, q * hm1], axis=0)
        q2 = jnp.concatenate([q2, row_onehot], axis=1)
        kk = block_rows(k_ref, dil, r, nb)
        vv = block_rows(v_ref, dil, r, nb)
        if nb > 0:
            kk = jnp.concatenate([block_rows(k_ref, dil, r, nb - 1), kk], axis=0)
            vv = jnp.concatenate([block_rows(v_ref, dil, r, nb - 1), vv], axis=0)
            kk = jnp.concatenate([kk, mask_both], axis=1)
        else:
            kk = jnp.concatenate([kk, cur_mask], axis=1)
        s = lax.dot_general(q2, kk, (((1,), (1,)), ((), ())), preferred_element_type=F32)
        return s, vv

    def finish(grp, s, vv, nat_start, dil):
        m = jnp.max(s, axis=1, keepdims=True)
        pb = jnp.exp2(s - m).astype(BF16)
        ones = jnp.ones_like(vv)
        o0 = _dot(pb[:ATTN_BLOCK], jnp.concatenate([vv * hm0, ones], axis=1))
        o1 = _dot(pb[ATTN_BLOCK:], jnp.concatenate([vv * hm1, ones], axis=1))
        acc = o0[:, :LANES] + o1[:, :LANES]
        mb = jnp.where(head0, m[:ATTN_BLOCK], m[ATTN_BLOCK:])
        lb = jnp.where(head0, o0[:, LANES:], o1[:, LANES:])
        if dil == 1:
            rows = pl.ds(nat_start, ATTN_BLOCK)
        else:
            rows = pl.ds(nat_start, ATTN_BLOCK, stride=dil)
        acc_s[grp][rows, :] = acc
        m_s[grp][rows, :] = mb
        l_s[grp][rows, :] = lb

    for grp, dil in enumerate(ATTN_DILATIONS):
        n_blocks = SEQ // dil // ATTN_BLOCK
        blocks = [(r, 0) for r in range(dil)] + [(r, nb) for r in range(dil) for nb in range(1, n_blocks)]
        for lo in range(0, len(blocks), ATTN_UNROLL):
            batch = blocks[lo:lo + ATTN_UNROLL]
            staged = [scores(grp, dil, r, nb) for r, nb in batch]
            for (r, nb), (s, vv) in zip(batch, staged):
                finish(grp, s, vv, nb * ATTN_BLOCK * dil + r, dil)

    def merge(c, carry):
        rows = pl.ds(pl.multiple_of(c * ATTN_BLOCK, ATTN_BLOCK), ATTN_BLOCK)
        ms = [m_s[g][rows, :] for g in range(3)]
        top = jnp.maximum(jnp.maximum(ms[0], ms[1]), ms[2])
        num = jnp.zeros((ATTN_BLOCK, LANES), F32)
        den = jnp.zeros((ATTN_BLOCK, LANES), F32)
        for g in range(3):
            w = jnp.exp2(ms[g] - top)
            num = num + w * acc_s[g][rows, :]
            den = den + w * l_s[g][rows, :]
        o_ref[rows, :] = (num / den).astype(BF16)
        return carry
    lax.fori_loop(0, SEQ // ATTN_BLOCK, merge, 0)


def _attention(qkvs, n):
    n_pairs = D_MODEL // LANES
    in_specs, args = [], []
    for qkv in qkvs:
        for part in range(3):
            in_specs.append(pl.BlockSpec((SEQ, LANES), lambda b, hp, part=part: (b, part * n_pairs + hp)))
            args.append(qkv)
    scratch = [pltpu.VMEM((SEQ, LANES), F32) for _ in range(9)]
    return pl.pallas_call(
        _attn_kernel,
        out_shape=jax.ShapeDtypeStruct((n, D_MODEL), BF16),
        grid=(n // SEQ, n_pairs),
        in_specs=in_specs,
        out_specs=pl.BlockSpec((SEQ, LANES), lambda b, hp: (b, hp)),
        scratch_shapes=scratch,
        compiler_params=_params(("arbitrary", "arbitrary")),
        name="dilated_attention",
    )(*args)


def _proj_ln_kernel(x_ref, a_ref, w_f32, g_ref, b_ref, o_ref, w_ref):
    _cast_once(w_ref, w_f32)
    h = _dot(a_ref[...], w_ref[...])
    o_ref[...] = _layer_norm(DN_ALPHA * x_ref[...] + h, g_ref[...], b_ref[...])


def _proj_ln(x, a, w, g, b, *, tm=512):
    n = x.shape[0]
    vec = pl.BlockSpec((1, D_MODEL), lambda i: (0, 0))
    return pl.pallas_call(
        _proj_ln_kernel,
        out_shape=jax.ShapeDtypeStruct((n, D_MODEL), F32),
        grid=(n // tm,),
        in_specs=[pl.BlockSpec((tm, D_MODEL), lambda i: (i, 0)),
                  pl.BlockSpec((tm, D_MODEL), lambda i: (i, 0)),
                  _resident((D_MODEL, D_MODEL), lambda i: (0, 0)),
                  vec, vec],
        out_specs=pl.BlockSpec((tm, D_MODEL), lambda i: (i, 0)),
        scratch_shapes=[pltpu.VMEM((D_MODEL, D_MODEL), BF16)],
        compiler_params=_params(("arbitrary",)),
        name="out_proj_ln",
    )(x, a, w, _row(g), _row(b))


def _attn_layer(x, tabs, w_qkv, w_o, g, b):
    n = x.shape[0]
    qkvs = [_qkv_group(x, tabs, w_qkv, grp, dil) for grp, dil in enumerate(ATTN_DILATIONS)]
    return _proj_ln(x, _attention(qkvs, n), w_o, g, b)


def kernel(x, positions, l0_pool_w_in, l0_pool_w_grp, l0_pool_scale, l0_ln1_g, l0_ln1_b, l0_ffn_w_gate, l0_ffn_w_up, l0_ffn_w_down, l0_ln2_g, l0_ln2_b, l1_attn_w_qkv, l1_attn_w_o, l1_ln1_g, l1_ln1_b, l1_moe_w_router, l1_moe_w_gate, l1_moe_w_up, l1_moe_w_down, l1_ln2_g, l1_ln2_b, l2_conv_w_in, l2_conv_w, l2_conv_w_out, l2_ln1_g, l2_ln1_b, l2_ffn_w_gate, l2_ffn_w_up, l2_ffn_w_down, l2_ln2_g, l2_ln2_b, l3_pool_w_in, l3_pool_w_grp, l3_pool_scale, l3_ln1_g, l3_ln1_b, l3_moe_w_router, l3_moe_w_gate, l3_moe_w_up, l3_moe_w_down, l3_ln2_g, l3_ln2_b):
    batch, seq, d = x.shape
    h = x.reshape(batch * seq, d)
    tabs = _rope_tables(positions)
    h = _pool_layer(h, l0_pool_w_in, l0_pool_w_grp, l0_pool_scale, l0_ln1_g, l0_ln1_b)
    h = _ffn_layer(h, l0_ffn_w_gate, l0_ffn_w_up, l0_ffn_w_down, l0_ln2_g, l0_ln2_b)
    h = _attn_layer(h, tabs, l1_attn_w_qkv, l1_attn_w_o, l1_ln1_g, l1_ln1_b)
    h = _moe_layer(h, l1_moe_w_router, l1_moe_w_gate, l1_moe_w_up, l1_moe_w_down, l1_ln2_g, l1_ln2_b)
    h = _conv_layer(h, l2_conv_w_in, l2_conv_w, l2_conv_w_out, l2_ln1_g, l2_ln1_b)
    h = _ffn_layer(h, l2_ffn_w_gate, l2_ffn_w_up, l2_ffn_w_down, l2_ln2_g, l2_ln2_b)
    h = _pool_layer(h, l3_pool_w_in, l3_pool_w_grp, l3_pool_scale, l3_ln1_g, l3_ln1_b)
    h = _moe_layer(h, l3_moe_w_router, l3_moe_w_gate, l3_moe_w_up, l3_moe_w_down, l3_ln2_g, l3_ln2_b)
    return h.reshape(batch, seq, d)
```

```python
import functools

import jax
import jax.numpy as jnp
from jax import lax
from jax.experimental import pallas as pl
from jax.experimental.pallas import tpu as pltpu

D_MODEL = 1024
SEQ = 2048
DEPTH = 4
POOL_WINDOWS = (2, 4, 8, 16)
POOL_GROUP_DIM = D_MODEL // len(POOL_WINDOWS)
ATTN_DILATIONS = (1, 4, 16)
ATTN_BLOCK = 128
ATTN_UNROLL = 8
HEAD_DIM = 64
ROT_DIM = HEAD_DIM // 4
ROPE_THETA = 500000.0
LOG2_E = 1.4426950408889634
CONV_WIDTH = 3
N_EXPERTS = 8
MOE_TILE = 1024
META_E1, META_E2, META_R1, META_R2, META_G1, META_G2 = range(6)
DN_ALPHA = (2 * DEPTH) ** 0.25
LN_EPS = 1e-5

LANES = 128
HALO = 16
NEG_BIG = -1e30
VMEM_LIMIT = 56 * 1024 * 1024

F32 = jnp.float32
BF16 = jnp.bfloat16


def _params(semantics, vmem=VMEM_LIMIT):
    return pltpu.CompilerParams(dimension_semantics=semantics, vmem_limit_bytes=vmem)


def _dot(a, b):
    return jnp.dot(a, b, preferred_element_type=F32)


def _layer_norm(z, g, b):
    mu = jnp.mean(z, axis=-1, keepdims=True)
    zc = z - mu
    var = jnp.mean(zc * zc, axis=-1, keepdims=True)
    return zc * lax.rsqrt(var + LN_EPS) * g + b


def _row(v):
    return v.reshape(1, -1)


def _resident(shape, index_map):
    return pl.BlockSpec(shape, index_map, pipeline_mode=pl.Buffered(1))


def _cast_once(dst_ref, src_ref):
    @pl.when(pl.program_id(0) == 0)
    def _():
        dst_ref[...] = src_ref[...].astype(BF16)


def _pool_kernel(x_ref, w_in_f32, w_grp_f32, scale_ref, g_ref, b_ref, o_ref, halo_ref, w_in_ref, w_grp_ref, *, tm, tiles_per_seq):
    i = pl.program_id(0)
    _cast_once(w_in_ref, w_in_f32)
    _cast_once(w_grp_ref, w_grp_f32)
    x = x_ref[...]
    u = _dot(x.astype(BF16), w_in_ref[...])

    @pl.when(i % tiles_per_seq == 0)
    def _():
        halo_ref[...] = jnp.zeros_like(halo_ref)

    buf = jnp.concatenate([halo_ref[...], u], axis=0)
    halo_ref[...] = u[tm - HALO:, :]
    t = (i % tiles_per_seq) * tm + lax.broadcasted_iota(jnp.int32, (tm, 1), 0)
    outs = []
    for grp, w in enumerate(POOL_WINDOWS):
        cols = slice(grp * POOL_GROUP_DIM, (grp + 1) * POOL_GROUP_DIM)
        s = buf[:, cols]
        k = 1
        while k < w:
            s = s + pltpu.roll(s, k, 0)
            k *= 2
        cnt = jnp.minimum(t + 1, w).astype(F32)
        pooled = s[HALO:, :] / cnt - u[:, cols]
        outs.append(_dot(pooled.astype(BF16), w_grp_ref[grp]))
    h = jnp.concatenate(outs, axis=1) * scale_ref[...]
    o_ref[...] = _layer_norm(DN_ALPHA * x + h, g_ref[...], b_ref[...])


def _pool_layer(x, w_in, w_grp, scale, g, b, *, tm=512):
    n = x.shape[0]
    kern = functools.partial(_pool_kernel, tm=tm, tiles_per_seq=SEQ // tm)
    vec = pl.BlockSpec((1, D_MODEL), lambda i: (0, 0))
    return pl.pallas_call(
        kern,
        out_shape=jax.ShapeDtypeStruct((n, D_MODEL), F32),
        grid=(n // tm,),
        in_specs=[pl.BlockSpec((tm, D_MODEL), lambda i: (i, 0)),
                  _resident((D_MODEL, D_MODEL), lambda i: (0, 0)),
                  _resident((len(POOL_WINDOWS), POOL_GROUP_DIM, POOL_GROUP_DIM), lambda i: (0, 0, 0)),
                  vec, vec, vec],
        out_specs=pl.BlockSpec((tm, D_MODEL), lambda i: (i, 0)),
        scratch_shapes=[pltpu.VMEM((HALO, D_MODEL), F32), pltpu.VMEM((D_MODEL, D_MODEL), BF16),
                        pltpu.VMEM((len(POOL_WINDOWS), POOL_GROUP_DIM, POOL_GROUP_DIM), BF16)],
        compiler_params=_params(("arbitrary",)),
        name="pool_mixer_ln",
    )(x, w_in, w_grp, _row(scale), _row(g), _row(b))


def _conv_kernel(x_ref, w_in_f32, cw_ref, w_out_f32, g_ref, b_ref, o_ref, halo_ref, w_in_ref, w_out_ref, *, tm, tiles_per_seq):
    i = pl.program_id(0)
    _cast_once(w_in_ref, w_in_f32)
    _cast_once(w_out_ref, w_out_f32)
    x = x_ref[...]
    proj = _dot(x.astype(BF16), w_in_ref[...])
    gate_b = proj[:, :D_MODEL]
    z = proj[:, D_MODEL:2 * D_MODEL] * proj[:, 2 * D_MODEL:]

    @pl.when(i % tiles_per_seq == 0)
    def _():
        halo_ref[...] = jnp.zeros_like(halo_ref)

    buf = jnp.concatenate([halo_ref[...], z], axis=0)
    halo_ref[...] = z[tm - HALO:, :]
    conv = cw_ref[0:1, :] * z
    for j in range(1, CONV_WIDTH):
        conv = conv + cw_ref[j:j + 1, :] * pltpu.roll(buf, j, 0)[HALO:, :]
    h = _dot((gate_b * conv).astype(BF16), w_out_ref[...])
    o_ref[...] = _layer_norm(DN_ALPHA * x + h, g_ref[...], b_ref[...])


def _conv_layer(x, w_in, conv_w, w_out, g, b, *, tm=512):
    n = x.shape[0]
    kern = functools.partial(_conv_kernel, tm=tm, tiles_per_seq=SEQ // tm)
    vec = pl.BlockSpec((1, D_MODEL), lambda i: (0, 0))
    return pl.pallas_call(
        kern,
        out_shape=jax.ShapeDtypeStruct((n, D_MODEL), F32),
        grid=(n // tm,),
        in_specs=[pl.BlockSpec((tm, D_MODEL), lambda i: (i, 0)),
                  _resident((D_MODEL, 3 * D_MODEL), lambda i: (0, 0)),
                  pl.BlockSpec((CONV_WIDTH, D_MODEL), lambda i: (0, 0)),
                  _resident((D_MODEL, D_MODEL), lambda i: (0, 0)),
                  vec, vec],
        out_specs=pl.BlockSpec((tm, D_MODEL), lambda i: (i, 0)),
        scratch_shapes=[pltpu.VMEM((HALO, D_MODEL), F32), pltpu.VMEM((D_MODEL, 3 * D_MODEL), BF16),
                        pltpu.VMEM((D_MODEL, D_MODEL), BF16)],
        compiler_params=_params(("arbitrary",)),
        name="conv_mixer_ln",
    )(x, w_in, conv_w, w_out, _row(g), _row(b))


FFN_CHUNK = 256


def _ffn_kernel(x_ref, wg_ref, wu_ref, wd_ref, g_ref, b_ref, o_ref):
    x = x_ref[...]
    xb = x.astype(BF16)
    acc = jnp.zeros(x.shape, F32)
    for lo in range(0, wg_ref.shape[1], FFN_CHUNK):
        gate = _dot(xb, wg_ref[:, lo:lo + FFN_CHUNK])
        up = _dot(xb, wu_ref[:, lo:lo + FFN_CHUNK])
        h = gate * jax.nn.sigmoid(gate) * up
        acc = acc + _dot(h.astype(BF16), wd_ref[lo:lo + FFN_CHUNK, :])
    o_ref[...] = _layer_norm(DN_ALPHA * x + acc, g_ref[...], b_ref[...])


def _ffn_layer(x, w_gate, w_up, w_down, g, b, *, tm=1024):
    n = x.shape[0]
    d_ff = w_gate.shape[1]
    vec = pl.BlockSpec((1, D_MODEL), lambda i: (0, 0))
    once = pl.Buffered(1)
    return pl.pallas_call(
        _ffn_kernel,
        out_shape=jax.ShapeDtypeStruct((n, D_MODEL), F32),
        grid=(n // tm,),
        in_specs=[pl.BlockSpec((tm, D_MODEL), lambda i: (i, 0)),
                  pl.BlockSpec((D_MODEL, d_ff), lambda i: (0, 0), pipeline_mode=once),
                  pl.BlockSpec((D_MODEL, d_ff), lambda i: (0, 0), pipeline_mode=once),
                  pl.BlockSpec((d_ff, D_MODEL), lambda i: (0, 0), pipeline_mode=once),
                  vec, vec],
        out_specs=pl.BlockSpec((tm, D_MODEL), lambda i: (i, 0)),
        compiler_params=_params(("arbitrary",)),
        name="swiglu_ln",
    )(x, w_gate.astype(BF16), w_up.astype(BF16), w_down.astype(BF16), _row(g), _row(b))


CHUNKS = D_MODEL // LANES


def _store_token_tiles(ref, y):
    for c in range(CHUNKS):
        ref[pl.ds(c, y.shape[0], stride=CHUNKS), :] = y[:, c * LANES:(c + 1) * LANES]


def _load_token_tiles(ref, rows):
    return jnp.concatenate([ref[pl.ds(c, rows, stride=CHUNKS), :] for c in range(CHUNKS)], axis=1)


def _router_kernel(x_ref, w_ref, meta_ref, cnt_ref, run_ref):
    i = pl.program_id(0)

    @pl.when(i == 0)
    def _():
        run_ref[...] = jnp.zeros_like(run_ref)

    x = x_ref[...]
    w = w_ref[...]
    xh = x.astype(BF16)
    xl = (x - xh.astype(F32)).astype(BF16)
    wh = w.astype(BF16)
    wl = (w - wh.astype(F32)).astype(BF16)
    logits = _dot(xh, wh) + (_dot(xl, wh) + _dot(xh, wl))
    tm = logits.shape[0]
    lane = lax.broadcasted_iota(jnp.int32, logits.shape, 1)
    logits = jnp.where(lane < N_EXPERTS, logits, -jnp.inf)
    v1 = jnp.max(logits, axis=1, keepdims=True)
    i1 = jnp.min(jnp.where(logits == v1, lane, LANES), axis=1, keepdims=True)
    rest = jnp.where(lane == i1, -jnp.inf, logits)
    v2 = jnp.max(rest, axis=1, keepdims=True)
    i2 = jnp.min(jnp.where(rest == v2, lane, LANES), axis=1, keepdims=True)
    e2 = jnp.exp(v2 - v1)
    g1 = 1.0 / (1.0 + e2)
    g2 = e2 / (1.0 + e2)

    sel = jnp.where(lane == i1, 1.0, jnp.where(lane == i2, 1.0, 0.0))
    before = (lax.broadcasted_iota(jnp.int32, (tm, tm), 1) < lax.broadcasted_iota(jnp.int32, (tm, tm), 0))
    rank = run_ref[0:1, :] + _dot(before.astype(BF16), sel.astype(BF16))
    r1 = jnp.sum(jnp.where(lane == i1, rank, 0.0), axis=1, keepdims=True)
    r2 = jnp.sum(jnp.where(lane == i2, rank, 0.0), axis=1, keepdims=True)
    run_ref[...] = run_ref[...] + jnp.sum(sel, axis=0, keepdims=True)
    cnt_ref[...] = run_ref[...]
    meta = jnp.zeros_like(logits)
    for k, val in enumerate((i1.astype(F32), i2.astype(F32), r1, r2, g1, g2)):
        meta = jnp.where(lane == k, val, meta)
    meta_ref[...] = meta


def _router(x, w_router, *, tm=512):
    n = x.shape[0]
    w = jnp.pad(w_router, ((0, 0), (0, LANES - N_EXPERTS)))
    return pl.pallas_call(
        _router_kernel,
        out_shape=(jax.ShapeDtypeStruct((n, LANES), F32), jax.ShapeDtypeStruct((8, LANES), F32)),
        grid=(n // tm,),
        in_specs=[pl.BlockSpec((tm, D_MODEL), lambda i: (i, 0)),
                  pl.BlockSpec((D_MODEL, LANES), lambda i: (0, 0))],
        out_specs=(pl.BlockSpec((tm, LANES), lambda i: (i, 0)), pl.BlockSpec((8, LANES), lambda i: (0, 0))),
        scratch_shapes=[pltpu.VMEM((8, LANES), F32)],
        compiler_params=_params(("arbitrary",)),
        name="router_top2",
    )(x, w)


def _routing_tables(meta, counts, tm):
    n = meta.shape[0]
    cnt = counts[0, :N_EXPERTS].astype(jnp.int32)
    padded = (cnt + MOE_TILE - 1) // MOE_TILE * MOE_TILE
    ends = jnp.cumsum(padded)
    offs = ends - padded
    sel = meta[:, :4].astype(jnp.int32)
    pos = offs[sel[:, :2]] + sel[:, 2:]
    pos = pos.reshape(n // tm, tm, 2).transpose(0, 2, 1).reshape(2 * n)
    max_tiles = (2 * n + N_EXPERTS * (MOE_TILE - 1)) // MOE_TILE
    first_row = jnp.arange(max_tiles, dtype=jnp.int32) * MOE_TILE
    tile_expert = jnp.minimum(jnp.sum(first_row[:, None] >= ends[None, :], axis=1), N_EXPERTS - 1).astype(jnp.int32)
    return pos, offs, ends, tile_expert, ends[-1:] // MOE_TILE, max_tiles


ZERO_ROWS = 128
DMA_GROUP = 16


def _token_rows(ref, start, rows):
    return ref.at[pl.ds(pl.multiple_of(start * CHUNKS, CHUNKS), rows * CHUNKS)]


def _dispatch_kernel(offs_ref, ends_ref, pos_ref, x_ref, xs_hbm, xt_ref, zero_ref, sem_zero, sem_rows, *, tm):
    i = pl.program_id(0)
    last = pl.num_programs(0) - 1

    @pl.when(i == 0)
    def _():
        zero_ref[...] = jnp.zeros_like(zero_ref)
        first_unused = ends_ref[N_EXPERTS - 1] // MOE_TILE
        n_tiles = xs_hbm.shape[0] // (MOE_TILE * CHUNKS)

        def clear_tile(start, wait):
            for k in range(MOE_TILE // ZERO_ROWS):
                clear = pltpu.make_async_copy(zero_ref, _token_rows(xs_hbm, start + k * ZERO_ROWS, ZERO_ROWS), sem_zero)
                clear.wait() if wait else clear.start()

        def clear_unused(wait, j, carry):
            clear_tile(j * MOE_TILE, wait)
            return carry

        for wait in (False, True):
            for e in range(N_EXPERTS):
                pl.when(ends_ref[e] > offs_ref[e])(functools.partial(clear_tile, ends_ref[e] - MOE_TILE, wait))
            lax.fori_loop(first_unused, n_tiles, functools.partial(clear_unused, wait), 0)

    def wait_rows(slot):
        for _ in range(2):
            pltpu.make_async_copy(xt_ref.at[slot], _token_rows(xs_hbm, 0, tm), sem_rows.at[slot]).wait()

    def step(slot):
        xt = xt_ref.at[slot]
        _store_token_tiles(xt, x_ref[...])
        base = i * (2 * tm)

        def send(grp, carry):
            ts = [grp * DMA_GROUP + j for j in range(DMA_GROUP)]
            dst = [(pos_ref[base + t], pos_ref[base + tm + t]) for t in ts]
            for t, (p1, p2) in zip(ts, dst):
                src = _token_rows(xt, t, 1)
                pltpu.make_async_copy(src, _token_rows(xs_hbm, p1, 1), sem_rows.at[slot]).start(priority=0)
                pltpu.make_async_copy(src, _token_rows(xs_hbm, p2, 1), sem_rows.at[slot]).start(priority=1)
            return carry
        lax.fori_loop(0, tm // DMA_GROUP, send, 0)

        @pl.when(i > 0)
        def _():
            wait_rows(1 - slot)

        @pl.when(i == last)
        def _():
            wait_rows(slot)

    for slot in range(2):
        pl.when(i % 2 == slot)(functools.partial(step, slot))


def _dispatch(x, pos, offs, ends, max_tiles, *, tm):
    n = x.shape[0]
    kern = functools.partial(_dispatch_kernel, tm=tm)
    return pl.pallas_call(
        kern,
        out_shape=jax.ShapeDtypeStruct((max_tiles * MOE_TILE * CHUNKS, LANES), F32),
        grid_spec=pltpu.PrefetchScalarGridSpec(
            num_scalar_prefetch=3,
            grid=(n // tm,),
            in_specs=[pl.BlockSpec((tm, D_MODEL), lambda i, offs, ends, pos: (i, 0))],
            out_specs=pl.BlockSpec(memory_space=pl.ANY),
            scratch_shapes=[pltpu.VMEM((2, tm * CHUNKS, LANES), F32), pltpu.VMEM((ZERO_ROWS * CHUNKS, LANES), F32),
                            pltpu.SemaphoreType.DMA, pltpu.SemaphoreType.DMA((2,))]),
        compiler_params=_params(("arbitrary",)),
        name="moe_dispatch",
    )(offs, ends, pos, x)


def _expert_kernel(te_ref, nu_ref, x_ref, wg_ref, wu_ref, wd_ref, o_ref, xb_ref, acc_ref):
    i = pl.program_id(0)
    f = pl.program_id(1)
    last_f = pl.num_programs(1) - 1

    def step(first, last):
        if first:
            xb = _load_token_tiles(x_ref, MOE_TILE).astype(BF16)
            xb_ref[...] = xb
        else:
            xb = xb_ref[...]
        gate = _dot(xb, wg_ref[...].astype(BF16))
        up = _dot(xb, wu_ref[...].astype(BF16))
        h = gate * jax.nn.sigmoid(gate) * up
        part = _dot(h.astype(BF16), wd_ref[...].astype(BF16))
        acc = part if first else acc_ref[...] + part
        if last:
            _store_token_tiles(o_ref, acc)
        else:
            acc_ref[...] = acc

    @pl.when(i < nu_ref[0])
    def _():
        pl.when(f == 0)(functools.partial(step, True, False))
        pl.when((f > 0) & (f < last_f))(functools.partial(step, False, False))
        pl.when(f == last_f)(functools.partial(step, False, True))

    @pl.when((i >= nu_ref[0]) & (f == 0))
    def _():
        o_ref[...] = jnp.zeros_like(o_ref)


def _experts(xs, tile_expert, n_used, w_gate, w_up, w_down, max_tiles, *, tf=512):
    d_ff = w_gate.shape[2]
    n_f = d_ff // tf

    def used_tile(i, nu):
        return jnp.minimum(i, jnp.maximum(nu[0] - 1, 0))

    def row_map(i, f, te, nu):
        return used_tile(i, nu), 0

    def out_map(i, f, te, nu):
        return i, 0

    def up_map(i, f, te, nu):
        return te[used_tile(i, nu)], 0, jnp.where(i < nu[0], f, n_f - 1)

    def down_map(i, f, te, nu):
        return te[used_tile(i, nu)], jnp.where(i < nu[0], f, n_f - 1), 0

    return pl.pallas_call(
        _expert_kernel,
        out_shape=jax.ShapeDtypeStruct((max_tiles * MOE_TILE * CHUNKS, LANES), F32),
        grid_spec=pltpu.PrefetchScalarGridSpec(
            num_scalar_prefetch=2,
            grid=(max_tiles, n_f),
            in_specs=[pl.BlockSpec((MOE_TILE * CHUNKS, LANES), row_map),
                      pl.BlockSpec((None, D_MODEL, tf), up_map),
                      pl.BlockSpec((None, D_MODEL, tf), up_map),
                      pl.BlockSpec((None, tf, D_MODEL), down_map)],
            out_specs=pl.BlockSpec((MOE_TILE * CHUNKS, LANES), out_map),
            scratch_shapes=[pltpu.VMEM((MOE_TILE, D_MODEL), BF16), pltpu.VMEM((MOE_TILE, D_MODEL), F32)]),
        compiler_params=_params(("arbitrary", "arbitrary")),
        name="moe_experts",
    )(tile_expert, n_used, xs, w_gate, w_up, w_down)


def _combine_kernel(pos_ref, x_ref, meta_ref, y_hbm, g_ref, b_ref, o_ref, buf_ref, sem_rows, *, tm):
    i = pl.program_id(0)
    n_steps = pl.num_programs(0)

    def fetch_rows(tile, slot):
        base = tile * (2 * tm)

        def fetch(grp, carry):
            ts = [grp * DMA_GROUP + j for j in range(DMA_GROUP)]
            src = [(pos_ref[base + t], pos_ref[base + tm + t]) for t in ts]
            for t, (p1, p2) in zip(ts, src):
                pltpu.make_async_copy(_token_rows(y_hbm, p1, 1), _token_rows(buf_ref.at[slot, 0], t, 1),
                                      sem_rows.at[slot]).start(priority=0)
                pltpu.make_async_copy(_token_rows(y_hbm, p2, 1), _token_rows(buf_ref.at[slot, 1], t, 1),
                                      sem_rows.at[slot]).start(priority=1)
            return carry
        lax.fori_loop(0, tm // DMA_GROUP, fetch, 0)

    @pl.when(i == 0)
    def _():
        fetch_rows(0, 0)

    def step(slot):
        @pl.when(i + 1 < n_steps)
        def _():
            fetch_rows(i + 1, 1 - slot)

        for k in range(2):
            pltpu.make_async_copy(_token_rows(y_hbm, 0, tm), buf_ref.at[slot, k], sem_rows.at[slot]).wait()
        meta = meta_ref[...]
        lane = lax.broadcasted_iota(jnp.int32, meta.shape, 1)
        g1 = jnp.sum(jnp.where(lane == META_G1, meta, 0.0), axis=1, keepdims=True)
        g2 = jnp.sum(jnp.where(lane == META_G2, meta, 0.0), axis=1, keepdims=True)
        mix = g1 * _load_token_tiles(buf_ref.at[slot, 0], tm) + g2 * _load_token_tiles(buf_ref.at[slot, 1], tm)
        o_ref[...] = _layer_norm(DN_ALPHA * x_ref[...] + mix, g_ref[...], b_ref[...])

    for slot in range(2):
        pl.when(i % 2 == slot)(functools.partial(step, slot))


def _combine(x, meta, y, pos, g, b, *, tm):
    n = x.shape[0]
    kern = functools.partial(_combine_kernel, tm=tm)
    vec = pl.BlockSpec((1, D_MODEL), lambda i, pos: (0, 0))
    return pl.pallas_call(
        kern,
        out_shape=jax.ShapeDtypeStruct((n, D_MODEL), F32),
        grid_spec=pltpu.PrefetchScalarGridSpec(
            num_scalar_prefetch=1,
            grid=(n // tm,),
            in_specs=[pl.BlockSpec((tm, D_MODEL), lambda i, pos: (i, 0)),
                      pl.BlockSpec((tm, LANES), lambda i, pos: (i, 0)),
                      pl.BlockSpec(memory_space=pl.ANY),
                      vec, vec],
            out_specs=pl.BlockSpec((tm, D_MODEL), lambda i, pos: (i, 0)),
            scratch_shapes=[pltpu.VMEM((2, 2, tm * CHUNKS, LANES), F32), pltpu.SemaphoreType.DMA((2,))]),
        compiler_params=_params(("arbitrary",)),
        name="moe_combine_ln",
    )(pos, x, meta, y, _row(g), _row(b))


def _moe_layer(x, w_router, w_gate, w_up, w_down, g, b, *, tm=512):
    meta, counts = _router(x, w_router)
    pos, offs, ends, tile_expert, n_used, max_tiles = _routing_tables(meta, counts, tm)
    xs = _dispatch(x, pos, offs, ends, max_tiles, tm=tm)
    y = _experts(xs, tile_expert, n_used, w_gate, w_up, w_down, max_tiles)
    return _combine(x, meta, y, pos, g, b, tm=tm)


def _rope_kernel(pos_ref, invf_ref, c_ref, s1_ref, s2_ref):
    ang = pos_ref[...].astype(F32) * invf_ref[...]
    c = jnp.cos(ang)
    s = jnp.sin(ang)
    dd = lax.broadcasted_iota(jnp.int32, ang.shape, 1) % HEAD_DIM
    c_ref[...] = c
    s1_ref[...] = jnp.where(dd < ROT_DIM // 2, -s, 0.0)
    s2_ref[...] = jnp.where((dd >= ROT_DIM // 2) & (dd < ROT_DIM), s, 0.0)


def _rope_tables(positions, *, tm=2048):
    n = positions.size
    half = ROT_DIM // 2
    inv_freq = ROPE_THETA ** (-(jnp.arange(0, ROT_DIM, 2, dtype=F32) / ROT_DIM))
    per_head = jnp.concatenate([inv_freq, inv_freq, jnp.zeros((HEAD_DIM - 2 * half,), F32)])
    invf = jnp.tile(per_head, LANES // HEAD_DIM).reshape(1, LANES)
    out = jax.ShapeDtypeStruct((n, LANES), F32)
    spec = pl.BlockSpec((tm, LANES), lambda i: (i, 0))
    return pl.pallas_call(
        _rope_kernel,
        out_shape=(out, out, out),
        grid=(n // tm,),
        in_specs=[pl.BlockSpec((tm, 1), lambda i: (i, 0)), pl.BlockSpec((1, LANES), lambda i: (0, 0))],
        out_specs=(spec, spec, spec),
        compiler_params=_params(("arbitrary",)),
        name="rope_tables",
    )(positions.reshape(n, 1), invf)


QKV_TILE = 512


def _qkv_kernel(x_ref, c_ref, s1_ref, s2_ref, w_f32, o_ref, w_ref, *, dil):
    _cast_once(w_ref, w_f32)
    tm = x_ref.shape[0]
    chunk = tm // dil
    xb = x_ref[...].astype(BF16)
    if dil == 1:
        c, s1, s2 = c_ref[...], s1_ref[...], s2_ref[...]
    else:
        dst = lax.broadcasted_iota(jnp.int32, (tm, tm), 0)
        tok = lax.broadcasted_iota(jnp.int32, (tm, tm), 1)
        perm = (tok == (dst % chunk) * dil + dst // chunk).astype(BF16)
        xb = _dot(perm, xb).astype(BF16)
        c, s1, s2 = (jnp.concatenate([t[pl.ds(r, chunk, stride=dil), :] for r in range(dil)], axis=0)
                     for t in (c_ref, s1_ref, s2_ref))
    y = _dot(xb, w_ref[...])
    for part in range(2):
        scale = HEAD_DIM ** -0.5 * LOG2_E if part == 0 else 1.0
        for blk in range(D_MODEL // LANES):
            lo = part * D_MODEL + blk * LANES
            t = y[:, lo:lo + LANES]
            rot = t * c + pltpu.roll(t, LANES - ROT_DIM // 2, 1) * s1 + pltpu.roll(t, ROT_DIM // 2, 1) * s2
            o_ref[:, lo:lo + LANES] = (rot * scale).astype(BF16)
    o_ref[:, 2 * D_MODEL:] = y[:, 2 * D_MODEL:].astype(BF16)


def _qkv_group(x, tabs, w_qkv, grp, dil):
    n = x.shape[0]
    tm = QKV_TILE
    kern = functools.partial(_qkv_kernel, dil=dil)
    tab_spec = pl.BlockSpec((tm, LANES), lambda i: (i, 0))
    return pl.pallas_call(
        kern,
        out_shape=jax.ShapeDtypeStruct((n, 3 * D_MODEL), BF16),
        grid=(n // tm,),
        in_specs=[pl.BlockSpec((tm, D_MODEL), lambda i: (i, 0)),
                  tab_spec, tab_spec, tab_spec,
                  _resident((D_MODEL, 3 * D_MODEL), lambda i: (0, grp))],
        out_specs=pl.BlockSpec((tm, 3 * D_MODEL), lambda i: (i, 0)),
        scratch_shapes=[pltpu.VMEM((D_MODEL, 3 * D_MODEL), BF16)],
        compiler_params=_params(("arbitrary",)),
        name="qkv_proj_dil%d" % dil,
    )(x, *tabs, w_qkv)


def _attn_kernel(*refs):
    qkv = refs[:9]
    o_ref = refs[9]
    acc_s, m_s, l_s = refs[10:13], refs[13:16], refs[16:19]

    lane = lax.broadcasted_iota(jnp.int32, (1, LANES), 1)
    head0 = lane < HEAD_DIM
    hm0 = head0.astype(BF16)
    hm1 = 1.0 - hm0
    qi = lax.broadcasted_iota(jnp.int32, (ATTN_BLOCK, ATTN_BLOCK), 0)
    kj = lax.broadcasted_iota(jnp.int32, (ATTN_BLOCK, ATTN_BLOCK), 1)
    cur_mask = jnp.where(qi <= kj, 0.0, NEG_BIG).astype(BF16)
    prev_mask = jnp.where(qi >= kj, 0.0, NEG_BIG).astype(BF16)
    mask_both = jnp.concatenate([prev_mask, cur_mask], axis=0)
    row_onehot = (qi == kj).astype(BF16)
    row_onehot = jnp.concatenate([row_onehot, row_onehot], axis=0)

    def block_rows(ref, dil, r, nb):
        chunk = QKV_TILE // dil
        if chunk >= ATTN_BLOCK:
            first = nb * ATTN_BLOCK
            base = first // chunk * QKV_TILE + r * chunk + first % chunk
            return ref[base:base + ATTN_BLOCK, :]
        pieces = ATTN_BLOCK // chunk
        starts = [(nb * pieces + m) * QKV_TILE + r * chunk for m in range(pieces)]
        return jnp.concatenate([ref[s:s + chunk, :] for s in starts], axis=0)

    def scores(grp, dil, r, nb):
        q_ref, k_ref, v_ref = qkv[3 * grp:3 * grp + 3]
        q = block_rows(q_ref, dil, r, nb)
        q2 = jnp.concatenate([q * hm0, q * hm1], axis=0)
        q2 = jnp.concatenate([q2, row_onehot], axis=1)
        kk = block_rows(k_ref, dil, r, nb)
        vv = block_rows(v_ref, dil, r, nb)
        if nb > 0:
            kk = jnp.concatenate([block_rows(k_ref, dil, r, nb - 1), kk], axis=0)
            vv = jnp.concatenate([block_rows(v_ref, dil, r, nb - 1), vv], axis=0)
            kk = jnp.concatenate([kk, mask_both], axis=1)
        else:
            kk = jnp.concatenate([kk, cur_mask], axis=1)
        s = lax.dot_general(q2, kk, (((1,), (1,)), ((), ())), preferred_element_type=F32)
        return s, vv

    def finish(grp, s, vv, nat_start, dil):
        m = jnp.max(s, axis=1, keepdims=True)
        pb = jnp.exp2(s - m).astype(BF16)
        ones = jnp.ones_like(vv)
        o0 = _dot(pb[:ATTN_BLOCK], jnp.concatenate([vv * hm0, ones], axis=1))
        o1 = _dot(pb[ATTN_BLOCK:], jnp.concatenate([vv * hm1, ones], axis=1))
        acc = o0[:, :LANES] + o1[:, :LANES]
        mb = jnp.where(head0, m[:ATTN_BLOCK], m[ATTN_BLOCK:])
        lb = jnp.where(head0, o0[:, LANES:], o1[:, LANES:])
        if dil == 1:
            rows = pl.ds(nat_start, ATTN_BLOCK)
        else:
            rows = pl.ds(nat_start, ATTN_BLOCK, stride=dil)
        acc_s[grp][rows, :] = acc
        m_s[grp][rows, :] = mb
        l_s[grp][rows, :] = lb

    for grp, dil in enumerate(ATTN_DILATIONS):
        n_blocks = SEQ // dil // ATTN_BLOCK
        blocks = [(r, 0) for r in range(dil)] + [(r, nb) for r in range(dil) for nb in range(1, n_blocks)]
        for lo in range(0, len(blocks), ATTN_UNROLL):
            batch = blocks[lo:lo + ATTN_UNROLL]
            staged = [scores(grp, dil, r, nb) for r, nb in batch]
            for (r, nb), (s, vv) in zip(batch, staged):
                finish(grp, s, vv, nb * ATTN_BLOCK * dil + r, dil)

    def merge(c, carry):
        rows = pl.ds(pl.multiple_of(c * ATTN_BLOCK, ATTN_BLOCK), ATTN_BLOCK)
        ms = [m_s[g][rows, :] for g in range(3)]
        top = jnp.maximum(jnp.maximum(ms[0], ms[1]), ms[2])
        num = jnp.zeros((ATTN_BLOCK, LANES), F32)
        den = jnp.zeros((ATTN_BLOCK, LANES), F32)
        for g in range(3):
            w = jnp.exp2(ms[g] - top)
            num = num + w * acc_s[g][rows, :]
            den = den + w * l_s[g][rows, :]
        o_ref[rows, :] = (num / den).astype(BF16)
        return carry
    lax.fori_loop(0, SEQ // ATTN_BLOCK, merge, 0)


def _attention(qkvs, n):
    n_pairs = D_MODEL // LANES
    in_specs, args = [], []
    for qkv in qkvs:
        for part in range(3):
            in_specs.append(pl.BlockSpec((SEQ, LANES), lambda b, hp, part=part: (b, part * n_pairs + hp)))
            args.append(qkv)
    scratch = [pltpu.VMEM((SEQ, LANES), F32) for _ in range(9)]
    return pl.pallas_call(
        _attn_kernel,
        out_shape=jax.ShapeDtypeStruct((n, D_MODEL), BF16),
        grid=(n // SEQ, n_pairs),
        in_specs=in_specs,
        out_specs=pl.BlockSpec((SEQ, LANES), lambda b, hp: (b, hp)),
        scratch_shapes=scratch,
        compiler_params=_params(("arbitrary", "arbitrary")),
        name="dilated_attention",
    )(*args)


def _proj_ln_kernel(x_ref, a_ref, w_f32, g_ref, b_ref, o_ref, w_ref):
    _cast_once(w_ref, w_f32)
    h = _dot(a_ref[...], w_ref[...])
    o_ref[...] = _layer_norm(DN_ALPHA * x_ref[...] + h, g_ref[...], b_ref[...])


def _proj_ln(x, a, w, g, b, *, tm=512):
    n = x.shape[0]
    vec = pl.BlockSpec((1, D_MODEL), lambda i: (0, 0))
    return pl.pallas_call(
        _proj_ln_kernel,
        out_shape=jax.ShapeDtypeStruct((n, D_MODEL), F32),
        grid=(n // tm,),
        in_specs=[pl.BlockSpec((tm, D_MODEL), lambda i: (i, 0)),
                  pl.BlockSpec((tm, D_MODEL), lambda i: (i, 0)),
                  _resident((D_MODEL, D_MODEL), lambda i: (0, 0)),
                  vec, vec],
        out_specs=pl.BlockSpec((tm, D_MODEL), lambda i: (i, 0)),
        scratch_shapes=[pltpu.VMEM((D_MODEL, D_MODEL), BF16)],
        compiler_params=_params(("arbitrary",)),
        name="out_proj_ln",
    )(x, a, w, _row(g), _row(b))


def _attn_layer(x, tabs, w_qkv, w_o, g, b):
    n = x.shape[0]
    qkvs = [_qkv_group(x, tabs, w_qkv, grp, dil) for grp, dil in enumerate(ATTN_DILATIONS)]
    return _proj_ln(x, _attention(qkvs, n), w_o, g, b)


def kernel(x, positions, l0_pool_w_in, l0_pool_w_grp, l0_pool_scale, l0_ln1_g, l0_ln1_b, l0_ffn_w_gate, l0_ffn_w_up, l0_ffn_w_down, l0_ln2_g, l0_ln2_b, l1_attn_w_qkv, l1_attn_w_o, l1_ln1_g, l1_ln1_b, l1_moe_w_router, l1_moe_w_gate, l1_moe_w_up, l1_moe_w_down, l1_ln2_g, l1_ln2_b, l2_conv_w_in, l2_conv_w, l2_conv_w_out, l2_ln1_g, l2_ln1_b, l2_ffn_w_gate, l2_ffn_w_up, l2_ffn_w_down, l2_ln2_g, l2_ln2_b, l3_pool_w_in, l3_pool_w_grp, l3_pool_scale, l3_ln1_g, l3_ln1_b, l3_moe_w_router, l3_moe_w_gate, l3_moe_w_up, l3_moe_w_down, l3_ln2_g, l3_ln2_b):
    batch, seq, d = x.shape
    h = x.reshape(batch * seq, d)
    tabs = _rope_tables(positions)
    h = _pool_layer(h, l0_pool_w_in, l0_pool_w_grp, l0_pool_scale, l0_ln1_g, l0_ln1_b)
    h = _ffn_layer(h, l0_ffn_w_gate, l0_ffn_w_up, l0_ffn_w_down, l0_ln2_g, l0_ln2_b)
    h = _attn_layer(h, tabs, l1_attn_w_qkv, l1_attn_w_o, l1_ln1_g, l1_ln1_b)
    h = _moe_layer(h, l1_moe_w_router, l1_moe_w_gate, l1_moe_w_up, l1_moe_w_down, l1_ln2_g, l1_ln2_b)
    h = _conv_layer(h, l2_conv_w_in, l2_conv_w, l2_conv_w_out, l2_ln1_g, l2_ln1_b)
    h = _ffn_layer(h, l2_ffn_w_gate, l2_ffn_w_up, l2_ffn_w_down, l2_ln2_g, l2_ln2_b)
    h = _pool_layer(h, l3_pool_w_in, l3_pool_w_grp, l3_pool_scale, l3_ln1_g, l3_ln1_b)
    h = _moe_layer(h, l3_moe_w_router, l3_moe_w_gate, l3_moe_w_up, l3_moe_w_down, l3_ln2_g, l3_ln2_b)
    return h.reshape(batch, seq, d)
```

```python
import functools

import jax
import jax.numpy as jnp
from jax import lax
from jax.experimental import pallas as pl
from jax.experimental.pallas import tpu as pltpu

D_MODEL = 1024
SEQ = 2048
DEPTH = 4
POOL_WINDOWS = (2, 4, 8, 16)
POOL_GROUP_DIM = D_MODEL // len(POOL_WINDOWS)
ATTN_DILATIONS = (1, 4, 16)
ATTN_BLOCK = 128
ATTN_UNROLL = 8
HEAD_DIM = 64
ROT_DIM = HEAD_DIM // 4
ROPE_THETA = 500000.0
LOG2_E = 1.4426950408889634
CONV_WIDTH = 3
N_EXPERTS = 8
MOE_TILE = 1024
META_E1, META_E2, META_R1, META_R2, META_G1, META_G2 = range(6)
DN_ALPHA = (2 * DEPTH) ** 0.25
LN_EPS = 1e-5

LANES = 128
HALO = 16
NEG_BIG = -1e30
VMEM_LIMIT = 56 * 1024 * 1024

F32 = jnp.float32
BF16 = jnp.bfloat16


def _params(semantics, vmem=VMEM_LIMIT):
    return pltpu.CompilerParams(dimension_semantics=semantics, vmem_limit_bytes=vmem)


def _dot(a, b):
    return jnp.dot(a, b, preferred_element_type=F32)


def _layer_norm(z, g, b):
    mu = jnp.mean(z, axis=-1, keepdims=True)
    zc = z - mu
    var = jnp.mean(zc * zc, axis=-1, keepdims=True)
    return zc * lax.rsqrt(var + LN_EPS) * g + b


def _row(v):
    return v.reshape(1, -1)


def _resident(shape, index_map):
    return pl.BlockSpec(shape, index_map, pipeline_mode=pl.Buffered(1))


def _cast_once(dst_ref, src_ref):
    @pl.when(pl.program_id(0) == 0)
    def _():
        dst_ref[...] = src_ref[...].astype(BF16)


def _pool_kernel(x_ref, w_in_f32, w_grp_f32, scale_ref, g_ref, b_ref, o_ref, halo_ref, w_in_ref, w_grp_ref, *, tm, tiles_per_seq):
    i = pl.program_id(0)
    _cast_once(w_in_ref, w_in_f32)
    _cast_once(w_grp_ref, w_grp_f32)
    x = x_ref[...]
    u = _dot(x.astype(BF16), w_in_ref[...])

    @pl.when(i % tiles_per_seq == 0)
    def _():
        halo_ref[...] = jnp.zeros_like(halo_ref)

    buf = jnp.concatenate([halo_ref[...], u], axis=0)
    halo_ref[...] = u[tm - HALO:, :]
    t = (i % tiles_per_seq) * tm + lax.broadcasted_iota(jnp.int32, (tm, 1), 0)
    outs = []
    for grp, w in enumerate(POOL_WINDOWS):
        cols = slice(grp * POOL_GROUP_DIM, (grp + 1) * POOL_GROUP_DIM)
        s = buf[:, cols]
        k = 1
        while k < w:
            s = s + pltpu.roll(s, k, 0)
            k *= 2
        cnt = jnp.minimum(t + 1, w).astype(F32)
        pooled = s[HALO:, :] / cnt - u[:, cols]
        outs.append(_dot(pooled.astype(BF16), w_grp_ref[grp]))
    h = jnp.concatenate(outs, axis=1) * scale_ref[...]
    o_ref[...] = _layer_norm(DN_ALPHA * x + h, g_ref[...], b_ref[...])


def _pool_layer(x, w_in, w_grp, scale, g, b, *, tm=512):
    n = x.shape[0]
    kern = functools.partial(_pool_kernel, tm=tm, tiles_per_seq=SEQ // tm)
    vec = pl.BlockSpec((1, D_MODEL), lambda i: (0, 0))
    return pl.pallas_call(
        kern,
        out_shape=jax.ShapeDtypeStruct((n, D_MODEL), F32),
        grid=(n // tm,),
        in_specs=[pl.BlockSpec((tm, D_MODEL), lambda i: (i, 0)),
                  _resident((D_MODEL, D_MODEL), lambda i: (0, 0)),
                  _resident((len(POOL_WINDOWS), POOL_GROUP_DIM, POOL_GROUP_DIM), lambda i: (0, 0, 0)),
                  vec, vec, vec],
        out_specs=pl.BlockSpec((tm, D_MODEL), lambda i: (i, 0)),
        scratch_shapes=[pltpu.VMEM((HALO, D_MODEL), F32), pltpu.VMEM((D_MODEL, D_MODEL), BF16),
                        pltpu.VMEM((len(POOL_WINDOWS), POOL_GROUP_DIM, POOL_GROUP_DIM), BF16)],
        compiler_params=_params(("arbitrary",)),
        name="pool_mixer_ln",
    )(x, w_in, w_grp, _row(scale), _row(g), _row(b))


def _conv_kernel(x_ref, w_in_f32, cw_ref, w_out_f32, g_ref, b_ref, o_ref, halo_ref, w_in_ref, w_out_ref, *, tm, tiles_per_seq):
    i = pl.program_id(0)
    _cast_once(w_in_ref, w_in_f32)
    _cast_once(w_out_ref, w_out_f32)
    x = x_ref[...]
    proj = _dot(x.astype(BF16), w_in_ref[...])
    gate_b = proj[:, :D_MODEL]
    z = proj[:, D_MODEL:2 * D_MODEL] * proj[:, 2 * D_MODEL:]

    @pl.when(i % tiles_per_seq == 0)
    def _():
        halo_ref[...] = jnp.zeros_like(halo_ref)

    buf = jnp.concatenate([halo_ref[...], z], axis=0)
    halo_ref[...] = z[tm - HALO:, :]
    conv = cw_ref[0:1, :] * z
    for j in range(1, CONV_WIDTH):
        conv = conv + cw_ref[j:j + 1, :] * pltpu.roll(buf, j, 0)[HALO:, :]
    h = _dot((gate_b * conv).astype(BF16), w_out_ref[...])
    o_ref[...] = _layer_norm(DN_ALPHA * x + h, g_ref[...], b_ref[...])


def _conv_layer(x, w_in, conv_w, w_out, g, b, *, tm=512):
    n = x.shape[0]
    kern = functools.partial(_conv_kernel, tm=tm, tiles_per_seq=SEQ // tm)
    vec = pl.BlockSpec((1, D_MODEL), lambda i: (0, 0))
    return pl.pallas_call(
        kern,
        out_shape=jax.ShapeDtypeStruct((n, D_MODEL), F32),
        grid=(n // tm,),
        in_specs=[pl.BlockSpec((tm, D_MODEL), lambda i: (i, 0)),
                  _resident((D_MODEL, 3 * D_MODEL), lambda i: (0, 0)),
                  pl.BlockSpec((CONV_WIDTH, D_MODEL), lambda i: (0, 0)),
                  _resident((D_MODEL, D_MODEL), lambda i: (0, 0)),
                  vec, vec],
        out_specs=pl.BlockSpec((tm, D_MODEL), lambda i: (i, 0)),
        scratch_shapes=[pltpu.VMEM((HALO, D_MODEL), F32), pltpu.VMEM((D_MODEL, 3 * D_MODEL), BF16),
                        pltpu.VMEM((D_MODEL, D_MODEL), BF16)],
        compiler_params=_params(("arbitrary",)),
        name="conv_mixer_ln",
    )(x, w_in, conv_w, w_out, _row(g), _row(b))


FFN_CHUNK = 256


def _ffn_kernel(x_ref, wg_ref, wu_ref, wd_ref, g_ref, b_ref, o_ref):
    x = x_ref[...]
    xb = x.astype(BF16)
    acc = jnp.zeros(x.shape, F32)
    for lo in range(0, wg_ref.shape[1], FFN_CHUNK):
        gate = _dot(xb, wg_ref[:, lo:lo + FFN_CHUNK].astype(BF16))
        up = _dot(xb, wu_ref[:, lo:lo + FFN_CHUNK].astype(BF16))
        h = gate * jax.nn.sigmoid(gate) * up
        acc = acc + _dot(h.astype(BF16), wd_ref[lo:lo + FFN_CHUNK, :].astype(BF16))
    o_ref[...] = _layer_norm(DN_ALPHA * x + acc, g_ref[...], b_ref[...])


def _ffn_layer(x, w_gate, w_up, w_down, g, b, *, tm=512):
    n = x.shape[0]
    d_ff = w_gate.shape[1]
    vec = pl.BlockSpec((1, D_MODEL), lambda i: (0, 0))
    once = pl.Buffered(1)
    return pl.pallas_call(
        _ffn_kernel,
        out_shape=jax.ShapeDtypeStruct((n, D_MODEL), F32),
        grid=(n // tm,),
        in_specs=[pl.BlockSpec((tm, D_MODEL), lambda i: (i, 0)),
                  pl.BlockSpec((D_MODEL, d_ff), lambda i: (0, 0), pipeline_mode=once),
                  pl.BlockSpec((D_MODEL, d_ff), lambda i: (0, 0), pipeline_mode=once),
                  pl.BlockSpec((d_ff, D_MODEL), lambda i: (0, 0), pipeline_mode=once),
                  vec, vec],
        out_specs=pl.BlockSpec((tm, D_MODEL), lambda i: (i, 0)),
        compiler_params=_params(("arbitrary",)),
        name="swiglu_ln",
    )(x, w_gate, w_up, w_down, _row(g), _row(b))


CHUNKS = D_MODEL // LANES


def _store_token_tiles(ref, y):
    for c in range(CHUNKS):
        ref[pl.ds(c, y.shape[0], stride=CHUNKS), :] = y[:, c * LANES:(c + 1) * LANES]


def _load_token_tiles(ref, rows):
    return jnp.concatenate([ref[pl.ds(c, rows, stride=CHUNKS), :] for c in range(CHUNKS)], axis=1)


def _router_kernel(x_ref, w_ref, meta_ref, cnt_ref, run_ref):
    i = pl.program_id(0)

    @pl.when(i == 0)
    def _():
        run_ref[...] = jnp.zeros_like(run_ref)

    x = x_ref[...]
    w = w_ref[...]
    xh = x.astype(BF16)
    xl = (x - xh.astype(F32)).astype(BF16)
    wh = w.astype(BF16)
    wl = (w - wh.astype(F32)).astype(BF16)
    logits = _dot(xh, wh) + (_dot(xl, wh) + _dot(xh, wl))
    tm = logits.shape[0]
    lane = lax.broadcasted_iota(jnp.int32, logits.shape, 1)
    logits = jnp.where(lane < N_EXPERTS, logits, -jnp.inf)
    v1 = jnp.max(logits, axis=1, keepdims=True)
    i1 = jnp.min(jnp.where(logits == v1, lane, LANES), axis=1, keepdims=True)
    rest = jnp.where(lane == i1, -jnp.inf, logits)
    v2 = jnp.max(rest, axis=1, keepdims=True)
    i2 = jnp.min(jnp.where(rest == v2, lane, LANES), axis=1, keepdims=True)
    e2 = jnp.exp(v2 - v1)
    g1 = 1.0 / (1.0 + e2)
    g2 = e2 / (1.0 + e2)

    sel = jnp.where(lane == i1, 1.0, jnp.where(lane == i2, 1.0, 0.0))
    before = (lax.broadcasted_iota(jnp.int32, (tm, tm), 1) < lax.broadcasted_iota(jnp.int32, (tm, tm), 0))
    rank = run_ref[0:1, :] + _dot(before.astype(BF16), sel.astype(BF16))
    r1 = jnp.sum(jnp.where(lane == i1, rank, 0.0), axis=1, keepdims=True)
    r2 = jnp.sum(jnp.where(lane == i2, rank, 0.0), axis=1, keepdims=True)
    run_ref[...] = run_ref[...] + jnp.sum(sel, axis=0, keepdims=True)
    cnt_ref[...] = run_ref[...]
    meta = jnp.zeros_like(logits)
    for k, val in enumerate((i1.astype(F32), i2.astype(F32), r1, r2, g1, g2)):
        meta = jnp.where(lane == k, val, meta)
    meta_ref[...] = meta


def _router(x, w_router, *, tm=512):
    n = x.shape[0]
    w = jnp.pad(w_router, ((0, 0), (0, LANES - N_EXPERTS)))
    return pl.pallas_call(
        _router_kernel,
        out_shape=(jax.ShapeDtypeStruct((n, LANES), F32), jax.ShapeDtypeStruct((8, LANES), F32)),
        grid=(n // tm,),
        in_specs=[pl.BlockSpec((tm, D_MODEL), lambda i: (i, 0)),
                  pl.BlockSpec((D_MODEL, LANES), lambda i: (0, 0))],
        out_specs=(pl.BlockSpec((tm, LANES), lambda i: (i, 0)), pl.BlockSpec((8, LANES), lambda i: (0, 0))),
        scratch_shapes=[pltpu.VMEM((8, LANES), F32)],
        compiler_params=_params(("arbitrary",)),
        name="router_top2",
    )(x, w)


def _routing_tables(meta, counts, tm):
    n = meta.shape[0]
    cnt = counts[0, :N_EXPERTS].astype(jnp.int32)
    padded = (cnt + MOE_TILE - 1) // MOE_TILE * MOE_TILE
    ends = jnp.cumsum(padded)
    offs = ends - padded
    sel = meta[:, :4].astype(jnp.int32)
    pos = offs[sel[:, :2]] + sel[:, 2:]
    pos = pos.reshape(n // tm, tm, 2).transpose(0, 2, 1).reshape(2 * n)
    max_tiles = (2 * n + N_EXPERTS * (MOE_TILE - 1)) // MOE_TILE
    first_row = jnp.arange(max_tiles, dtype=jnp.int32) * MOE_TILE
    tile_expert = jnp.minimum(jnp.sum(first_row[:, None] >= ends[None, :], axis=1), N_EXPERTS - 1).astype(jnp.int32)
    return pos, offs, ends, tile_expert, ends[-1:] // MOE_TILE, max_tiles


ZERO_ROWS = 128
DMA_GROUP = 16


def _token_rows(ref, start, rows):
    return ref.at[pl.ds(pl.multiple_of(start * CHUNKS, CHUNKS), rows * CHUNKS)]


def _dispatch_kernel(offs_ref, ends_ref, pos_ref, x_ref, xs_hbm, xt_ref, zero_ref, sem_zero, sem_rows, *, tm):
    i = pl.program_id(0)
    last = pl.num_programs(0) - 1

    @pl.when(i == 0)
    def _():
        zero_ref[...] = jnp.zeros_like(zero_ref)
        first_unused = ends_ref[N_EXPERTS - 1] // MOE_TILE
        n_tiles = xs_hbm.shape[0] // (MOE_TILE * CHUNKS)

        def clear_tile(start, wait):
            for k in range(MOE_TILE // ZERO_ROWS):
                clear = pltpu.make_async_copy(zero_ref, _token_rows(xs_hbm, start + k * ZERO_ROWS, ZERO_ROWS), sem_zero)
                clear.wait() if wait else clear.start()

        def clear_unused(wait, j, carry):
            clear_tile(j * MOE_TILE, wait)
            return carry

        for wait in (False, True):
            for e in range(N_EXPERTS):
                pl.when(ends_ref[e] > offs_ref[e])(functools.partial(clear_tile, ends_ref[e] - MOE_TILE, wait))
            lax.fori_loop(first_unused, n_tiles, functools.partial(clear_unused, wait), 0)

    def wait_rows(slot):
        for _ in range(2):
            pltpu.make_async_copy(xt_ref.at[slot], _token_rows(xs_hbm, 0, tm), sem_rows.at[slot]).wait()

    def step(slot):
        xt = xt_ref.at[slot]
        _store_token_tiles(xt, x_ref[...])
        base = i * (2 * tm)

        def send(grp, carry):
            ts = [grp * DMA_GROUP + j for j in range(DMA_GROUP)]
            dst = [(pos_ref[base + t], pos_ref[base + tm + t]) for t in ts]
            for t, (p1, p2) in zip(ts, dst):
                src = _token_rows(xt, t, 1)
                pltpu.make_async_copy(src, _token_rows(xs_hbm, p1, 1), sem_rows.at[slot]).start(priority=0)
                pltpu.make_async_copy(src, _token_rows(xs_hbm, p2, 1), sem_rows.at[slot]).start(priority=1)
            return carry
        lax.fori_loop(0, tm // DMA_GROUP, send, 0)

        @pl.when(i > 0)
        def _():
            wait_rows(1 - slot)

        @pl.when(i == last)
        def _():
            wait_rows(slot)

    for slot in range(2):
        pl.when(i % 2 == slot)(functools.partial(step, slot))


def _dispatch(x, pos, offs, ends, max_tiles, *, tm):
    n = x.shape[0]
    kern = functools.partial(_dispatch_kernel, tm=tm)
    return pl.pallas_call(
        kern,
        out_shape=jax.ShapeDtypeStruct((max_tiles * MOE_TILE * CHUNKS, LANES), F32),
        grid_spec=pltpu.PrefetchScalarGridSpec(
            num_scalar_prefetch=3,
            grid=(n // tm,),
            in_specs=[pl.BlockSpec((tm, D_MODEL), lambda i, offs, ends, pos: (i, 0))],
            out_specs=pl.BlockSpec(memory_space=pl.ANY),
            scratch_shapes=[pltpu.VMEM((2, tm * CHUNKS, LANES), F32), pltpu.VMEM((ZERO_ROWS * CHUNKS, LANES), F32),
                            pltpu.SemaphoreType.DMA, pltpu.SemaphoreType.DMA((2,))]),
        compiler_params=_params(("arbitrary",)),
        name="moe_dispatch",
    )(offs, ends, pos, x)


def _expert_kernel(te_ref, nu_ref, x_ref, wg_ref, wu_ref, wd_ref, o_ref, xb_ref, acc_ref):
    i = pl.program_id(0)
    f = pl.program_id(1)
    last_f = pl.num_programs(1) - 1

    def step(first, last):
        if first:
            xb = _load_token_tiles(x_ref, MOE_TILE).astype(BF16)
            xb_ref[...] = xb
        else:
            xb = xb_ref[...]
        gate = _dot(xb, wg_ref[...].astype(BF16))
        up = _dot(xb, wu_ref[...].astype(BF16))
        h = gate * jax.nn.sigmoid(gate) * up
        part = _dot(h.astype(BF16), wd_ref[...].astype(BF16))
        acc = part if first else acc_ref[...] + part
        if last:
            _store_token_tiles(o_ref, acc)
        else:
            acc_ref[...] = acc

    @pl.when(i < nu_ref[0])
    def _():
        pl.when(f == 0)(functools.partial(step, True, False))
        pl.when((f > 0) & (f < last_f))(functools.partial(step, False, False))
        pl.when(f == last_f)(functools.partial(step, False, True))

    @pl.when((i >= nu_ref[0]) & (f == 0))
    def _():
        o_ref[...] = jnp.zeros_like(o_ref)


def _experts(xs, tile_expert, n_used, w_gate, w_up, w_down, max_tiles, *, tf=512):
    d_ff = w_gate.shape[2]
    n_f = d_ff // tf

    def used_tile(i, nu):
        return jnp.minimum(i, jnp.maximum(nu[0] - 1, 0))

    def row_map(i, f, te, nu):
        return used_tile(i, nu), 0

    def out_map(i, f, te, nu):
        return i, 0

    def up_map(i, f, te, nu):
        return te[used_tile(i, nu)], 0, jnp.where(i < nu[0], f, n_f - 1)

    def down_map(i, f, te, nu):
        return te[used_tile(i, nu)], jnp.where(i < nu[0], f, n_f - 1), 0

    return pl.pallas_call(
        _expert_kernel,
        out_shape=jax.ShapeDtypeStruct((max_tiles * MOE_TILE * CHUNKS, LANES), F32),
        grid_spec=pltpu.PrefetchScalarGridSpec(
            num_scalar_prefetch=2,
            grid=(max_tiles, n_f),
            in_specs=[pl.BlockSpec((MOE_TILE * CHUNKS, LANES), row_map),
                      pl.BlockSpec((None, D_MODEL, tf), up_map),
                      pl.BlockSpec((None, D_MODEL, tf), up_map),
                      pl.BlockSpec((None, tf, D_MODEL), down_map)],
            out_specs=pl.BlockSpec((MOE_TILE * CHUNKS, LANES), out_map),
            scratch_shapes=[pltpu.VMEM((MOE_TILE, D_MODEL), BF16), pltpu.VMEM((MOE_TILE, D_MODEL), F32)]),
        compiler_params=_params(("arbitrary", "arbitrary")),
        name="moe_experts",
    )(tile_expert, n_used, xs, w_gate, w_up, w_down)


def _combine_kernel(pos_ref, x_ref, meta_ref, y_hbm, g_ref, b_ref, o_ref, buf_ref, sem_rows, *, tm):
    i = pl.program_id(0)
    n_steps = pl.num_programs(0)

    def fetch_rows(tile, slot):
        base = tile * (2 * tm)

        def fetch(grp, carry):
            ts = [grp * DMA_GROUP + j for j in range(DMA_GROUP)]
            src = [(pos_ref[base + t], pos_ref[base + tm + t]) for t in ts]
            for t, (p1, p2) in zip(ts, src):
                pltpu.make_async_copy(_token_rows(y_hbm, p1, 1), _token_rows(buf_ref.at[slot, 0], t, 1),
                                      sem_rows.at[slot]).start(priority=0)
                pltpu.make_async_copy(_token_rows(y_hbm, p2, 1), _token_rows(buf_ref.at[slot, 1], t, 1),
                                      sem_rows.at[slot]).start(priority=1)
            return carry
        lax.fori_loop(0, tm // DMA_GROUP, fetch, 0)

    @pl.when(i == 0)
    def _():
        fetch_rows(0, 0)

    def step(slot):
        @pl.when(i + 1 < n_steps)
        def _():
            fetch_rows(i + 1, 1 - slot)

        for k in range(2):
            pltpu.make_async_copy(_token_rows(y_hbm, 0, tm), buf_ref.at[slot, k], sem_rows.at[slot]).wait()
        meta = meta_ref[...]
        lane = lax.broadcasted_iota(jnp.int32, meta.shape, 1)
        g1 = jnp.sum(jnp.where(lane == META_G1, meta, 0.0), axis=1, keepdims=True)
        g2 = jnp.sum(jnp.where(lane == META_G2, meta, 0.0), axis=1, keepdims=True)
        mix = g1 * _load_token_tiles(buf_ref.at[slot, 0], tm) + g2 * _load_token_tiles(buf_ref.at[slot, 1], tm)
        o_ref[...] = _layer_norm(DN_ALPHA * x_ref[...] + mix, g_ref[...], b_ref[...])

    for slot in range(2):
        pl.when(i % 2 == slot)(functools.partial(step, slot))


def _combine(x, meta, y, pos, g, b, *, tm):
    n = x.shape[0]
    kern = functools.partial(_combine_kernel, tm=tm)
    vec = pl.BlockSpec((1, D_MODEL), lambda i, pos: (0, 0))
    return pl.pallas_call(
        kern,
        out_shape=jax.ShapeDtypeStruct((n, D_MODEL), F32),
        grid_spec=pltpu.PrefetchScalarGridSpec(
            num_scalar_prefetch=1,
            grid=(n // tm,),
            in_specs=[pl.BlockSpec((tm, D_MODEL), lambda i, pos: (i, 0)),
                      pl.BlockSpec((tm, LANES), lambda i, pos: (i, 0)),
                      pl.BlockSpec(memory_space=pl.ANY),
                      vec, vec],
            out_specs=pl.BlockSpec((tm, D_MODEL), lambda i, pos: (i, 0)),
            scratch_shapes=[pltpu.VMEM((2, 2, tm * CHUNKS, LANES), F32), pltpu.SemaphoreType.DMA((2,))]),
        compiler_params=_params(("arbitrary",)),
        name="moe_combine_ln",
    )(pos, x, meta, y, _row(g), _row(b))


def _moe_layer(x, w_router, w_gate, w_up, w_down, g, b, *, tm=512):
    meta, counts = _router(x, w_router)
    pos, offs, ends, tile_expert, n_used, max_tiles = _routing_tables(meta, counts, tm)
    xs = _dispatch(x, pos, offs, ends, max_tiles, tm=tm)
    y = _experts(xs, tile_expert, n_used, w_gate, w_up, w_down, max_tiles)
    return _combine(x, meta, y, pos, g, b, tm=tm)


def _rope_kernel(pos_ref, invf_ref, c_ref, s1_ref, s2_ref):
    ang = pos_ref[...].astype(F32) * invf_ref[...]
    c = jnp.cos(ang)
    s = jnp.sin(ang)
    dd = lax.broadcasted_iota(jnp.int32, ang.shape, 1) % HEAD_DIM
    c_ref[...] = c
    s1_ref[...] = jnp.where(dd < ROT_DIM // 2, -s, 0.0)
    s2_ref[...] = jnp.where((dd >= ROT_DIM // 2) & (dd < ROT_DIM), s, 0.0)


def _rope_tables(positions, *, tm=2048):
    n = positions.size
    half = ROT_DIM // 2
    inv_freq = ROPE_THETA ** (-(jnp.arange(0, ROT_DIM, 2, dtype=F32) / ROT_DIM))
    per_head = jnp.concatenate([inv_freq, inv_freq, jnp.zeros((HEAD_DIM - 2 * half,), F32)])
    invf = jnp.tile(per_head, LANES // HEAD_DIM).reshape(1, LANES)
    out = jax.ShapeDtypeStruct((n, LANES), F32)
    spec = pl.BlockSpec((tm, LANES), lambda i: (i, 0))
    return pl.pallas_call(
        _rope_kernel,
        out_shape=(out, out, out),
        grid=(n // tm,),
        in_specs=[pl.BlockSpec((tm, 1), lambda i: (i, 0)), pl.BlockSpec((1, LANES), lambda i: (0, 0))],
        out_specs=(spec, spec, spec),
        compiler_params=_params(("arbitrary",)),
        name="rope_tables",
    )(positions.reshape(n, 1), invf)


QKV_TILE = 512


def _qkv_kernel(x_ref, c_ref, s1_ref, s2_ref, w_f32, o_ref, w_ref, *, dil):
    _cast_once(w_ref, w_f32)
    tm = x_ref.shape[0]
    chunk = tm // dil
    xb = x_ref[...].astype(BF16)
    if dil == 1:
        c, s1, s2 = c_ref[...], s1_ref[...], s2_ref[...]
    else:
        dst = lax.broadcasted_iota(jnp.int32, (tm, tm), 0)
        tok = lax.broadcasted_iota(jnp.int32, (tm, tm), 1)
        perm = (tok == (dst % chunk) * dil + dst // chunk).astype(BF16)
        xb = _dot(perm, xb).astype(BF16)
        c, s1, s2 = (jnp.concatenate([t[pl.ds(r, chunk, stride=dil), :] for r in range(dil)], axis=0)
                     for t in (c_ref, s1_ref, s2_ref))
    y = _dot(xb, w_ref[...])
    for part in range(2):
        scale = HEAD_DIM ** -0.5 * LOG2_E if part == 0 else 1.0
        for blk in range(D_MODEL // LANES):
            lo = part * D_MODEL + blk * LANES
            t = y[:, lo:lo + LANES]
            rot = t * c + pltpu.roll(t, LANES - ROT_DIM // 2, 1) * s1 + pltpu.roll(t, ROT_DIM // 2, 1) * s2
            o_ref[:, lo:lo + LANES] = (rot * scale).astype(BF16)
    o_ref[:, 2 * D_MODEL:] = y[:, 2 * D_MODEL:].astype(BF16)


def _qkv_group(x, tabs, w_qkv, grp, dil):
    n = x.shape[0]
    tm = QKV_TILE
    kern = functools.partial(_qkv_kernel, dil=dil)
    tab_spec = pl.BlockSpec((tm, LANES), lambda i: (i, 0))
    return pl.pallas_call(
        kern,
        out_shape=jax.ShapeDtypeStruct((n, 3 * D_MODEL), BF16),
        grid=(n // tm,),
        in_specs=[pl.BlockSpec((tm, D_MODEL), lambda i: (i, 0)),
                  tab_spec, tab_spec, tab_spec,
                  _resident((D_MODEL, 3 * D_MODEL), lambda i: (0, grp))],
        out_specs=pl.BlockSpec((tm, 3 * D_MODEL), lambda i: (i, 0)),
        scratch_shapes=[pltpu.VMEM((D_MODEL, 3 * D_MODEL), BF16)],
        compiler_params=_params(("arbitrary",)),
        name="qkv_proj_dil%d" % dil,
    )(x, *tabs, w_qkv)


def _attn_kernel(*refs):
    qkv = refs[:9]
    o_ref = refs[9]
    acc_s, m_s, l_s = refs[10:13], refs[13:16], refs[16:19]

    lane = lax.broadcasted_iota(jnp.int32, (1, LANES), 1)
    head0 = lane < HEAD_DIM
    hm0 = head0.astype(BF16)
    hm1 = 1.0 - hm0
    qi = lax.broadcasted_iota(jnp.int32, (ATTN_BLOCK, ATTN_BLOCK), 0)
    kj = lax.broadcasted_iota(jnp.int32, (ATTN_BLOCK, ATTN_BLOCK), 1)
    cur_mask = jnp.where(qi <= kj, 0.0, NEG_BIG).astype(BF16)
    prev_mask = jnp.where(qi >= kj, 0.0, NEG_BIG).astype(BF16)
    mask_both = jnp.concatenate([prev_mask, cur_mask], axis=0)
    row_onehot = (qi == kj).astype(BF16)
    row_onehot = jnp.concatenate([row_onehot, row_onehot], axis=0)

    def block_rows(ref, dil, r, nb):
        chunk = QKV_TILE // dil
        if chunk >= ATTN_BLOCK:
            first = nb * ATTN_BLOCK
            base = first // chunk * QKV_TILE + r * chunk + first % chunk
            return ref[base:base + ATTN_BLOCK, :]
        pieces = ATTN_BLOCK // chunk
        starts = [(nb * pieces + m) * QKV_TILE + r * chunk for m in range(pieces)]
        return jnp.concatenate([ref[s:s + chunk, :] for s in starts], axis=0)

    def scores(grp, dil, r, nb):
        q_ref, k_ref, v_ref = qkv[3 * grp:3 * grp + 3]
        q = block_rows(q_ref, dil, r, nb)
        q2 = jnp.concatenate([q * hm0, q * hm1], axis=0)
        q2 = jnp.concatenate([q2, row_onehot], axis=1)
        kk = block_rows(k_ref, dil, r, nb)
        vv = block_rows(v_ref, dil, r, nb)
        if nb > 0:
            kk = jnp.concatenate([block_rows(k_ref, dil, r, nb - 1), kk], axis=0)
            vv = jnp.concatenate([block_rows(v_ref, dil, r, nb - 1), vv], axis=0)
            kk = jnp.concatenate([kk, mask_both], axis=1)
        else:
            kk = jnp.concatenate([kk, cur_mask], axis=1)
        s = lax.dot_general(q2, kk, (((1,), (1,)), ((), ())), preferred_element_type=F32)
        return s, vv

    def finish(grp, s, vv, nat_start, dil):
        m = jnp.max(s, axis=1, keepdims=True)
        pb = jnp.exp2(s - m).astype(BF16)
        ones = jnp.ones_like(vv)
        o0 = _dot(pb[:ATTN_BLOCK], jnp.concatenate([vv * hm0, ones], axis=1))
        o1 = _dot(pb[ATTN_BLOCK:], jnp.concatenate([vv * hm1, ones], axis=1))
        acc = o0[:, :LANES] + o1[:, :LANES]
        mb = jnp.where(head0, m[:ATTN_BLOCK], m[ATTN_BLOCK:])
        lb = jnp.where(head0, o0[:, LANES:], o1[:, LANES:])
        if dil == 1:
            rows = pl.ds(nat_start, ATTN_BLOCK)
        else:
            rows = pl.ds(nat_start, ATTN_BLOCK, stride=dil)
        acc_s[grp][rows, :] = acc
        m_s[grp][rows, :] = mb
        l_s[grp][rows, :] = lb

    for grp, dil in enumerate(ATTN_DILATIONS):
        n_blocks = SEQ // dil // ATTN_BLOCK
        blocks = [(r, 0) for r in range(dil)] + [(r, nb) for r in range(dil) for nb in range(1, n_blocks)]
        for lo in range(0, len(blocks), ATTN_UNROLL):
            batch = blocks[lo:lo + ATTN_UNROLL]
            staged = [scores(grp, dil, r, nb) for r, nb in batch]
            for (r, nb), (s, vv) in zip(batch, staged):
                finish(grp, s, vv, nb * ATTN_BLOCK * dil + r, dil)

    def merge(c, carry):
        rows = pl.ds(pl.multiple_of(c * ATTN_BLOCK, ATTN_BLOCK), ATTN_BLOCK)
        ms = [m_s[g][rows, :] for g in range(3)]
        top = jnp.maximum(jnp.maximum(ms[0], ms[1]), ms[2])
        num = jnp.zeros((ATTN_BLOCK, LANES), F32)
        den = jnp.zeros((ATTN_BLOCK, LANES), F32)
        for g in range(3):
            w = jnp.exp2(ms[g] - top)
            num = num + w * acc_s[g][rows, :]
            den = den + w * l_s[g][rows, :]
        o_ref[rows, :] = (num / den).astype(BF16)
        return carry
    lax.fori_loop(0, SEQ // ATTN_BLOCK, merge, 0)


def _attention(qkvs, n):
    n_pairs = D_MODEL // LANES
    in_specs, args = [], []
    for qkv in qkvs:
        for part in range(3):
            in_specs.append(pl.BlockSpec((SEQ, LANES), lambda b, hp, part=part: (b, part * n_pairs + hp)))
            args.append(qkv)
    scratch = [pltpu.VMEM((SEQ, LANES), F32) for _ in range(9)]
    return pl.pallas_call(
        _attn_kernel,
        out_shape=jax.ShapeDtypeStruct((n, D_MODEL), BF16),
        grid=(n // SEQ, n_pairs),
        in_specs=in_specs,
        out_specs=pl.BlockSpec((SEQ, LANES), lambda b, hp: (b, hp)),
        scratch_shapes=scratch,
        compiler_params=_params(("arbitrary", "arbitrary")),
        name="dilated_attention",
    )(*args)


def _proj_ln_kernel(x_ref, a_ref, w_f32, g_ref, b_ref, o_ref, w_ref):
    _cast_once(w_ref, w_f32)
    h = _dot(a_ref[...], w_ref[...])
    o_ref[...] = _layer_norm(DN_ALPHA * x_ref[...] + h, g_ref[...], b_ref[...])


def _proj_ln(x, a, w, g, b, *, tm=512):
    n = x.shape[0]
    vec = pl.BlockSpec((1, D_MODEL), lambda i: (0, 0))
    return pl.pallas_call(
        _proj_ln_kernel,
        out_shape=jax.ShapeDtypeStruct((n, D_MODEL), F32),
        grid=(n // tm,),
        in_specs=[pl.BlockSpec((tm, D_MODEL), lambda i: (i, 0)),
                  pl.BlockSpec((tm, D_MODEL), lambda i: (i, 0)),
                  _resident((D_MODEL, D_MODEL), lambda i: (0, 0)),
                  vec, vec],
        out_specs=pl.BlockSpec((tm, D_MODEL), lambda i: (i, 0)),
        scratch_shapes=[pltpu.VMEM((D_MODEL, D_MODEL), BF16)],
        compiler_params=_params(("arbitrary",)),
        name="out_proj_ln",
    )(x, a, w, _row(g), _row(b))


def _attn_layer(x, tabs, w_qkv, w_o, g, b):
    n = x.shape[0]
    qkvs = [_qkv_group(x, tabs, w_qkv, grp, dil) for grp, dil in enumerate(ATTN_DILATIONS)]
    return _proj_ln(x, _attention(qkvs, n), w_o, g, b)


def kernel(x, positions, l0_pool_w_in, l0_pool_w_grp, l0_pool_scale, l0_ln1_g, l0_ln1_b, l0_ffn_w_gate, l0_ffn_w_up, l0_ffn_w_down, l0_ln2_g, l0_ln2_b, l1_attn_w_qkv, l1_attn_w_o, l1_ln1_g, l1_ln1_b, l1_moe_w_router, l1_moe_w_gate, l1_moe_w_up, l1_moe_w_down, l1_ln2_g, l1_ln2_b, l2_conv_w_in, l2_conv_w, l2_conv_w_out, l2_ln1_g, l2_ln1_b, l2_ffn_w_gate, l2_ffn_w_up, l2_ffn_w_down, l2_ln2_g, l2_ln2_b, l3_pool_w_in, l3_pool_w_grp, l3_pool_scale, l3_ln1_g, l3_ln1_b, l3_moe_w_router, l3_moe_w_gate, l3_moe_w_up, l3_moe_w_down, l3_ln2_g, l3_ln2_b):
    batch, seq, d = x.shape
    h = x.reshape(batch * seq, d)
    tabs = _rope_tables(positions)
    h = _pool_layer(h, l0_pool_w_in, l0_pool_w_grp, l0_pool_scale, l0_ln1_g, l0_ln1_b)
    h = _ffn_layer(h, l0_ffn_w_gate, l0_ffn_w_up, l0_ffn_w_down, l0_ln2_g, l0_ln2_b)
    h = _attn_layer(h, tabs, l1_attn_w_qkv, l1_attn_w_o, l1_ln1_g, l1_ln1_b)
    h = _moe_layer(h, l1_moe_w_router, l1_moe_w_gate, l1_moe_w_up, l1_moe_w_down, l1_ln2_g, l1_ln2_b)
    h = _conv_layer(h, l2_conv_w_in, l2_conv_w, l2_conv_w_out, l2_ln1_g, l2_ln1_b)
    h = _ffn_layer(h, l2_ffn_w_gate, l2_ffn_w_up, l2_ffn_w_down, l2_ln2_g, l2_ln2_b)
    h = _pool_layer(h, l3_pool_w_in, l3_pool_w_grp, l3_pool_scale, l3_ln1_g, l3_ln1_b)
    h = _moe_layer(h, l3_moe_w_router, l3_moe_w_gate, l3_moe_w_up, l3_moe_w_down, l3_ln2_g, l3_ln2_b)
    return h.reshape(batch, seq, d)
```

```python
import functools

import jax
import jax.numpy as jnp
from jax import lax
from jax.experimental import pallas as pl
from jax.experimental.pallas import tpu as pltpu

D_MODEL = 1024
SEQ = 2048
DEPTH = 4
POOL_WINDOWS = (2, 4, 8, 16)
POOL_GROUP_DIM = D_MODEL // len(POOL_WINDOWS)
ATTN_DILATIONS = (1, 4, 16)
ATTN_BLOCK = 128
ATTN_UNROLL = 8
HEAD_DIM = 64
ROT_DIM = HEAD_DIM // 4
ROPE_THETA = 500000.0
LOG2_E = 1.4426950408889634
CONV_WIDTH = 3
N_EXPERTS = 8
MOE_TILE = 1024
META_E1, META_E2, META_R1, META_R2, META_G1, META_G2 = range(6)
DN_ALPHA = (2 * DEPTH) ** 0.25
LN_EPS = 1e-5

LANES = 128
SUBLANES = 8
HALO = 16
NEG_BIG = -1e30
VMEM_LIMIT = 56 * 1024 * 1024

F32 = jnp.float32
BF16 = jnp.bfloat16


def _params(semantics, vmem=VMEM_LIMIT):
    return pltpu.CompilerParams(dimension_semantics=semantics, vmem_limit_bytes=vmem)


def _dot(a, b):
    return jnp.dot(a, b, preferred_element_type=F32)


def _layer_norm(z, g, b):
    mu = jnp.mean(z, axis=-1, keepdims=True)
    zc = z - mu
    var = jnp.mean(zc * zc, axis=-1, keepdims=True)
    return zc * lax.rsqrt(var + LN_EPS) * g + b


def _row(v):
    return v.reshape(1, -1)


def _resident(shape, index_map):
    return pl.BlockSpec(shape, index_map, pipeline_mode=pl.Buffered(1))


def _cast_once(dst_ref, src_ref):
    @pl.when(pl.program_id(0) == 0)
    def _():
        dst_ref[...] = src_ref[...].astype(BF16)


def _pool_kernel(x_ref, w_in_f32, w_grp_f32, scale_ref, g_ref, b_ref, o_ref, halo_ref, w_in_ref, w_grp_ref, *, tm, tiles_per_seq):
    i = pl.program_id(0)
    _cast_once(w_in_ref, w_in_f32)
    _cast_once(w_grp_ref, w_grp_f32)
    x = x_ref[...]
    u = _dot(x.astype(BF16), w_in_ref[...])

    @pl.when(i % tiles_per_seq == 0)
    def _():
        halo_ref[...] = jnp.zeros_like(halo_ref)

    buf = jnp.concatenate([halo_ref[...], u], axis=0)
    halo_ref[...] = u[tm - HALO:, :]
    t = (i % tiles_per_seq) * tm + lax.broadcasted_iota(jnp.int32, (tm, 1), 0)
    outs = []
    for grp, w in enumerate(POOL_WINDOWS):
        cols = slice(grp * POOL_GROUP_DIM, (grp + 1) * POOL_GROUP_DIM)
        s = buf[:, cols]
        k = 1
        while k < w:
            s = s + pltpu.roll(s, k, 0)
            k *= 2
        cnt = jnp.minimum(t + 1, w).astype(F32)
        pooled = s[HALO:, :] / cnt - u[:, cols]
        outs.append(_dot(pooled.astype(BF16), w_grp_ref[grp]))
    h = jnp.concatenate(outs, axis=1) * scale_ref[...]
    o_ref[...] = _layer_norm(DN_ALPHA * x + h, g_ref[...], b_ref[...])


def _pool_layer(x, w_in, w_grp, scale, g, b, *, tm=512):
    n = x.shape[0]
    kern = functools.partial(_pool_kernel, tm=tm, tiles_per_seq=SEQ // tm)
    vec = pl.BlockSpec((1, D_MODEL), lambda i: (0, 0))
    return pl.pallas_call(
        kern,
        out_shape=jax.ShapeDtypeStruct((n, D_MODEL), F32),
        grid=(n // tm,),
        in_specs=[pl.BlockSpec((tm, D_MODEL), lambda i: (i, 0)),
                  _resident((D_MODEL, D_MODEL), lambda i: (0, 0)),
                  _resident((len(POOL_WINDOWS), POOL_GROUP_DIM, POOL_GROUP_DIM), lambda i: (0, 0, 0)),
                  vec, vec, vec],
        out_specs=pl.BlockSpec((tm, D_MODEL), lambda i: (i, 0)),
        scratch_shapes=[pltpu.VMEM((HALO, D_MODEL), F32), pltpu.VMEM((D_MODEL, D_MODEL), BF16),
                        pltpu.VMEM((len(POOL_WINDOWS), POOL_GROUP_DIM, POOL_GROUP_DIM), BF16)],
        compiler_params=_params(("arbitrary",)),
        name="pool_mixer_ln",
    )(x, w_in, w_grp, _row(scale), _row(g), _row(b))


def _conv_kernel(x_ref, w_in_f32, cw_ref, w_out_f32, g_ref, b_ref, o_ref, halo_ref, w_in_ref, w_out_ref, *, tm, tiles_per_seq):
    i = pl.program_id(0)
    _cast_once(w_in_ref, w_in_f32)
    _cast_once(w_out_ref, w_out_f32)
    x = x_ref[...]
    proj = _dot(x.astype(BF16), w_in_ref[...])
    gate_b = proj[:, :D_MODEL]
    z = proj[:, D_MODEL:2 * D_MODEL] * proj[:, 2 * D_MODEL:]

    @pl.when(i % tiles_per_seq == 0)
    def _():
        halo_ref[...] = jnp.zeros_like(halo_ref)

    buf = jnp.concatenate([halo_ref[...], z], axis=0)
    halo_ref[...] = z[tm - HALO:, :]
    conv = cw_ref[0:1, :] * z
    for j in range(1, CONV_WIDTH):
        conv = conv + cw_ref[j:j + 1, :] * pltpu.roll(buf, j, 0)[HALO:, :]
    h = _dot((gate_b * conv).astype(BF16), w_out_ref[...])
    o_ref[...] = _layer_norm(DN_ALPHA * x + h, g_ref[...], b_ref[...])


def _conv_layer(x, w_in, conv_w, w_out, g, b, *, tm=512):
    n = x.shape[0]
    kern = functools.partial(_conv_kernel, tm=tm, tiles_per_seq=SEQ // tm)
    vec = pl.BlockSpec((1, D_MODEL), lambda i: (0, 0))
    return pl.pallas_call(
        kern,
        out_shape=jax.ShapeDtypeStruct((n, D_MODEL), F32),
        grid=(n // tm,),
        in_specs=[pl.BlockSpec((tm, D_MODEL), lambda i: (i, 0)),
                  _resident((D_MODEL, 3 * D_MODEL), lambda i: (0, 0)),
                  pl.BlockSpec((CONV_WIDTH, D_MODEL), lambda i: (0, 0)),
                  _resident((D_MODEL, D_MODEL), lambda i: (0, 0)),
                  vec, vec],
        out_specs=pl.BlockSpec((tm, D_MODEL), lambda i: (i, 0)),
        scratch_shapes=[pltpu.VMEM((HALO, D_MODEL), F32), pltpu.VMEM((D_MODEL, 3 * D_MODEL), BF16),
                        pltpu.VMEM((D_MODEL, D_MODEL), BF16)],
        compiler_params=_params(("arbitrary",)),
        name="conv_mixer_ln",
    )(x, w_in, conv_w, w_out, _row(g), _row(b))


FFN_CHUNK = 256


def _ffn_kernel(x_ref, wg_ref, wu_ref, wd_ref, g_ref, b_ref, o_ref):
    x = x_ref[...]
    xb = x.astype(BF16)
    acc = jnp.zeros(x.shape, F32)
    for lo in range(0, wg_ref.shape[1], FFN_CHUNK):
        gate = _dot(xb, wg_ref[:, lo:lo + FFN_CHUNK].astype(BF16))
        up = _dot(xb, wu_ref[:, lo:lo + FFN_CHUNK].astype(BF16))
        h = gate * jax.nn.sigmoid(gate) * up
        acc = acc + _dot(h.astype(BF16), wd_ref[lo:lo + FFN_CHUNK, :].astype(BF16))
    o_ref[...] = _layer_norm(DN_ALPHA * x + acc, g_ref[...], b_ref[...])


def _ffn_layer(x, w_gate, w_up, w_down, g, b, *, tm=512):
    n = x.shape[0]
    d_ff = w_gate.shape[1]
    vec = pl.BlockSpec((1, D_MODEL), lambda i: (0, 0))
    once = pl.Buffered(1)
    return pl.pallas_call(
        _ffn_kernel,
        out_shape=jax.ShapeDtypeStruct((n, D_MODEL), F32),
        grid=(n // tm,),
        in_specs=[pl.BlockSpec((tm, D_MODEL), lambda i: (i, 0)),
                  pl.BlockSpec((D_MODEL, d_ff), lambda i: (0, 0), pipeline_mode=once),
                  pl.BlockSpec((D_MODEL, d_ff), lambda i: (0, 0), pipeline_mode=once),
                  pl.BlockSpec((d_ff, D_MODEL), lambda i: (0, 0), pipeline_mode=once),
                  vec, vec],
        out_specs=pl.BlockSpec((tm, D_MODEL), lambda i: (i, 0)),
        compiler_params=_params(("arbitrary",)),
        name="swiglu_ln",
    )(x, w_gate, w_up, w_down, _row(g), _row(b))


CHUNKS = D_MODEL // LANES


def _store_token_tiles(ref, y):
    for c in range(CHUNKS):
        ref[pl.ds(c, y.shape[0], stride=CHUNKS), :] = y[:, c * LANES:(c + 1) * LANES]


def _load_token_tiles(ref, rows):
    return jnp.concatenate([ref[pl.ds(c, rows, stride=CHUNKS), :] for c in range(CHUNKS)], axis=1)


def _router_kernel(x_ref, w_ref, meta_ref, cnt_ref, run_ref):
    i = pl.program_id(0)

    @pl.when(i == 0)
    def _():
        run_ref[...] = jnp.zeros_like(run_ref)

    x = x_ref[...]
    w = w_ref[...]
    xh = x.astype(BF16)
    xl = (x - xh.astype(F32)).astype(BF16)
    wh = w.astype(BF16)
    wl = (w - wh.astype(F32)).astype(BF16)
    logits = _dot(xh, wh) + (_dot(xl, wh) + _dot(xh, wl))
    tm = logits.shape[0]
    lane = lax.broadcasted_iota(jnp.int32, logits.shape, 1)
    logits = jnp.where(lane < N_EXPERTS, logits, -jnp.inf)
    v1 = jnp.max(logits, axis=1, keepdims=True)
    i1 = jnp.min(jnp.where(logits == v1, lane, LANES), axis=1, keepdims=True)
    rest = jnp.where(lane == i1, -jnp.inf, logits)
    v2 = jnp.max(rest, axis=1, keepdims=True)
    i2 = jnp.min(jnp.where(rest == v2, lane, LANES), axis=1, keepdims=True)
    e2 = jnp.exp(v2 - v1)
    g1 = 1.0 / (1.0 + e2)
    g2 = e2 / (1.0 + e2)

    sel = jnp.where(lane == i1, 1.0, jnp.where(lane == i2, 1.0, 0.0))
    before = (lax.broadcasted_iota(jnp.int32, (tm, tm), 1) < lax.broadcasted_iota(jnp.int32, (tm, tm), 0))
    rank = run_ref[0:1, :] + _dot(before.astype(BF16), sel.astype(BF16))
    r1 = jnp.sum(jnp.where(lane == i1, rank, 0.0), axis=1, keepdims=True)
    r2 = jnp.sum(jnp.where(lane == i2, rank, 0.0), axis=1, keepdims=True)
    run_ref[...] = run_ref[...] + jnp.sum(sel, axis=0, keepdims=True)
    cnt_ref[...] = run_ref[...]
    meta = jnp.zeros_like(logits)
    record = {META_E1: i1.astype(F32), META_E2: i2.astype(F32), META_R1: r1, META_R2: r2, META_G1: g1, META_G2: g2}
    for k, val in record.items():
        meta = jnp.where(lane == k, val, meta)
    meta_ref[...] = meta


def _router(x, w_router, *, tm=512):
    n = x.shape[0]
    w = jnp.pad(w_router, ((0, 0), (0, LANES - N_EXPERTS)))
    return pl.pallas_call(
        _router_kernel,
        out_shape=(jax.ShapeDtypeStruct((n, LANES), F32), jax.ShapeDtypeStruct((SUBLANES, LANES), F32)),
        grid=(n // tm,),
        in_specs=[pl.BlockSpec((tm, D_MODEL), lambda i: (i, 0)),
                  pl.BlockSpec((D_MODEL, LANES), lambda i: (0, 0))],
        out_specs=(pl.BlockSpec((tm, LANES), lambda i: (i, 0)), pl.BlockSpec((SUBLANES, LANES), lambda i: (0, 0))),
        scratch_shapes=[pltpu.VMEM((SUBLANES, LANES), F32)],
        compiler_params=_params(("arbitrary",)),
        name="router_top2",
    )(x, w)


def _routing_tables(meta, counts, tm):
    n = meta.shape[0]
    cnt = counts[0, :N_EXPERTS].astype(jnp.int32)
    padded = (cnt + MOE_TILE - 1) // MOE_TILE * MOE_TILE
    ends = jnp.cumsum(padded)
    offs = ends - padded
    experts = meta[:, META_E1:META_E2 + 1].astype(jnp.int32)
    ranks = meta[:, META_R1:META_R2 + 1].astype(jnp.int32)
    pos = offs[experts] + ranks
    pos = pos.reshape(n // tm, tm, 2).transpose(0, 2, 1).reshape(2 * n)
    max_tiles = (2 * n + N_EXPERTS * (MOE_TILE - 1)) // MOE_TILE
    first_row = jnp.arange(max_tiles, dtype=jnp.int32) * MOE_TILE
    tile_expert = jnp.minimum(jnp.sum(first_row[:, None] >= ends[None, :], axis=1), N_EXPERTS - 1).astype(jnp.int32)
    return pos, offs, ends, tile_expert, ends[-1:] // MOE_TILE, max_tiles


ZERO_ROWS = 128
DMA_GROUP = 16


def _token_rows(ref, start, rows):
    return ref.at[pl.ds(pl.multiple_of(start * CHUNKS, CHUNKS), rows * CHUNKS)]


def _dispatch_kernel(offs_ref, ends_ref, pos_ref, x_ref, xs_hbm, xt_ref, zero_ref, sem_zero, sem_rows, *, tm):
    i = pl.program_id(0)
    last = pl.num_programs(0) - 1

    @pl.when(i == 0)
    def _():
        zero_ref[...] = jnp.zeros_like(zero_ref)
        first_unused = ends_ref[N_EXPERTS - 1] // MOE_TILE
        n_tiles = xs_hbm.shape[0] // (MOE_TILE * CHUNKS)

        def clear_tile(start, wait):
            for k in range(MOE_TILE // ZERO_ROWS):
                clear = pltpu.make_async_copy(zero_ref, _token_rows(xs_hbm, start + k * ZERO_ROWS, ZERO_ROWS), sem_zero)
                clear.wait() if wait else clear.start()

        def clear_unused(wait, j, carry):
            clear_tile(j * MOE_TILE, wait)
            return carry

        for wait in (False, True):
            for e in range(N_EXPERTS):
                pl.when(ends_ref[e] > offs_ref[e])(functools.partial(clear_tile, ends_ref[e] - MOE_TILE, wait))
            lax.fori_loop(first_unused, n_tiles, functools.partial(clear_unused, wait), 0)

    def wait_rows(slot):
        for _ in range(2):
            pltpu.make_async_copy(xt_ref.at[slot], _token_rows(xs_hbm, 0, tm), sem_rows.at[slot]).wait()

    def step(slot):
        xt = xt_ref.at[slot]
        _store_token_tiles(xt, x_ref[...])
        base = i * (2 * tm)

        def send(grp, carry):
            ts = [grp * DMA_GROUP + j for j in range(DMA_GROUP)]
            dst = [(pos_ref[base + t], pos_ref[base + tm + t]) for t in ts]
            for t, (p1, p2) in zip(ts, dst):
                src = _token_rows(xt, t, 1)
                pltpu.make_async_copy(src, _token_rows(xs_hbm, p1, 1), sem_rows.at[slot]).start(priority=0)
                pltpu.make_async_copy(src, _token_rows(xs_hbm, p2, 1), sem_rows.at[slot]).start(priority=1)
            return carry
        lax.fori_loop(0, tm // DMA_GROUP, send, 0)

        @pl.when(i > 0)
        def _():
            wait_rows(1 - slot)

        @pl.when(i == last)
        def _():
            wait_rows(slot)

    for slot in range(2):
        pl.when(i % 2 == slot)(functools.partial(step, slot))


def _dispatch(x, pos, offs, ends, max_tiles, *, tm):
    n = x.shape[0]
    kern = functools.partial(_dispatch_kernel, tm=tm)
    return pl.pallas_call(
        kern,
        out_shape=jax.ShapeDtypeStruct((max_tiles * MOE_TILE * CHUNKS, LANES), F32),
        grid_spec=pltpu.PrefetchScalarGridSpec(
            num_scalar_prefetch=3,
            grid=(n // tm,),
            in_specs=[pl.BlockSpec((tm, D_MODEL), lambda i, offs, ends, pos: (i, 0))],
            out_specs=pl.BlockSpec(memory_space=pl.ANY),
            scratch_shapes=[pltpu.VMEM((2, tm * CHUNKS, LANES), F32), pltpu.VMEM((ZERO_ROWS * CHUNKS, LANES), F32),
                            pltpu.SemaphoreType.DMA, pltpu.SemaphoreType.DMA((2,))]),
        compiler_params=_params(("arbitrary",)),
        name="moe_dispatch",
    )(offs, ends, pos, x)


def _expert_kernel(te_ref, nu_ref, x_ref, wg_ref, wu_ref, wd_ref, o_ref, xb_ref, acc_ref):
    i = pl.program_id(0)
    f = pl.program_id(1)
    last_f = pl.num_programs(1) - 1

    def step(first, last):
        if first:
            xb = _load_token_tiles(x_ref, MOE_TILE).astype(BF16)
            xb_ref[...] = xb
        else:
            xb = xb_ref[...]
        gate = _dot(xb, wg_ref[...].astype(BF16))
        up = _dot(xb, wu_ref[...].astype(BF16))
        h = gate * jax.nn.sigmoid(gate) * up
        part = _dot(h.astype(BF16), wd_ref[...].astype(BF16))
        acc = part if first else acc_ref[...] + part
        if last:
            _store_token_tiles(o_ref, acc)
        else:
            acc_ref[...] = acc

    @pl.when(i < nu_ref[0])
    def _():
        pl.when(f == 0)(functools.partial(step, True, False))
        pl.when((f > 0) & (f < last_f))(functools.partial(step, False, False))
        pl.when(f == last_f)(functools.partial(step, False, True))

    @pl.when((i >= nu_ref[0]) & (f == 0))
    def _():
        o_ref[...] = jnp.zeros_like(o_ref)


def _experts(xs, tile_expert, n_used, w_gate, w_up, w_down, max_tiles, *, tf=512):
    d_ff = w_gate.shape[2]
    n_f = d_ff // tf

    def used_tile(i, nu):
        return jnp.minimum(i, jnp.maximum(nu[0] - 1, 0))

    def row_map(i, f, te, nu):
        return used_tile(i, nu), 0

    def out_map(i, f, te, nu):
        return i, 0

    def up_map(i, f, te, nu):
        return te[used_tile(i, nu)], 0, jnp.where(i < nu[0], f, n_f - 1)

    def down_map(i, f, te, nu):
        return te[used_tile(i, nu)], jnp.where(i < nu[0], f, n_f - 1), 0

    return pl.pallas_call(
        _expert_kernel,
        out_shape=jax.ShapeDtypeStruct((max_tiles * MOE_TILE * CHUNKS, LANES), F32),
        grid_spec=pltpu.PrefetchScalarGridSpec(
            num_scalar_prefetch=2,
            grid=(max_tiles, n_f),
            in_specs=[pl.BlockSpec((MOE_TILE * CHUNKS, LANES), row_map),
                      pl.BlockSpec((None, D_MODEL, tf), up_map),
                      pl.BlockSpec((None, D_MODEL, tf), up_map),
                      pl.BlockSpec((None, tf, D_MODEL), down_map)],
            out_specs=pl.BlockSpec((MOE_TILE * CHUNKS, LANES), out_map),
            scratch_shapes=[pltpu.VMEM((MOE_TILE, D_MODEL), BF16), pltpu.VMEM((MOE_TILE, D_MODEL), F32)]),
        compiler_params=_params(("arbitrary", "arbitrary")),
        name="moe_experts",
    )(tile_expert, n_used, xs, w_gate, w_up, w_down)


def _combine_kernel(pos_ref, x_ref, meta_ref, y_hbm, g_ref, b_ref, o_ref, buf_ref, sem_rows, *, tm):
    i = pl.program_id(0)
    n_steps = pl.num_programs(0)

    def fetch_rows(tile, slot):
        base = tile * (2 * tm)

        def fetch(grp, carry):
            ts = [grp * DMA_GROUP + j for j in range(DMA_GROUP)]
            src = [(pos_ref[base + t], pos_ref[base + tm + t]) for t in ts]
            for t, (p1, p2) in zip(ts, src):
                pltpu.make_async_copy(_token_rows(y_hbm, p1, 1), _token_rows(buf_ref.at[slot, 0], t, 1),
                                      sem_rows.at[slot]).start(priority=0)
                pltpu.make_async_copy(_token_rows(y_hbm, p2, 1), _token_rows(buf_ref.at[slot, 1], t, 1),
                                      sem_rows.at[slot]).start(priority=1)
            return carry
        lax.fori_loop(0, tm // DMA_GROUP, fetch, 0)

    @pl.when(i == 0)
    def _():
        fetch_rows(0, 0)

    def step(slot):
        @pl.when(i + 1 < n_steps)
        def _():
            fetch_rows(i + 1, 1 - slot)

        for k in range(2):
            pltpu.make_async_copy(_token_rows(y_hbm, 0, tm), buf_ref.at[slot, k], sem_rows.at[slot]).wait()
        meta = meta_ref[...]
        lane = lax.broadcasted_iota(jnp.int32, meta.shape, 1)
        g1 = jnp.sum(jnp.where(lane == META_G1, meta, 0.0), axis=1, keepdims=True)
        g2 = jnp.sum(jnp.where(lane == META_G2, meta, 0.0), axis=1, keepdims=True)
        mix = g1 * _load_token_tiles(buf_ref.at[slot, 0], tm) + g2 * _load_token_tiles(buf_ref.at[slot, 1], tm)
        o_ref[...] = _layer_norm(DN_ALPHA * x_ref[...] + mix, g_ref[...], b_ref[...])

    for slot in range(2):
        pl.when(i % 2 == slot)(functools.partial(step, slot))


def _combine(x, meta, y, pos, g, b, *, tm):
    n = x.shape[0]
    kern = functools.partial(_combine_kernel, tm=tm)
    vec = pl.BlockSpec((1, D_MODEL), lambda i, pos: (0, 0))
    return pl.pallas_call(
        kern,
        out_shape=jax.ShapeDtypeStruct((n, D_MODEL), F32),
        grid_spec=pltpu.PrefetchScalarGridSpec(
            num_scalar_prefetch=1,
            grid=(n // tm,),
            in_specs=[pl.BlockSpec((tm, D_MODEL), lambda i, pos: (i, 0)),
                      pl.BlockSpec((tm, LANES), lambda i, pos: (i, 0)),
                      pl.BlockSpec(memory_space=pl.ANY),
                      vec, vec],
            out_specs=pl.BlockSpec((tm, D_MODEL), lambda i, pos: (i, 0)),
            scratch_shapes=[pltpu.VMEM((2, 2, tm * CHUNKS, LANES), F32), pltpu.SemaphoreType.DMA((2,))]),
        compiler_params=_params(("arbitrary",)),
        name="moe_combine_ln",
    )(pos, x, meta, y, _row(g), _row(b))


def _moe_layer(x, w_router, w_gate, w_up, w_down, g, b, *, tm=512):
    meta, counts = _router(x, w_router)
    pos, offs, ends, tile_expert, n_used, max_tiles = _routing_tables(meta, counts, tm)
    xs = _dispatch(x, pos, offs, ends, max_tiles, tm=tm)
    y = _experts(xs, tile_expert, n_used, w_gate, w_up, w_down, max_tiles)
    return _combine(x, meta, y, pos, g, b, tm=tm)


def _rope_kernel(pos_ref, invf_ref, c_ref, s1_ref, s2_ref):
    ang = pos_ref[...].astype(F32) * invf_ref[...]
    c = jnp.cos(ang)
    s = jnp.sin(ang)
    dd = lax.broadcasted_iota(jnp.int32, ang.shape, 1) % HEAD_DIM
    c_ref[...] = c
    s1_ref[...] = jnp.where(dd < ROT_DIM // 2, -s, 0.0)
    s2_ref[...] = jnp.where((dd >= ROT_DIM // 2) & (dd < ROT_DIM), s, 0.0)


def _rope_tables(positions, *, tm=2048):
    n = positions.size
    half = ROT_DIM // 2
    inv_freq = ROPE_THETA ** (-(jnp.arange(0, ROT_DIM, 2, dtype=F32) / ROT_DIM))
    per_head = jnp.concatenate([inv_freq, inv_freq, jnp.zeros((HEAD_DIM - 2 * half,), F32)])
    invf = jnp.tile(per_head, LANES // HEAD_DIM).reshape(1, LANES)
    out = jax.ShapeDtypeStruct((n, LANES), F32)
    spec = pl.BlockSpec((tm, LANES), lambda i: (i, 0))
    return pl.pallas_call(
        _rope_kernel,
        out_shape=(out, out, out),
        grid=(n // tm,),
        in_specs=[pl.BlockSpec((tm, 1), lambda i: (i, 0)), pl.BlockSpec((1, LANES), lambda i: (0, 0))],
        out_specs=(spec, spec, spec),
        compiler_params=_params(("arbitrary",)),
        name="rope_tables",
    )(positions.reshape(n, 1), invf)


QKV_TILE = 512


def _qkv_kernel(x_ref, c_ref, s1_ref, s2_ref, w_f32, o_ref, w_ref, *, dil):
    _cast_once(w_ref, w_f32)
    tm = x_ref.shape[0]
    chunk = tm // dil
    xb = x_ref[...].astype(BF16)
    if dil == 1:
        c, s1, s2 = c_ref[...], s1_ref[...], s2_ref[...]
    else:
        dst = lax.broadcasted_iota(jnp.int32, (tm, tm), 0)
        tok = lax.broadcasted_iota(jnp.int32, (tm, tm), 1)
        perm = (tok == (dst % chunk) * dil + dst // chunk).astype(BF16)
        xb = _dot(perm, xb).astype(BF16)
        c, s1, s2 = (jnp.concatenate([t[pl.ds(r, chunk, stride=dil), :] for r in range(dil)], axis=0)
                     for t in (c_ref, s1_ref, s2_ref))
    y = _dot(xb, w_ref[...])
    for part in range(2):
        scale = HEAD_DIM ** -0.5 * LOG2_E if part == 0 else 1.0
        for blk in range(D_MODEL // LANES):
            lo = part * D_MODEL + blk * LANES
            t = y[:, lo:lo + LANES]
            rot = t * c + pltpu.roll(t, LANES - ROT_DIM // 2, 1) * s1 + pltpu.roll(t, ROT_DIM // 2, 1) * s2
            o_ref[:, lo:lo + LANES] = (rot * scale).astype(BF16)
    o_ref[:, 2 * D_MODEL:] = y[:, 2 * D_MODEL:].astype(BF16)


def _qkv_group(x, tabs, w_qkv, grp, dil):
    n = x.shape[0]
    tm = QKV_TILE
    kern = functools.partial(_qkv_kernel, dil=dil)
    tab_spec = pl.BlockSpec((tm, LANES), lambda i: (i, 0))
    return pl.pallas_call(
        kern,
        out_shape=jax.ShapeDtypeStruct((n, 3 * D_MODEL), BF16),
        grid=(n // tm,),
        in_specs=[pl.BlockSpec((tm, D_MODEL), lambda i: (i, 0)),
                  tab_spec, tab_spec, tab_spec,
                  _resident((D_MODEL, 3 * D_MODEL), lambda i: (0, grp))],
        out_specs=pl.BlockSpec((tm, 3 * D_MODEL), lambda i: (i, 0)),
        scratch_shapes=[pltpu.VMEM((D_MODEL, 3 * D_MODEL), BF16)],
        compiler_params=_params(("arbitrary",)),
        name="qkv_proj_dil%d" % dil,
    )(x, *tabs, w_qkv)


def _attn_kernel(*refs):
    qkv = refs[:9]
    o_ref = refs[9]
    acc_s, m_s, l_s = refs[10:13], refs[13:16], refs[16:19]

    lane = lax.broadcasted_iota(jnp.int32, (1, LANES), 1)
    head0 = lane < HEAD_DIM
    hm0 = head0.astype(BF16)
    hm1 = 1.0 - hm0
    qi = lax.broadcasted_iota(jnp.int32, (ATTN_BLOCK, ATTN_BLOCK), 0)
    kj = lax.broadcasted_iota(jnp.int32, (ATTN_BLOCK, ATTN_BLOCK), 1)
    cur_mask = jnp.where(qi <= kj, 0.0, NEG_BIG).astype(BF16)
    prev_mask = jnp.where(qi >= kj, 0.0, NEG_BIG).astype(BF16)
    mask_both = jnp.concatenate([prev_mask, cur_mask], axis=0)
    row_onehot = (qi == kj).astype(BF16)
    row_onehot = jnp.concatenate([row_onehot, row_onehot], axis=0)

    def block_rows(ref, dil, r, nb):
        chunk = QKV_TILE // dil
        if chunk >= ATTN_BLOCK:
            first = nb * ATTN_BLOCK
            base = first // chunk * QKV_TILE + r * chunk + first % chunk
            return ref[base:base + ATTN_BLOCK, :]
        pieces = ATTN_BLOCK // chunk
        starts = [(nb * pieces + m) * QKV_TILE + r * chunk for m in range(pieces)]
        return jnp.concatenate([ref[s:s + chunk, :] for s in starts], axis=0)

    def scores(grp, dil, r, nb):
        q_ref, k_ref, v_ref = qkv[3 * grp:3 * grp + 3]
        q = block_rows(q_ref, dil, r, nb)
        q2 = jnp.concatenate([q * hm0, q * hm1], axis=0)
        q2 = jnp.concatenate([q2, row_onehot], axis=1)
        kk = block_rows(k_ref, dil, r, nb)
        vv = block_rows(v_ref, dil, r, nb)
        if nb > 0:
            kk = jnp.concatenate([block_rows(k_ref, dil, r, nb - 1), kk], axis=0)
            vv = jnp.concatenate([block_rows(v_ref, dil, r, nb - 1), vv], axis=0)
            kk = jnp.concatenate([kk, mask_both], axis=1)
        else:
            kk = jnp.concatenate([kk, cur_mask], axis=1)
        s = lax.dot_general(q2, kk, (((1,), (1,)), ((), ())), preferred_element_type=F32)
        return s, vv

    def finish(grp, s, vv, nat_start, dil):
        m = jnp.max(s, axis=1, keepdims=True)
        pb = jnp.exp2(s - m).astype(BF16)
        ones = jnp.ones_like(vv)
        o0 = _dot(pb[:ATTN_BLOCK], jnp.concatenate([vv * hm0, ones], axis=1))
        o1 = _dot(pb[ATTN_BLOCK:], jnp.concatenate([vv * hm1, ones], axis=1))
        acc = o0[:, :LANES] + o1[:, :LANES]
        mb = jnp.where(head0, m[:ATTN_BLOCK], m[ATTN_BLOCK:])
        lb = jnp.where(head0, o0[:, LANES:], o1[:, LANES:])
        if dil == 1:
            rows = pl.ds(nat_start, ATTN_BLOCK)
        else:
            rows = pl.ds(nat_start, ATTN_BLOCK, stride=dil)
        acc_s[grp][rows, :] = acc
        m_s[grp][rows, :] = mb
        l_s[grp][rows, :] = lb

    for grp, dil in enumerate(ATTN_DILATIONS):
        n_blocks = SEQ // dil // ATTN_BLOCK
        blocks = [(r, 0) for r in range(dil)] + [(r, nb) for r in range(dil) for nb in range(1, n_blocks)]
        for lo in range(0, len(blocks), ATTN_UNROLL):
            batch = blocks[lo:lo + ATTN_UNROLL]
            staged = [scores(grp, dil, r, nb) for r, nb in batch]
            for (r, nb), (s, vv) in zip(batch, staged):
                finish(grp, s, vv, nb * ATTN_BLOCK * dil + r, dil)

    def merge(c, carry):
        rows = pl.ds(pl.multiple_of(c * ATTN_BLOCK, ATTN_BLOCK), ATTN_BLOCK)
        ms = [m_s[g][rows, :] for g in range(3)]
        top = jnp.maximum(jnp.maximum(ms[0], ms[1]), ms[2])
        num = jnp.zeros((ATTN_BLOCK, LANES), F32)
        den = jnp.zeros((ATTN_BLOCK, LANES), F32)
        for g in range(3):
            w = jnp.exp2(ms[g] - top)
            num = num + w * acc_s[g][rows, :]
            den = den + w * l_s[g][rows, :]
        o_ref[rows, :] = (num / den).astype(BF16)
        return carry
    lax.fori_loop(0, SEQ // ATTN_BLOCK, merge, 0)


def _attention(qkvs, n):
    n_pairs = D_MODEL // LANES
    in_specs, args = [], []
    for qkv in qkvs:
        for part in range(3):
            in_specs.append(pl.BlockSpec((SEQ, LANES), lambda b, hp, part=part: (b, part * n_pairs + hp)))
            args.append(qkv)
    scratch = [pltpu.VMEM((SEQ, LANES), F32) for _ in range(9)]
    return pl.pallas_call(
        _attn_kernel,
        out_shape=jax.ShapeDtypeStruct((n, D_MODEL), BF16),
        grid=(n // SEQ, n_pairs),
        in_specs=in_specs,
        out_specs=pl.BlockSpec((SEQ, LANES), lambda b, hp: (b, hp)),
        scratch_shapes=scratch,
        compiler_params=_params(("arbitrary", "arbitrary")),
        name="dilated_attention",
    )(*args)


def _proj_ln_kernel(x_ref, a_ref, w_f32, g_ref, b_ref, o_ref, w_ref):
    _cast_once(w_ref, w_f32)
    h = _dot(a_ref[...], w_ref[...])
    o_ref[...] = _layer_norm(DN_ALPHA * x_ref[...] + h, g_ref[...], b_ref[...])


def _proj_ln(x, a, w, g, b, *, tm=512):
    n = x.shape[0]
    vec = pl.BlockSpec((1, D_MODEL), lambda i: (0, 0))
    return pl.pallas_call(
        _proj_ln_kernel,
        out_shape=jax.ShapeDtypeStruct((n, D_MODEL), F32),
        grid=(n // tm,),
        in_specs=[pl.BlockSpec((tm, D_MODEL), lambda i: (i, 0)),
                  pl.BlockSpec((tm, D_MODEL), lambda i: (i, 0)),
                  _resident((D_MODEL, D_MODEL), lambda i: (0, 0)),
                  vec, vec],
        out_specs=pl.BlockSpec((tm, D_MODEL), lambda i: (i, 0)),
        scratch_shapes=[pltpu.VMEM((D_MODEL, D_MODEL), BF16)],
        compiler_params=_params(("arbitrary",)),
        name="out_proj_ln",
    )(x, a, w, _row(g), _row(b))


def _attn_layer(x, tabs, w_qkv, w_o, g, b):
    n = x.shape[0]
    qkvs = [_qkv_group(x, tabs, w_qkv, grp, dil) for grp, dil in enumerate(ATTN_DILATIONS)]
    return _proj_ln(x, _attention(qkvs, n), w_o, g, b)


def kernel(x, positions, l0_pool_w_in, l0_pool_w_grp, l0_pool_scale, l0_ln1_g, l0_ln1_b, l0_ffn_w_gate, l0_ffn_w_up, l0_ffn_w_down, l0_ln2_g, l0_ln2_b, l1_attn_w_qkv, l1_attn_w_o, l1_ln1_g, l1_ln1_b, l1_moe_w_router, l1_moe_w_gate, l1_moe_w_up, l1_moe_w_down, l1_ln2_g, l1_ln2_b, l2_conv_w_in, l2_conv_w, l2_conv_w_out, l2_ln1_g, l2_ln1_b, l2_ffn_w_gate, l2_ffn_w_up, l2_ffn_w_down, l2_ln2_g, l2_ln2_b, l3_pool_w_in, l3_pool_w_grp, l3_pool_scale, l3_ln1_g, l3_ln1_b, l3_moe_w_router, l3_moe_w_gate, l3_moe_w_up, l3_moe_w_down, l3_ln2_g, l3_ln2_b):
    batch, seq, d = x.shape
    h = x.reshape(batch * seq, d)
    tabs = _rope_tables(positions)
    h = _pool_layer(h, l0_pool_w_in, l0_pool_w_grp, l0_pool_scale, l0_ln1_g, l0_ln1_b)
    h = _ffn_layer(h, l0_ffn_w_gate, l0_ffn_w_up, l0_ffn_w_down, l0_ln2_g, l0_ln2_b)
    h = _attn_layer(h, tabs, l1_attn_w_qkv, l1_attn_w_o, l1_ln1_g, l1_ln1_b)
    h = _moe_layer(h, l1_moe_w_router, l1_moe_w_gate, l1_moe_w_up, l1_moe_w_down, l1_ln2_g, l1_ln2_b)
    h = _conv_layer(h, l2_conv_w_in, l2_conv_w, l2_conv_w_out, l2_ln1_g, l2_ln1_b)
    h = _ffn_layer(h, l2_ffn_w_gate, l2_ffn_w_up, l2_ffn_w_down, l2_ln2_g, l2_ln2_b)
    h = _pool_layer(h, l3_pool_w_in, l3_pool_w_grp, l3_pool_scale, l3_ln1_g, l3_ln1_b)
    h = _moe_layer(h, l3_moe_w_router, l3_moe_w_gate, l3_moe_w_up, l3_moe_w_down, l3_ln2_g, l3_ln2_b)
    return h.reshape(batch, seq, d)
```

```python
import functools

import jax
import jax.numpy as jnp
from jax import lax
from jax.experimental import pallas as pl
from jax.experimental.pallas import tpu as pltpu

D_MODEL = 1024
SEQ = 2048
DEPTH = 4
POOL_WINDOWS = (2, 4, 8, 16)
POOL_GROUP_DIM = D_MODEL // len(POOL_WINDOWS)
ATTN_DILATIONS = (1, 4, 16)
ATTN_BLOCK = 128
ATTN_UNROLL = 8
HEAD_DIM = 64
ROT_DIM = HEAD_DIM // 4
ROPE_THETA = 500000.0
LOG2_E = 1.4426950408889634
CONV_WIDTH = 3
N_EXPERTS = 8
MOE_TILE = 1024
META_E1, META_E2, META_R1, META_R2, META_G1, META_G2 = range(6)
DN_ALPHA = (2 * DEPTH) ** 0.25
LN_EPS = 1e-5

LANES = 128
SUBLANES = 8
HALO = 16
NEG_BIG = -1e30
VMEM_LIMIT = 56 * 1024 * 1024

F32 = jnp.float32
BF16 = jnp.bfloat16


def _params(semantics, vmem=VMEM_LIMIT):
    return pltpu.CompilerParams(dimension_semantics=semantics, vmem_limit_bytes=vmem)


def _dot(a, b):
    return jnp.dot(a, b, preferred_element_type=F32)


def _layer_norm(z, g, b):
    mu = jnp.mean(z, axis=-1, keepdims=True)
    zc = z - mu
    var = jnp.mean(zc * zc, axis=-1, keepdims=True)
    return zc * lax.rsqrt(var + LN_EPS) * g + b


def _row(v):
    return v.reshape(1, -1)


def _resident(shape, index_map):
    return pl.BlockSpec(shape, index_map, pipeline_mode=pl.Buffered(1))


def _cast_once(dst_ref, src_ref):
    @pl.when(pl.program_id(0) == 0)
    def _():
        dst_ref[...] = src_ref[...].astype(BF16)


def _pool_kernel(x_ref, w_in_f32, w_grp_f32, scale_ref, g_ref, b_ref, o_ref, halo_ref, w_in_ref, w_grp_ref, *, tm, tiles_per_seq):
    i = pl.program_id(0)
    _cast_once(w_in_ref, w_in_f32)
    _cast_once(w_grp_ref, w_grp_f32)
    x = x_ref[...]
    u = _dot(x.astype(BF16), w_in_ref[...])

    @pl.when(i % tiles_per_seq == 0)
    def _():
        halo_ref[...] = jnp.zeros_like(halo_ref)

    buf = jnp.concatenate([halo_ref[...], u], axis=0)
    halo_ref[...] = u[tm - HALO:, :]
    t = (i % tiles_per_seq) * tm + lax.broadcasted_iota(jnp.int32, (tm, 1), 0)
    outs = []
    for grp, w in enumerate(POOL_WINDOWS):
        cols = slice(grp * POOL_GROUP_DIM, (grp + 1) * POOL_GROUP_DIM)
        s = buf[:, cols]
        k = 1
        while k < w:
            s = s + pltpu.roll(s, k, 0)
            k *= 2
        cnt = jnp.minimum(t + 1, w).astype(F32)
        pooled = s[HALO:, :] / cnt - u[:, cols]
        outs.append(_dot(pooled.astype(BF16), w_grp_ref[grp]))
    h = jnp.concatenate(outs, axis=1) * scale_ref[...]
    o_ref[...] = _layer_norm(DN_ALPHA * x + h, g_ref[...], b_ref[...])


def _pool_layer(x, w_in, w_grp, scale, g, b, *, tm=1024):
    n = x.shape[0]
    kern = functools.partial(_pool_kernel, tm=tm, tiles_per_seq=SEQ // tm)
    vec = pl.BlockSpec((1, D_MODEL), lambda i: (0, 0))
    return pl.pallas_call(
        kern,
        out_shape=jax.ShapeDtypeStruct((n, D_MODEL), F32),
        grid=(n // tm,),
        in_specs=[pl.BlockSpec((tm, D_MODEL), lambda i: (i, 0)),
                  _resident((D_MODEL, D_MODEL), lambda i: (0, 0)),
                  _resident((len(POOL_WINDOWS), POOL_GROUP_DIM, POOL_GROUP_DIM), lambda i: (0, 0, 0)),
                  vec, vec, vec],
        out_specs=pl.BlockSpec((tm, D_MODEL), lambda i: (i, 0)),
        scratch_shapes=[pltpu.VMEM((HALO, D_MODEL), F32), pltpu.VMEM((D_MODEL, D_MODEL), BF16),
                        pltpu.VMEM((len(POOL_WINDOWS), POOL_GROUP_DIM, POOL_GROUP_DIM), BF16)],
        compiler_params=_params(("arbitrary",)),
        name="pool_mixer_ln",
    )(x, w_in, w_grp, _row(scale), _row(g), _row(b))


def _conv_kernel(x_ref, w_in_f32, cw_ref, w_out_f32, g_ref, b_ref, o_ref, halo_ref, w_in_ref, w_out_ref, *, tm, tiles_per_seq):
    i = pl.program_id(0)
    _cast_once(w_in_ref, w_in_f32)
    _cast_once(w_out_ref, w_out_f32)
    x = x_ref[...]
    proj = _dot(x.astype(BF16), w_in_ref[...])
    gate_b = proj[:, :D_MODEL]
    z = proj[:, D_MODEL:2 * D_MODEL] * proj[:, 2 * D_MODEL:]

    @pl.when(i % tiles_per_seq == 0)
    def _():
        halo_ref[...] = jnp.zeros_like(halo_ref)

    buf = jnp.concatenate([halo_ref[...], z], axis=0)
    halo_ref[...] = z[tm - HALO:, :]
    conv = cw_ref[0:1, :] * z
    for j in range(1, CONV_WIDTH):
        conv = conv + cw_ref[j:j + 1, :] * pltpu.roll(buf, j, 0)[HALO:, :]
    h = _dot((gate_b * conv).astype(BF16), w_out_ref[...])
    o_ref[...] = _layer_norm(DN_ALPHA * x + h, g_ref[...], b_ref[...])


def _conv_layer(x, w_in, conv_w, w_out, g, b, *, tm=512):
    n = x.shape[0]
    kern = functools.partial(_conv_kernel, tm=tm, tiles_per_seq=SEQ // tm)
    vec = pl.BlockSpec((1, D_MODEL), lambda i: (0, 0))
    return pl.pallas_call(
        kern,
        out_shape=jax.ShapeDtypeStruct((n, D_MODEL), F32),
        grid=(n // tm,),
        in_specs=[pl.BlockSpec((tm, D_MODEL), lambda i: (i, 0)),
                  _resident((D_MODEL, 3 * D_MODEL), lambda i: (0, 0)),
                  pl.BlockSpec((CONV_WIDTH, D_MODEL), lambda i: (0, 0)),
                  _resident((D_MODEL, D_MODEL), lambda i: (0, 0)),
                  vec, vec],
        out_specs=pl.BlockSpec((tm, D_MODEL), lambda i: (i, 0)),
        scratch_shapes=[pltpu.VMEM((HALO, D_MODEL), F32), pltpu.VMEM((D_MODEL, 3 * D_MODEL), BF16),
                        pltpu.VMEM((D_MODEL, D_MODEL), BF16)],
        compiler_params=_params(("arbitrary",)),
        name="conv_mixer_ln",
    )(x, w_in, conv_w, w_out, _row(g), _row(b))


FFN_CHUNK = 256


def _ffn_kernel(x_ref, wg_ref, wu_ref, wd_ref, g_ref, b_ref, o_ref):
    x = x_ref[...]
    xb = x.astype(BF16)
    acc = jnp.zeros(x.shape, F32)
    for lo in range(0, wg_ref.shape[1], FFN_CHUNK):
        gate = _dot(xb, wg_ref[:, lo:lo + FFN_CHUNK].astype(BF16))
        up = _dot(xb, wu_ref[:, lo:lo + FFN_CHUNK].astype(BF16))
        h = gate * jax.nn.sigmoid(gate) * up
        acc = acc + _dot(h.astype(BF16), wd_ref[lo:lo + FFN_CHUNK, :].astype(BF16))
    o_ref[...] = _layer_norm(DN_ALPHA * x + acc, g_ref[...], b_ref[...])


def _ffn_layer(x, w_gate, w_up, w_down, g, b, *, tm=512):
    n = x.shape[0]
    d_ff = w_gate.shape[1]
    vec = pl.BlockSpec((1, D_MODEL), lambda i: (0, 0))
    once = pl.Buffered(1)
    return pl.pallas_call(
        _ffn_kernel,
        out_shape=jax.ShapeDtypeStruct((n, D_MODEL), F32),
        grid=(n // tm,),
        in_specs=[pl.BlockSpec((tm, D_MODEL), lambda i: (i, 0)),
                  pl.BlockSpec((D_MODEL, d_ff), lambda i: (0, 0), pipeline_mode=once),
                  pl.BlockSpec((D_MODEL, d_ff), lambda i: (0, 0), pipeline_mode=once),
                  pl.BlockSpec((d_ff, D_MODEL), lambda i: (0, 0), pipeline_mode=once),
                  vec, vec],
        out_specs=pl.BlockSpec((tm, D_MODEL), lambda i: (i, 0)),
        compiler_params=_params(("arbitrary",)),
        name="swiglu_ln",
    )(x, w_gate, w_up, w_down, _row(g), _row(b))


CHUNKS = D_MODEL // LANES


def _store_token_tiles(ref, y):
    for c in range(CHUNKS):
        ref[pl.ds(c, y.shape[0], stride=CHUNKS), :] = y[:, c * LANES:(c + 1) * LANES]


def _load_token_tiles(ref, rows):
    return jnp.concatenate([ref[pl.ds(c, rows, stride=CHUNKS), :] for c in range(CHUNKS)], axis=1)


def _router_kernel(x_ref, w_ref, meta_ref, cnt_ref, run_ref):
    i = pl.program_id(0)

    @pl.when(i == 0)
    def _():
        run_ref[...] = jnp.zeros_like(run_ref)

    x = x_ref[...]
    w = w_ref[...]
    xh = x.astype(BF16)
    xl = (x - xh.astype(F32)).astype(BF16)
    wh = w.astype(BF16)
    wl = (w - wh.astype(F32)).astype(BF16)
    logits = _dot(xh, wh) + (_dot(xl, wh) + _dot(xh, wl))
    tm = logits.shape[0]
    lane = lax.broadcasted_iota(jnp.int32, logits.shape, 1)
    logits = jnp.where(lane < N_EXPERTS, logits, -jnp.inf)
    v1 = jnp.max(logits, axis=1, keepdims=True)
    i1 = jnp.min(jnp.where(logits == v1, lane, LANES), axis=1, keepdims=True)
    rest = jnp.where(lane == i1, -jnp.inf, logits)
    v2 = jnp.max(rest, axis=1, keepdims=True)
    i2 = jnp.min(jnp.where(rest == v2, lane, LANES), axis=1, keepdims=True)
    e2 = jnp.exp(v2 - v1)
    g1 = 1.0 / (1.0 + e2)
    g2 = e2 / (1.0 + e2)

    sel = jnp.where(lane == i1, 1.0, jnp.where(lane == i2, 1.0, 0.0))
    before = (lax.broadcasted_iota(jnp.int32, (tm, tm), 1) < lax.broadcasted_iota(jnp.int32, (tm, tm), 0))
    rank = run_ref[0:1, :] + _dot(before.astype(BF16), sel.astype(BF16))
    r1 = jnp.sum(jnp.where(lane == i1, rank, 0.0), axis=1, keepdims=True)
    r2 = jnp.sum(jnp.where(lane == i2, rank, 0.0), axis=1, keepdims=True)
    run_ref[...] = run_ref[...] + jnp.sum(sel, axis=0, keepdims=True)
    cnt_ref[...] = run_ref[...]
    meta = jnp.zeros_like(logits)
    record = {META_E1: i1.astype(F32), META_E2: i2.astype(F32), META_R1: r1, META_R2: r2, META_G1: g1, META_G2: g2}
    for k, val in record.items():
        meta = jnp.where(lane == k, val, meta)
    meta_ref[...] = meta


def _router(x, w_router, *, tm=512):
    n = x.shape[0]
    w = jnp.pad(w_router, ((0, 0), (0, LANES - N_EXPERTS)))
    return pl.pallas_call(
        _router_kernel,
        out_shape=(jax.ShapeDtypeStruct((n, LANES), F32), jax.ShapeDtypeStruct((SUBLANES, LANES), F32)),
        grid=(n // tm,),
        in_specs=[pl.BlockSpec((tm, D_MODEL), lambda i: (i, 0)),
                  pl.BlockSpec((D_MODEL, LANES), lambda i: (0, 0))],
        out_specs=(pl.BlockSpec((tm, LANES), lambda i: (i, 0)), pl.BlockSpec((SUBLANES, LANES), lambda i: (0, 0))),
        scratch_shapes=[pltpu.VMEM((SUBLANES, LANES), F32)],
        compiler_params=_params(("arbitrary",)),
        name="router_top2",
    )(x, w)


def _routing_tables(meta, counts, tm):
    n = meta.shape[0]
    cnt = counts[0, :N_EXPERTS].astype(jnp.int32)
    padded = (cnt + MOE_TILE - 1) // MOE_TILE * MOE_TILE
    ends = jnp.cumsum(padded)
    offs = ends - padded
    experts = meta[:, META_E1:META_E2 + 1].astype(jnp.int32)
    ranks = meta[:, META_R1:META_R2 + 1].astype(jnp.int32)
    pos = offs[experts] + ranks
    pos = pos.reshape(n // tm, tm, 2).transpose(0, 2, 1).reshape(2 * n)
    max_tiles = (2 * n + N_EXPERTS * (MOE_TILE - 1)) // MOE_TILE
    first_row = jnp.arange(max_tiles, dtype=jnp.int32) * MOE_TILE
    tile_expert = jnp.minimum(jnp.sum(first_row[:, None] >= ends[None, :], axis=1), N_EXPERTS - 1).astype(jnp.int32)
    return pos, offs, ends, tile_expert, ends[-1:] // MOE_TILE, max_tiles


ZERO_ROWS = 128
DMA_GROUP = 16


def _token_rows(ref, start, rows):
    return ref.at[pl.ds(pl.multiple_of(start * CHUNKS, CHUNKS), rows * CHUNKS)]


def _dispatch_kernel(offs_ref, ends_ref, pos_ref, x_ref, xs_hbm, xt_ref, zero_ref, sem_zero, sem_rows, *, tm):
    i = pl.program_id(0)
    last = pl.num_programs(0) - 1

    @pl.when(i == 0)
    def _():
        zero_ref[...] = jnp.zeros_like(zero_ref)
        first_unused = ends_ref[N_EXPERTS - 1] // MOE_TILE
        n_tiles = xs_hbm.shape[0] // (MOE_TILE * CHUNKS)

        def clear_tile(start, wait):
            for k in range(MOE_TILE // ZERO_ROWS):
                clear = pltpu.make_async_copy(zero_ref, _token_rows(xs_hbm, start + k * ZERO_ROWS, ZERO_ROWS), sem_zero)
                clear.wait() if wait else clear.start()

        def clear_unused(wait, j, carry):
            clear_tile(j * MOE_TILE, wait)
            return carry

        for wait in (False, True):
            for e in range(N_EXPERTS):
                pl.when(ends_ref[e] > offs_ref[e])(functools.partial(clear_tile, ends_ref[e] - MOE_TILE, wait))
            lax.fori_loop(first_unused, n_tiles, functools.partial(clear_unused, wait), 0)

    def wait_rows(slot):
        for _ in range(2):
            pltpu.make_async_copy(xt_ref.at[slot], _token_rows(xs_hbm, 0, tm), sem_rows.at[slot]).wait()

    def step(slot):
        xt = xt_ref.at[slot]
        _store_token_tiles(xt, x_ref[...])
        base = i * (2 * tm)

        def send(grp, carry):
            ts = [grp * DMA_GROUP + j for j in range(DMA_GROUP)]
            dst = [(pos_ref[base + t], pos_ref[base + tm + t]) for t in ts]
            for t, (p1, p2) in zip(ts, dst):
                src = _token_rows(xt, t, 1)
                pltpu.make_async_copy(src, _token_rows(xs_hbm, p1, 1), sem_rows.at[slot]).start(priority=0)
                pltpu.make_async_copy(src, _token_rows(xs_hbm, p2, 1), sem_rows.at[slot]).start(priority=1)
            return carry
        lax.fori_loop(0, tm // DMA_GROUP, send, 0)

        @pl.when(i > 0)
        def _():
            wait_rows(1 - slot)

        @pl.when(i == last)
        def _():
            wait_rows(slot)

    for slot in range(2):
        pl.when(i % 2 == slot)(functools.partial(step, slot))


def _dispatch(x, pos, offs, ends, max_tiles, *, tm):
    n = x.shape[0]
    kern = functools.partial(_dispatch_kernel, tm=tm)
    return pl.pallas_call(
        kern,
        out_shape=jax.ShapeDtypeStruct((max_tiles * MOE_TILE * CHUNKS, LANES), F32),
        grid_spec=pltpu.PrefetchScalarGridSpec(
            num_scalar_prefetch=3,
            grid=(n // tm,),
            in_specs=[pl.BlockSpec((tm, D_MODEL), lambda i, offs, ends, pos: (i, 0))],
            out_specs=pl.BlockSpec(memory_space=pl.ANY),
            scratch_shapes=[pltpu.VMEM((2, tm * CHUNKS, LANES), F32), pltpu.VMEM((ZERO_ROWS * CHUNKS, LANES), F32),
                            pltpu.SemaphoreType.DMA, pltpu.SemaphoreType.DMA((2,))]),
        compiler_params=_params(("arbitrary",)),
        name="moe_dispatch",
    )(offs, ends, pos, x)


def _expert_kernel(te_ref, nu_ref, x_ref, wg_ref, wu_ref, wd_ref, o_ref, xb_ref, acc_ref):
    i = pl.program_id(0)
    f = pl.program_id(1)
    last_f = pl.num_programs(1) - 1

    def step(first, last):
        if first:
            xb = _load_token_tiles(x_ref, MOE_TILE).astype(BF16)
            xb_ref[...] = xb
        else:
            xb = xb_ref[...]
        gate = _dot(xb, wg_ref[...].astype(BF16))
        up = _dot(xb, wu_ref[...].astype(BF16))
        h = gate * jax.nn.sigmoid(gate) * up
        part = _dot(h.astype(BF16), wd_ref[...].astype(BF16))
        acc = part if first else acc_ref[...] + part
        if last:
            _store_token_tiles(o_ref, acc)
        else:
            acc_ref[...] = acc

    @pl.when(i < nu_ref[0])
    def _():
        pl.when(f == 0)(functools.partial(step, True, False))
        pl.when((f > 0) & (f < last_f))(functools.partial(step, False, False))
        pl.when(f == last_f)(functools.partial(step, False, True))

    @pl.when((i >= nu_ref[0]) & (f == 0))
    def _():
        o_ref[...] = jnp.zeros_like(o_ref)


def _experts(xs, tile_expert, n_used, w_gate, w_up, w_down, max_tiles, *, tf=512):
    d_ff = w_gate.shape[2]
    n_f = d_ff // tf

    def used_tile(i, nu):
        return jnp.minimum(i, jnp.maximum(nu[0] - 1, 0))

    def row_map(i, f, te, nu):
        return used_tile(i, nu), 0

    def out_map(i, f, te, nu):
        return i, 0

    def up_map(i, f, te, nu):
        return te[used_tile(i, nu)], 0, jnp.where(i < nu[0], f, n_f - 1)

    def down_map(i, f, te, nu):
        return te[used_tile(i, nu)], jnp.where(i < nu[0], f, n_f - 1), 0

    return pl.pallas_call(
        _expert_kernel,
        out_shape=jax.ShapeDtypeStruct((max_tiles * MOE_TILE * CHUNKS, LANES), F32),
        grid_spec=pltpu.PrefetchScalarGridSpec(
            num_scalar_prefetch=2,
            grid=(max_tiles, n_f),
            in_specs=[pl.BlockSpec((MOE_TILE * CHUNKS, LANES), row_map),
                      pl.BlockSpec((None, D_MODEL, tf), up_map),
                      pl.BlockSpec((None, D_MODEL, tf), up_map),
                      pl.BlockSpec((None, tf, D_MODEL), down_map)],
            out_specs=pl.BlockSpec((MOE_TILE * CHUNKS, LANES), out_map),
            scratch_shapes=[pltpu.VMEM((MOE_TILE, D_MODEL), BF16), pltpu.VMEM((MOE_TILE, D_MODEL), F32)]),
        compiler_params=_params(("arbitrary", "arbitrary")),
        name="moe_experts",
    )(tile_expert, n_used, xs, w_gate, w_up, w_down)


def _combine_kernel(pos_ref, x_ref, meta_ref, y_hbm, g_ref, b_ref, o_ref, buf_ref, sem_rows, *, tm):
    i = pl.program_id(0)
    n_steps = pl.num_programs(0)

    def fetch_rows(tile, slot):
        base = tile * (2 * tm)

        def fetch(grp, carry):
            ts = [grp * DMA_GROUP + j for j in range(DMA_GROUP)]
            src = [(pos_ref[base + t], pos_ref[base + tm + t]) for t in ts]
            for t, (p1, p2) in zip(ts, src):
                pltpu.make_async_copy(_token_rows(y_hbm, p1, 1), _token_rows(buf_ref.at[slot, 0], t, 1),
                                      sem_rows.at[slot]).start(priority=0)
                pltpu.make_async_copy(_token_rows(y_hbm, p2, 1), _token_rows(buf_ref.at[slot, 1], t, 1),
                                      sem_rows.at[slot]).start(priority=1)
            return carry
        lax.fori_loop(0, tm // DMA_GROUP, fetch, 0)

    @pl.when(i == 0)
    def _():
        fetch_rows(0, 0)

    def step(slot):
        @pl.when(i + 1 < n_steps)
        def _():
            fetch_rows(i + 1, 1 - slot)

        for k in range(2):
            pltpu.make_async_copy(_token_rows(y_hbm, 0, tm), buf_ref.at[slot, k], sem_rows.at[slot]).wait()
        meta = meta_ref[...]
        lane = lax.broadcasted_iota(jnp.int32, meta.shape, 1)
        g1 = jnp.sum(jnp.where(lane == META_G1, meta, 0.0), axis=1, keepdims=True)
        g2 = jnp.sum(jnp.where(lane == META_G2, meta, 0.0), axis=1, keepdims=True)
        mix = g1 * _load_token_tiles(buf_ref.at[slot, 0], tm) + g2 * _load_token_tiles(buf_ref.at[slot, 1], tm)
        o_ref[...] = _layer_norm(DN_ALPHA * x_ref[...] + mix, g_ref[...], b_ref[...])

    for slot in range(2):
        pl.when(i % 2 == slot)(functools.partial(step, slot))


def _combine(x, meta, y, pos, g, b, *, tm):
    n = x.shape[0]
    kern = functools.partial(_combine_kernel, tm=tm)
    vec = pl.BlockSpec((1, D_MODEL), lambda i, pos: (0, 0))
    return pl.pallas_call(
        kern,
        out_shape=jax.ShapeDtypeStruct((n, D_MODEL), F32),
        grid_spec=pltpu.PrefetchScalarGridSpec(
            num_scalar_prefetch=1,
            grid=(n // tm,),
            in_specs=[pl.BlockSpec((tm, D_MODEL), lambda i, pos: (i, 0)),
                      pl.BlockSpec((tm, LANES), lambda i, pos: (i, 0)),
                      pl.BlockSpec(memory_space=pl.ANY),
                      vec, vec],
            out_specs=pl.BlockSpec((tm, D_MODEL), lambda i, pos: (i, 0)),
            scratch_shapes=[pltpu.VMEM((2, 2, tm * CHUNKS, LANES), F32), pltpu.SemaphoreType.DMA((2,))]),
        compiler_params=_params(("arbitrary",)),
        name="moe_combine_ln",
    )(pos, x, meta, y, _row(g), _row(b))


def _moe_layer(x, w_router, w_gate, w_up, w_down, g, b, *, tm=512):
    meta, counts = _router(x, w_router)
    pos, offs, ends, tile_expert, n_used, max_tiles = _routing_tables(meta, counts, tm)
    xs = _dispatch(x, pos, offs, ends, max_tiles, tm=tm)
    y = _experts(xs, tile_expert, n_used, w_gate, w_up, w_down, max_tiles)
    return _combine(x, meta, y, pos, g, b, tm=tm)


def _rope_kernel(pos_ref, invf_ref, c_ref, s1_ref, s2_ref):
    ang = pos_ref[...].astype(F32) * invf_ref[...]
    c = jnp.cos(ang)
    s = jnp.sin(ang)
    dd = lax.broadcasted_iota(jnp.int32, ang.shape, 1) % HEAD_DIM
    c_ref[...] = c
    s1_ref[...] = jnp.where(dd < ROT_DIM // 2, -s, 0.0)
    s2_ref[...] = jnp.where((dd >= ROT_DIM // 2) & (dd < ROT_DIM), s, 0.0)


def _rope_tables(positions, *, tm=2048):
    n = positions.size
    half = ROT_DIM // 2
    inv_freq = ROPE_THETA ** (-(jnp.arange(0, ROT_DIM, 2, dtype=F32) / ROT_DIM))
    per_head = jnp.concatenate([inv_freq, inv_freq, jnp.zeros((HEAD_DIM - 2 * half,), F32)])
    invf = jnp.tile(per_head, LANES // HEAD_DIM).reshape(1, LANES)
    out = jax.ShapeDtypeStruct((n, LANES), F32)
    spec = pl.BlockSpec((tm, LANES), lambda i: (i, 0))
    return pl.pallas_call(
        _rope_kernel,
        out_shape=(out, out, out),
        grid=(n // tm,),
        in_specs=[pl.BlockSpec((tm, 1), lambda i: (i, 0)), pl.BlockSpec((1, LANES), lambda i: (0, 0))],
        out_specs=(spec, spec, spec),
        compiler_params=_params(("arbitrary",)),
        name="rope_tables",
    )(positions.reshape(n, 1), invf)


QKV_TILE = 512


def _qkv_kernel(x_ref, c_ref, s1_ref, s2_ref, w_f32, o_ref, w_ref, *, dil):
    _cast_once(w_ref, w_f32)
    tm = x_ref.shape[0]
    chunk = tm // dil
    xb = x_ref[...].astype(BF16)
    if dil == 1:
        c, s1, s2 = c_ref[...], s1_ref[...], s2_ref[...]
    else:
        dst = lax.broadcasted_iota(jnp.int32, (tm, tm), 0)
        tok = lax.broadcasted_iota(jnp.int32, (tm, tm), 1)
        perm = (tok == (dst % chunk) * dil + dst // chunk).astype(BF16)
        xb = _dot(perm, xb).astype(BF16)
        c, s1, s2 = (jnp.concatenate([t[pl.ds(r, chunk, stride=dil), :] for r in range(dil)], axis=0)
                     for t in (c_ref, s1_ref, s2_ref))
    y = _dot(xb, w_ref[...])
    for part in range(2):
        scale = HEAD_DIM ** -0.5 * LOG2_E if part == 0 else 1.0
        for blk in range(D_MODEL // LANES):
            lo = part * D_MODEL + blk * LANES
            t = y[:, lo:lo + LANES]
            rot = t * c + pltpu.roll(t, LANES - ROT_DIM // 2, 1) * s1 + pltpu.roll(t, ROT_DIM // 2, 1) * s2
            o_ref[:, lo:lo + LANES] = (rot * scale).astype(BF16)
    o_ref[:, 2 * D_MODEL:] = y[:, 2 * D_MODEL:].astype(BF16)


def _qkv_group(x, tabs, w_qkv, grp, dil):
    n = x.shape[0]
    tm = QKV_TILE
    kern = functools.partial(_qkv_kernel, dil=dil)
    tab_spec = pl.BlockSpec((tm, LANES), lambda i: (i, 0))
    return pl.pallas_call(
        kern,
        out_shape=jax.ShapeDtypeStruct((n, 3 * D_MODEL), BF16),
        grid=(n // tm,),
        in_specs=[pl.BlockSpec((tm, D_MODEL), lambda i: (i, 0)),
                  tab_spec, tab_spec, tab_spec,
                  _resident((D_MODEL, 3 * D_MODEL), lambda i: (0, grp))],
        out_specs=pl.BlockSpec((tm, 3 * D_MODEL), lambda i: (i, 0)),
        scratch_shapes=[pltpu.VMEM((D_MODEL, 3 * D_MODEL), BF16)],
        compiler_params=_params(("arbitrary",)),
        name="qkv_proj_dil%d" % dil,
    )(x, *tabs, w_qkv)


def _attn_kernel(*refs):
    qkv = refs[:9]
    o_ref = refs[9]
    acc_s, m_s, l_s = refs[10:13], refs[13:16], refs[16:19]

    lane = lax.broadcasted_iota(jnp.int32, (1, LANES), 1)
    head0 = lane < HEAD_DIM
    hm0 = head0.astype(BF16)
    hm1 = 1.0 - hm0
    qi = lax.broadcasted_iota(jnp.int32, (ATTN_BLOCK, ATTN_BLOCK), 0)
    kj = lax.broadcasted_iota(jnp.int32, (ATTN_BLOCK, ATTN_BLOCK), 1)
    cur_mask = jnp.where(qi <= kj, 0.0, NEG_BIG).astype(BF16)
    prev_mask = jnp.where(qi >= kj, 0.0, NEG_BIG).astype(BF16)
    mask_both = jnp.concatenate([prev_mask, cur_mask], axis=0)
    row_onehot = (qi == kj).astype(BF16)
    row_onehot = jnp.concatenate([row_onehot, row_onehot], axis=0)

    def block_rows(ref, dil, r, nb):
        chunk = QKV_TILE // dil
        if chunk >= ATTN_BLOCK:
            first = nb * ATTN_BLOCK
            base = first // chunk * QKV_TILE + r * chunk + first % chunk
            return ref[base:base + ATTN_BLOCK, :]
        pieces = ATTN_BLOCK // chunk
        starts = [(nb * pieces + m) * QKV_TILE + r * chunk for m in range(pieces)]
        return jnp.concatenate([ref[s:s + chunk, :] for s in starts], axis=0)

    def scores(grp, dil, r, nb):
        q_ref, k_ref, v_ref = qkv[3 * grp:3 * grp + 3]
        q = block_rows(q_ref, dil, r, nb)
        q2 = jnp.concatenate([q * hm0, q * hm1], axis=0)
        q2 = jnp.concatenate([q2, row_onehot], axis=1)
        kk = block_rows(k_ref, dil, r, nb)
        vv = block_rows(v_ref, dil, r, nb)
        if nb > 0:
            kk = jnp.concatenate([block_rows(k_ref, dil, r, nb - 1), kk], axis=0)
            vv = jnp.concatenate([block_rows(v_ref, dil, r, nb - 1), vv], axis=0)
            kk = jnp.concatenate([kk, mask_both], axis=1)
        else:
            kk = jnp.concatenate([kk, cur_mask], axis=1)
        s = lax.dot_general(q2, kk, (((1,), (1,)), ((), ())), preferred_element_type=F32)
        return s, vv

    def finish(grp, s, vv, nat_start, dil):
        m = jnp.max(s, axis=1, keepdims=True)
        pb = jnp.exp2(s - m).astype(BF16)
        ones = jnp.ones_like(vv)
        o0 = _dot(pb[:ATTN_BLOCK], jnp.concatenate([vv * hm0, ones], axis=1))
        o1 = _dot(pb[ATTN_BLOCK:], jnp.concatenate([vv * hm1, ones], axis=1))
        acc = o0[:, :LANES] + o1[:, :LANES]
        mb = jnp.where(head0, m[:ATTN_BLOCK], m[ATTN_BLOCK:])
        lb = jnp.where(head0, o0[:, LANES:], o1[:, LANES:])
        if dil == 1:
            rows = pl.ds(nat_start, ATTN_BLOCK)
        else:
            rows = pl.ds(nat_start, ATTN_BLOCK, stride=dil)
        acc_s[grp][rows, :] = acc
        m_s[grp][rows, :] = mb
        l_s[grp][rows, :] = lb

    for grp, dil in enumerate(ATTN_DILATIONS):
        n_blocks = SEQ // dil // ATTN_BLOCK
        blocks = [(r, 0) for r in range(dil)] + [(r, nb) for r in range(dil) for nb in range(1, n_blocks)]
        for lo in range(0, len(blocks), ATTN_UNROLL):
            batch = blocks[lo:lo + ATTN_UNROLL]
            staged = [scores(grp, dil, r, nb) for r, nb in batch]
            for (r, nb), (s, vv) in zip(batch, staged):
                finish(grp, s, vv, nb * ATTN_BLOCK * dil + r, dil)

    def merge(c, carry):
        rows = pl.ds(pl.multiple_of(c * ATTN_BLOCK, ATTN_BLOCK), ATTN_BLOCK)
        ms = [m_s[g][rows, :] for g in range(3)]
        top = jnp.maximum(jnp.maximum(ms[0], ms[1]), ms[2])
        num = jnp.zeros((ATTN_BLOCK, LANES), F32)
        den = jnp.zeros((ATTN_BLOCK, LANES), F32)
        for g in range(3):
            w = jnp.exp2(ms[g] - top)
            num = num + w * acc_s[g][rows, :]
            den = den + w * l_s[g][rows, :]
        o_ref[rows, :] = (num / den).astype(BF16)
        return carry
    lax.fori_loop(0, SEQ // ATTN_BLOCK, merge, 0)


def _attention(qkvs, n):
    n_pairs = D_MODEL // LANES
    in_specs, args = [], []
    for qkv in qkvs:
        for part in range(3):
            in_specs.append(pl.BlockSpec((SEQ, LANES), lambda b, hp, part=part: (b, part * n_pairs + hp)))
            args.append(qkv)
    scratch = [pltpu.VMEM((SEQ, LANES), F32) for _ in range(9)]
    return pl.pallas_call(
        _attn_kernel,
        out_shape=jax.ShapeDtypeStruct((n, D_MODEL), BF16),
        grid=(n // SEQ, n_pairs),
        in_specs=in_specs,
        out_specs=pl.BlockSpec((SEQ, LANES), lambda b, hp: (b, hp)),
        scratch_shapes=scratch,
        compiler_params=_params(("arbitrary", "arbitrary")),
        name="dilated_attention",
    )(*args)


def _proj_ln_kernel(x_ref, a_ref, w_f32, g_ref, b_ref, o_ref, w_ref):
    _cast_once(w_ref, w_f32)
    h = _dot(a_ref[...], w_ref[...])
    o_ref[...] = _layer_norm(DN_ALPHA * x_ref[...] + h, g_ref[...], b_ref[...])


def _proj_ln(x, a, w, g, b, *, tm=1024):
    n = x.shape[0]
    vec = pl.BlockSpec((1, D_MODEL), lambda i: (0, 0))
    return pl.pallas_call(
        _proj_ln_kernel,
        out_shape=jax.ShapeDtypeStruct((n, D_MODEL), F32),
        grid=(n // tm,),
        in_specs=[pl.BlockSpec((tm, D_MODEL), lambda i: (i, 0)),
                  pl.BlockSpec((tm, D_MODEL), lambda i: (i, 0)),
                  _resident((D_MODEL, D_MODEL), lambda i: (0, 0)),
                  vec, vec],
        out_specs=pl.BlockSpec((tm, D_MODEL), lambda i: (i, 0)),
        scratch_shapes=[pltpu.VMEM((D_MODEL, D_MODEL), BF16)],
        compiler_params=_params(("arbitrary",)),
        name="out_proj_ln",
    )(x, a, w, _row(g), _row(b))


def _attn_layer(x, tabs, w_qkv, w_o, g, b):
    n = x.shape[0]
    qkvs = [_qkv_group(x, tabs, w_qkv, grp, dil) for grp, dil in enumerate(ATTN_DILATIONS)]
    return _proj_ln(x, _attention(qkvs, n), w_o, g, b)


def kernel(x, positions, l0_pool_w_in, l0_pool_w_grp, l0_pool_scale, l0_ln1_g, l0_ln1_b, l0_ffn_w_gate, l0_ffn_w_up, l0_ffn_w_down, l0_ln2_g, l0_ln2_b, l1_attn_w_qkv, l1_attn_w_o, l1_ln1_g, l1_ln1_b, l1_moe_w_router, l1_moe_w_gate, l1_moe_w_up, l1_moe_w_down, l1_ln2_g, l1_ln2_b, l2_conv_w_in, l2_conv_w, l2_conv_w_out, l2_ln1_g, l2_ln1_b, l2_ffn_w_gate, l2_ffn_w_up, l2_ffn_w_down, l2_ln2_g, l2_ln2_b, l3_pool_w_in, l3_pool_w_grp, l3_pool_scale, l3_ln1_g, l3_ln1_b, l3_moe_w_router, l3_moe_w_gate, l3_moe_w_up, l3_moe_w_down, l3_ln2_g, l3_ln2_b):
    batch, seq, d = x.shape
    h = x.reshape(batch * seq, d)
    tabs = _rope_tables(positions)
    h = _pool_layer(h, l0_pool_w_in, l0_pool_w_grp, l0_pool_scale, l0_ln1_g, l0_ln1_b)
    h = _ffn_layer(h, l0_ffn_w_gate, l0_ffn_w_up, l0_ffn_w_down, l0_ln2_g, l0_ln2_b)
    h = _attn_layer(h, tabs, l1_attn_w_qkv, l1_attn_w_o, l1_ln1_g, l1_ln1_b)
    h = _moe_layer(h, l1_moe_w_router, l1_moe_w_gate, l1_moe_w_up, l1_moe_w_down, l1_ln2_g, l1_ln2_b)
    h = _conv_layer(h, l2_conv_w_in, l2_conv_w, l2_conv_w_out, l2_ln1_g, l2_ln1_b)
    h = _ffn_layer(h, l2_ffn_w_gate, l2_ffn_w_up, l2_ffn_w_down, l2_ln2_g, l2_ln2_b)
    h = _pool_layer(h, l3_pool_w_in, l3_pool_w_grp, l3_pool_scale, l3_ln1_g, l3_ln1_b)
    h = _moe_layer(h, l3_moe_w_router, l3_moe_w_gate, l3_moe_w_up, l3_moe_w_down, l3_ln2_g, l3_ln2_b)
    return h.reshape(batch, seq, d)
```
